```python
import math
import jax
import jax.numpy as jnp
from jax import lax
import numpy as np

D_MODEL = 1024
BATCH = 8
SEQ = 2048
DEPTH = 4
DEC_BATCH = 32
DEC_SEQ = 4
PAST_LEN = 8192
PAGE_SIZE = 128

MIX_WIDTH = D_MODEL
ATTN_WIDTH = MIX_WIDTH // 2
SSM_WIDTH = MIX_WIDTH - ATTN_WIDTH
HEAD_DIM = 64
N_HEADS = ATTN_WIDTH // HEAD_DIM
DILATION_PATTERNS = ((128, 1), (512, 4), (2048, 16))
MAX_WINDOW = max(w for w, _ in DILATION_PATTERNS)
N_BUCKETS = 32
BUCKET_MAX_DIST = MAX_WINDOW
SSM_GROUP = 16
N_SSM_GROUPS = SSM_WIDTH // SSM_GROUP
SSM_STATE = 64
D_FF = ((8 * D_MODEL // 3 + 127) // 128) * 128
N_EXPERTS = 8
TOP_K = 2
EXPERT_FF = D_FF
Q_BLOCK = 128
RMS_EPS = 1e-6
NEG_BIG = -1e30

kernel_name = 'hymba_dilated_s5_moe_step'


def _rmsnorm(x, g):
    xf = x.astype(jnp.float32)
    y = xf * lax.rsqrt(jnp.mean(xf * xf, axis=-1, keepdims=True) + RMS_EPS)
    return (y * g.astype(jnp.float32)).astype(x.dtype)


def _t5_causal_bucket(dist):
    max_exact = N_BUCKETS // 2
    d = np.asarray(dist, dtype=np.int64)
    large = max_exact + (np.log(np.maximum(d, max_exact) / max_exact)
                         / np.log(BUCKET_MAX_DIST / max_exact)
                         * (N_BUCKETS - max_exact)).astype(np.int64)
    large = np.minimum(large, N_BUCKETS - 1)
    return np.where(d < max_exact, d, large).astype(np.int32)


def _dilated_window_attention(q, k_ext, v_ext, q_idx, rel_bias):
    qf = q.astype(jnp.float32) * (HEAD_DIM ** -0.5)
    outs, lses = [], []
    for window, dil in DILATION_PATTERNS:
        dist = (np.arange(window // dil + 1) * dil).astype(np.int32)
        bias = rel_bias[_t5_causal_bucket(dist)].astype(jnp.float32).T
        idx = q_idx[:, None] - dist[None, :]
        valid = idx >= 0
        idx = jnp.maximum(idx, 0)
        kg = k_ext[:, idx].astype(jnp.float32)
        vg = v_ext[:, idx].astype(jnp.float32)
        s = jnp.einsum('bqhd,bqnhd->bqhn', qf, kg) + bias
        s = jnp.where(valid[None, :, None, :], s, NEG_BIG)
        lse = jax.nn.logsumexp(s, axis=-1)
        p = jnp.exp(s - lse[..., None])
        outs.append(jnp.einsum('bqhn,bqnhd->bqhd', p, vg))
        lses.append(lse)
    wts = jax.nn.softmax(jnp.stack(lses, 0), axis=0)
    return jnp.sum(wts[..., None] * jnp.stack(outs, 0), axis=0)


def _prompt_attention(q, k, v, rel_bias):
    B, L, H, hd = q.shape
    nb = L // Q_BLOCK
    qb = q.reshape(B, nb, Q_BLOCK, H, hd).transpose(1, 0, 2, 3, 4)
    idxb = jnp.arange(L, dtype=jnp.int32).reshape(nb, Q_BLOCK)
    ob = lax.map(lambda a: _dilated_window_attention(a[0], k, v, a[1], rel_bias), (qb, idxb))
    return ob.transpose(1, 0, 2, 3, 4).reshape(B, L, H, hd)


def _s5_scan(u, a_re, a_im, log_dt, b_re, b_im, c_re, c_im, d_skip, h0_re, h0_im):
    f32 = jnp.float32
    u = u.astype(f32)
    a_re, a_im = a_re.astype(f32), a_im.astype(f32)
    dt = jnp.exp(log_dt.astype(f32))[:, None]
    mag = jnp.exp(dt * a_re)
    ar, ai = mag * jnp.cos(dt * a_im), mag * jnp.sin(dt * a_im)
    nr, ni = ar - 1.0, ai
    den = a_re * a_re + a_im * a_im
    fr = (nr * a_re + ni * a_im) / den
    fi = (ni * a_re - nr * a_im) / den
    br, bi = b_re.astype(f32), b_im.astype(f32)
    bb_re = fr[..., None] * br - fi[..., None] * bi
    bb_im = fr[..., None] * bi + fi[..., None] * br
    x_re = jnp.einsum('blgc,gpc->blgp', u, bb_re)
    x_im = jnp.einsum('blgc,gpc->blgp', u, bb_im)
    h0_re, h0_im = h0_re.astype(f32), h0_im.astype(f32)
    x_re = x_re.at[:, 0].add(ar * h0_re - ai * h0_im)
    x_im = x_im.at[:, 0].add(ar * h0_im + ai * h0_re)
    el_re = jnp.broadcast_to(ar, x_re.shape)
    el_im = jnp.broadcast_to(ai, x_im.shape)

    def combine(e1, e2):
        a1r, a1i, b1r, b1i = e1
        a2r, a2i, b2r, b2i = e2
        return (a1r * a2r - a1i * a2i, a1r * a2i + a1i * a2r,
                a2r * b1r - a2i * b1i + b2r, a2r * b1i + a2i * b1r + b2i)

    _, _, hr, hi = lax.associative_scan(combine, (el_re, el_im, x_re, x_im), axis=1)
    y = (jnp.einsum('blgp,gcp->blgc', hr, c_re.astype(f32))
         - jnp.einsum('blgp,gcp->blgc', hi, c_im.astype(f32))
         + d_skip.astype(f32) * u)
    return y, hr[:, -1], hi[:, -1]


def _swiglu(x, wg, wu, wd):
    return (jax.nn.silu(x @ wg) * (x @ wu)) @ wd


def _moe(x, w_router, b_router, wg, wu, wd):
    logits = (x @ w_router).astype(jnp.float32) + b_router.astype(jnp.float32)
    top_v, top_i = lax.top_k(logits, TOP_K)
    gates = jax.nn.softmax(top_v, axis=-1)
    comb = jnp.sum(jax.nn.one_hot(top_i, N_EXPERTS, dtype=jnp.float32) * gates[..., None], axis=-2)
    out = jnp.zeros(x.shape, jnp.float32)
    for e in range(N_EXPERTS):
        out = out + comb[..., e:e + 1] * _swiglu(x, wg[e], wu[e], wd[e]).astype(jnp.float32)
    return out


def setup_inputs(seed: int = 0) -> dict:
    key = jax.random.key(seed)
    ks = iter(jax.random.split(key, 40))
    f32 = jnp.float32

    def nrm(shape, scale):
        return scale * jax.random.normal(next(ks), shape, f32)

    win_rows = min(MAX_WINDOW, PAST_LEN)
    n_dense = (DEPTH + 1) // 2
    n_moe = DEPTH // 2
    G, P, C = N_SSM_GROUPS, SSM_STATE, SSM_GROUP
    proj_cols = 3 * ATTN_WIDTH + SSM_WIDTH
    return {
        'x_prompt': nrm((BATCH, SEQ, D_MODEL), 1.0),
        'x_sample': nrm((DEC_BATCH, DEC_SEQ, D_MODEL), 1.0),
        'cache_win_k': nrm((DEPTH, DEC_BATCH, win_rows, N_HEADS, HEAD_DIM), 1.0),
        'cache_win_v': nrm((DEPTH, DEC_BATCH, win_rows, N_HEADS, HEAD_DIM), 1.0),
        'state_ssm_re': nrm((DEPTH, DEC_BATCH, G, P), 0.5),
        'state_ssm_im': nrm((DEPTH, DEC_BATCH, G, P), 0.5),
        'rel_bias': nrm((N_BUCKETS, N_HEADS), 0.5),
        'norm_mix': 1.0 + nrm((DEPTH, D_MODEL), 0.02),
        'w_in': nrm((DEPTH, D_MODEL, proj_cols), D_MODEL ** -0.5),
        'ssm_a_re': -0.5 * jnp.exp(nrm((DEPTH, G, P), 0.05)),
        'ssm_a_im': math.pi * jnp.arange(P, dtype=f32) + nrm((DEPTH, G, P), 0.05),
        'ssm_log_dt': jax.random.uniform(next(ks), (DEPTH, G), f32, math.log(1e-3), math.log(1e-1)),
        'ssm_b_re': nrm((DEPTH, G, P, C), (2 * C) ** -0.5),
        'ssm_b_im': nrm((DEPTH, G, P, C), (2 * C) ** -0.5),
        'ssm_c_re': nrm((DEPTH, G, C, P), (2 * P) ** -0.5),
        'ssm_c_im': nrm((DEPTH, G, C, P), (2 * P) ** -0.5),
        'ssm_d': nrm((DEPTH, G, C), 1.0),
        'ssm_w_glu': nrm((DEPTH, SSM_WIDTH, SSM_WIDTH), SSM_WIDTH ** -0.5),
        'ssm_b_glu': nrm((DEPTH, SSM_WIDTH), 0.02),
        'norm_attn_out': 1.0 + nrm((DEPTH, ATTN_WIDTH), 0.02),
        'norm_ssm_out': 1.0 + nrm((DEPTH, SSM_WIDTH), 0.02),
        'w_out': nrm((DEPTH, MIX_WIDTH, D_MODEL), (2 * MIX_WIDTH) ** -0.5),
        'norm_ffn': 1.0 + nrm((DEPTH, D_MODEL), 0.02),
        'ffn_w_gate': nrm((n_dense, D_MODEL, D_FF), D_MODEL ** -0.5),
        'ffn_w_up': nrm((n_dense, D_MODEL, D_FF), D_MODEL ** -0.5),
        'ffn_w_down': nrm((n_dense, D_FF, D_MODEL), (2 * D_FF) ** -0.5),
        'moe_w_router': nrm((n_moe, D_MODEL, N_EXPERTS), D_MODEL ** -0.5),
        'moe_b_router': nrm((n_moe, N_EXPERTS), 0.01),
        'moe_w_gate': nrm((n_moe, N_EXPERTS, D_MODEL, EXPERT_FF), D_MODEL ** -0.5),
        'moe_w_up': nrm((n_moe, N_EXPERTS, D_MODEL, EXPERT_FF), D_MODEL ** -0.5),
        'moe_w_down': nrm((n_moe, N_EXPERTS, EXPERT_FF, D_MODEL), (2 * EXPERT_FF) ** -0.5),
        'norm_final': 1.0 + nrm((D_MODEL,), 0.02),
    }


def reference(x_prompt, x_sample, cache_win_k, cache_win_v, state_ssm_re, state_ssm_im, rel_bias,
              norm_mix, w_in, ssm_a_re, ssm_a_im, ssm_log_dt, ssm_b_re, ssm_b_im, ssm_c_re, ssm_c_im,
              ssm_d, ssm_w_glu, ssm_b_glu, norm_attn_out, norm_ssm_out, w_out, norm_ffn,
              ffn_w_gate, ffn_w_up, ffn_w_down, moe_w_router, moe_b_router, moe_w_gate, moe_w_up,
              moe_w_down, norm_final):

    def layer(x, l, k_past, v_past, h0_re, h0_im):
        B, L, _ = x.shape
        hn = _rmsnorm(x, norm_mix[l])
        proj = hn @ w_in[l]
        q, k, v, u = jnp.split(proj, [ATTN_WIDTH, 2 * ATTN_WIDTH, 3 * ATTN_WIDTH], axis=-1)
        q = q.reshape(B, L, N_HEADS, HEAD_DIM)
        k = k.reshape(B, L, N_HEADS, HEAD_DIM)
        v = v.reshape(B, L, N_HEADS, HEAD_DIM)
        if k_past is None:
            attn = _prompt_attention(q, k, v, rel_bias)
            keep = min(MAX_WINDOW, L)
            new_k, new_v = k[:, L - keep:], v[:, L - keep:]
        else:
            k_ext = jnp.concatenate([k_past, k.astype(k_past.dtype)], axis=1)
            v_ext = jnp.concatenate([v_past, v.astype(v_past.dtype)], axis=1)
            q_idx = k_past.shape[1] + jnp.arange(L, dtype=jnp.int32)
            attn = _dilated_window_attention(q, k_ext, v_ext, q_idx, rel_bias)
            new_k, new_v = k, v
        y, hr, hi = _s5_scan(u.reshape(B, L, N_SSM_GROUPS, SSM_GROUP), ssm_a_re[l], ssm_a_im[l],
                             ssm_log_dt[l], ssm_b_re[l], ssm_b_im[l], ssm_c_re[l], ssm_c_im[l],
                             ssm_d[l], h0_re, h0_im)
        z = jax.nn.gelu(y.reshape(B, L, SSM_WIDTH))
        ssm = z * jax.nn.sigmoid(z @ ssm_w_glu[l].astype(jnp.float32) + ssm_b_glu[l].astype(jnp.float32))
        mixed = jnp.concatenate([_rmsnorm(attn.reshape(B, L, ATTN_WIDTH), norm_attn_out[l]),
                                 _rmsnorm(ssm, norm_ssm_out[l])], axis=-1).astype(x.dtype)
        x = x + mixed @ w_out[l]
        hn = _rmsnorm(x, norm_ffn[l])
        if l % 2 == 0:
            f = _swiglu(hn, ffn_w_gate[l // 2], ffn_w_up[l // 2], ffn_w_down[l // 2])
        else:
            f = _moe(hn, moe_w_router[l // 2], moe_b_router[l // 2], moe_w_gate[l // 2],
                     moe_w_up[l // 2], moe_w_down[l // 2])
        x = x + f.astype(x.dtype)
        return x, new_k, new_v, hr.astype(x.dtype), hi.astype(x.dtype)

    xp, xs = x_prompt, x_sample
    h0 = jnp.zeros((x_prompt.shape[0], N_SSM_GROUPS, SSM_STATE), jnp.float32)
    pk, pv, pr, pi, sk, sv, sr, si = [], [], [], [], [], [], [], []
    for l in range(DEPTH):
        xp, a, b, c, d = layer(xp, l, None, None, h0, h0)
        pk.append(a); pv.append(b); pr.append(c); pi.append(d)
        xs, a, b, c, d = layer(xs, l, cache_win_k[l], cache_win_v[l], state_ssm_re[l], state_ssm_im[l])
        sk.append(a); sv.append(b); sr.append(c); si.append(d)
    y_prompt = _rmsnorm(xp, norm_final)
    y_sample = _rmsnorm(xs, norm_final)
    return (y_prompt, y_sample, jnp.stack(pk, 0), jnp.stack(pv, 0), jnp.stack(pr, 0), jnp.stack(pi, 0),
            jnp.stack(sk, 0), jnp.stack(sv, 0), jnp.stack(sr, 0), jnp.stack(si, 0))
```

```python
import functools

import numpy as np
import jax
import jax.numpy as jnp
from jax import lax
from jax.experimental import pallas as pl
from jax.experimental.pallas import tpu as pltpu

F32 = jnp.float32
BF16 = jnp.bfloat16

HEAD_DIM = 64
DILATION_PATTERNS = ((128, 1), (512, 4), (2048, 16))
MAX_WINDOW = 2048
N_BUCKETS = 32
SSM_GROUP = 16
SSM_STATE = 64
RMS_EPS = 1e-6
NEG_BIG = -1e30

LANES = 128
Q_TILE = 128
GROUPS_PER_TILE = LANES // SSM_GROUP
STATE_TILE = GROUPS_PER_TILE * SSM_STATE
VMEM_LIMIT_BYTES = 56 * 1024 * 1024


def _compiler_params(semantics):
    return pltpu.CompilerParams(dimension_semantics=semantics, vmem_limit_bytes=VMEM_LIMIT_BYTES)


def _rms(x, g):
    return x * lax.rsqrt(jnp.mean(x * x, axis=-1, keepdims=True) + RMS_EPS) * g


def _t5_causal_bucket(dist):
    max_exact = N_BUCKETS // 2
    d = np.asarray(dist, dtype=np.int64)
    large = max_exact + (np.log(np.maximum(d, max_exact) / max_exact)
                         / np.log(MAX_WINDOW / max_exact)
                         * (N_BUCKETS - max_exact)).astype(np.int64)
    large = np.minimum(large, N_BUCKETS - 1)
    return np.where(d < max_exact, d, large).astype(np.int32)


def _multiplicity(d):
    m = np.zeros(d.shape, np.int32)
    for window, dil in DILATION_PATTERNS:
        m += ((d >= 0) & (d <= window) & (d % dil == 0)).astype(np.int32)
    return m


def _norm_proj_body(x_ref, g_ref, w_ref, o_ref):
    hn = _rms(x_ref[...], g_ref[...])
    o_ref[...] = jnp.dot(hn.astype(BF16), w_ref[...], preferred_element_type=F32)


def _norm_proj(x, g, w, tm):
    n, d = x.shape
    c = w.shape[1]
    return pl.pallas_call(
        _norm_proj_body,
        grid=(n // tm,),
        in_specs=[pl.BlockSpec((tm, d), lambda i: (i, 0)),
                  pl.BlockSpec((1, d), lambda i: (0, 0)),
                  pl.BlockSpec((d, c), lambda i: (0, 0))],
        out_specs=pl.BlockSpec((tm, c), lambda i: (i, 0)),
        out_shape=jax.ShapeDtypeStruct((n, c), F32),
        compiler_params=_compiler_params(("parallel",)),
        name="norm_proj",
    )(x, g.reshape(1, d), w)


def _band_bias(rel_bias):
    i = np.arange(Q_TILE)[:, None]
    kk = np.arange(2 * Q_TILE)[None, :]
    delta = Q_TILE + i - kk
    valid = (delta >= 0) & (delta <= Q_TILE)
    tabs = []
    for window, dil in DILATION_PATTERNS:
        assert window // dil == Q_TILE
        b = rel_bias[_t5_causal_bucket(np.clip(delta, 0, Q_TILE) * dil)]
        b = jnp.where(valid[..., None], b.astype(F32), NEG_BIG)
        tabs.append(jnp.transpose(b, (2, 0, 1)))
    return jnp.stack(tabs, 0)


def _prompt_attn_body(q_ref, k_ref, v_ref, bias_ref, o_ref, op_ref, lse_ref):
    seq = q_ref.shape[0]
    lane = lax.broadcasted_iota(jnp.int32, (Q_TILE, LANES), 1)
    head0 = lane < HEAD_DIM
    scale = HEAD_DIM ** -0.5

    def rows(start, size, dil):
        return pl.ds(start, size) if dil == 1 else pl.ds(start, size, stride=dil)

    def tile(p, dil, q_start, k_start, first):
        nk = Q_TILE if first else 2 * Q_TILE
        qsl = rows(q_start, Q_TILE, dil)
        ksl = rows(k_start, nk, dil)
        qt = q_ref[qsl, :] * scale
        kt = k_ref[ksl, :].astype(BF16)
        vt = v_ref[ksl, :].astype(BF16)
        outs, lses = [], []
        for h in range(2):
            qh = jnp.where(head0 if h == 0 else jnp.logical_not(head0), qt, 0.0).astype(BF16)
            s = lax.dot_general(qh, kt, (((1,), (1,)), ((), ())), preferred_element_type=F32)
            s = s + (bias_ref[p, h, :, Q_TILE:] if first else bias_ref[p, h])
            m = jnp.max(s, axis=-1, keepdims=True)
            e = jnp.exp(s - m)
            l = jnp.sum(e, axis=-1, keepdims=True)
            o = jnp.dot(e.astype(BF16), vt, preferred_element_type=F32)
            outs.append(o / l)
            lses.append(m + jnp.log(l))
        op_ref[p, qsl, :] = jnp.where(head0, outs[0], outs[1])
        lse_ref[p, qsl, :] = jnp.where(head0, lses[0], lses[1])

    for p, (_, dil) in enumerate(DILATION_PATTERNS):
        n_tiles = seq // dil // Q_TILE

        def residue(r, carry, p=p, dil=dil, n_tiles=n_tiles):
            tile(p, dil, r, r, True)
            if n_tiles > 1:
                def blk(t, c):
                    tile(p, dil, r + dil * Q_TILE * t, r + dil * Q_TILE * (t - 1), False)
                    return c
                lax.fori_loop(1, n_tiles, blk, 0)
            return carry

        if dil == 1:
            residue(0, 0)
        else:
            lax.fori_loop(0, dil, residue, 0)

    def combine(t, carry):
        r = pl.ds(pl.multiple_of(t * Q_TILE, Q_TILE), Q_TILE)
        lse = [lse_ref[p, r, :] for p in range(len(DILATION_PATTERNS))]
        mx = functools.reduce(jnp.maximum, lse)
        ws = [jnp.exp(x - mx) for x in lse]
        num = sum(w * op_ref[p, r, :] for p, w in enumerate(ws))
        o_ref[r, :] = num / sum(ws)
        return carry

    lax.fori_loop(0, seq // Q_TILE, combine, 0)


def _prompt_attn(proj, bias, batch, seq, attn_width):
    n_pairs = attn_width // LANES
    n_pat = len(DILATION_PATTERNS)
    assert seq % (Q_TILE * max(d for _, d in DILATION_PATTERNS)) == 0
    return pl.pallas_call(
        _prompt_attn_body,
        grid=(batch, n_pairs),
        in_specs=[pl.BlockSpec((seq, LANES), lambda b, h: (b, h)),
                  pl.BlockSpec((seq, LANES), lambda b, h: (b, n_pairs + h)),
                  pl.BlockSpec((seq, LANES), lambda b, h: (b, 2 * n_pairs + h)),
                  pl.BlockSpec((n_pat, 2, Q_TILE, 2 * Q_TILE), lambda b, h: (0, h, 0, 0))],
        out_specs=pl.BlockSpec((seq, LANES), lambda b, h: (b, h)),
        out_shape=jax.ShapeDtypeStruct((batch * seq, attn_width), F32),
        scratch_shapes=[pltpu.VMEM((n_pat, seq, LANES), F32),
                        pltpu.VMEM((n_pat, seq, LANES), F32)],
        compiler_params=_compiler_params(("parallel", "parallel")),
        name="prompt_attn",
    )(proj, proj, proj, bias)


DEC_RES = 16
NEAR_ROWS = 512


def _sample_tables(rel_bias, win_rows, n_new, n_heads):
    far = (16 * np.arange((win_rows - NEAR_ROWS) // DEC_RES)[:, None] + np.arange(n_new)[None, :]).reshape(-1)
    near = win_rows - NEAR_ROWS + np.arange(NEAR_ROWS)
    new = win_rows + np.arange(n_new)
    eye = np.eye(n_heads, dtype=bool)
    out = []
    for pos in (far, near, new):
        d = win_rows + np.arange(n_new)[:, None] - pos[None, :]
        mult = _multiplicity(d)
        b = rel_bias[_t5_causal_bucket(np.maximum(d, 0))].astype(F32)
        ok = (mult > 0)[:, None, :, None] & eye[None, :, None, :]
        b = jnp.where(ok, b[:, None, :, :], NEG_BIG)
        rows = pos.shape[0] * n_heads
        bias_tab = b.reshape(n_new * n_heads, rows)
        mult_tab = np.broadcast_to(np.maximum(mult, 1)[:, None, :, None],
                                   (n_new, n_heads, pos.shape[0], n_heads)).reshape(n_new * n_heads, rows)
        out.append((bias_tab, jnp.asarray(mult_tab, F32)))
    return out


def _sample_attn_body(q_ref, kf_ref, kn_ref, kw_ref, vf_ref, vn_ref, vw_ref,
                      bf_ref, bn_ref, bw_ref, mf_ref, mn_ref, mw_ref, o_ref):
    qb = (q_ref[...] * HEAD_DIM ** -0.5).astype(BF16)
    segs = ((kf_ref, vf_ref, bf_ref, mf_ref), (kn_ref, vn_ref, bn_ref, mn_ref), (kw_ref, vw_ref, bw_ref, mw_ref))
    scores = []
    for k_ref, _, b_ref, _ in segs:
        k = k_ref[...].reshape(-1, HEAD_DIM).astype(BF16)
        s = lax.dot_general(qb, k, (((1,), (1,)), ((), ())), preferred_element_type=F32)
        scores.append(s + b_ref[...])
    m = functools.reduce(jnp.maximum, [jnp.max(s, axis=-1, keepdims=True) for s in scores])
    num = jnp.zeros(o_ref.shape, F32)
    den = jnp.zeros((o_ref.shape[0], 1), F32)
    for s, (_, v_ref, _, m_ref) in zip(scores, segs):
        e = jnp.exp(s - m) * m_ref[...]
        den = den + jnp.sum(e, axis=-1, keepdims=True)
        v = v_ref[...].reshape(-1, HEAD_DIM).astype(BF16)
        num = num + jnp.dot(e.astype(BF16), v, preferred_element_type=F32)
    o_ref[...] = num / den


def _sample_attn(q, k_new, v_new, cache_k, cache_v, layer, tables):
    depth, batch, win, n_heads, hd = cache_k.shape
    n_new = k_new.shape[1]
    assert hd == HEAD_DIM and win == MAX_WINDOW and win % NEAR_ROWS == 0 and n_new <= DEC_RES
    n_far = (win - NEAR_ROWS) // DEC_RES
    far_shape = (depth, batch, win // DEC_RES, DEC_RES, n_heads, hd)
    far_spec = pl.BlockSpec((None, None, n_far, n_new, n_heads, hd), lambda b: (layer, b, 0, 0, 0, 0))
    near_spec = pl.BlockSpec((None, None, NEAR_ROWS, n_heads, hd), lambda b: (layer, b, win // NEAR_ROWS - 1, 0, 0))
    new_spec = pl.BlockSpec((None, n_new, n_heads, hd), lambda b: (b, 0, 0, 0))
    tab_specs = [pl.BlockSpec(t.shape, lambda b: (0, 0)) for pair in tables for t in pair]
    (bf, mf), (bn, mn), (bw, mw) = tables
    return pl.pallas_call(
        _sample_attn_body,
        grid=(batch,),
        in_specs=[pl.BlockSpec((None, n_new * n_heads, hd), lambda b: (b, 0, 0)),
                  far_spec, near_spec, new_spec, far_spec, near_spec, new_spec,
                  tab_specs[0], tab_specs[2], tab_specs[4], tab_specs[1], tab_specs[3], tab_specs[5]],
        out_specs=pl.BlockSpec((None, n_new * n_heads, hd), lambda b: (b, 0, 0)),
        out_shape=jax.ShapeDtypeStruct((batch, n_new * n_heads, hd), F32),
        compiler_params=_compiler_params(("parallel",)),
        name="sample_attn",
    )(q, cache_k.reshape(far_shape), cache_k, k_new, cache_v.reshape(far_shape), cache_v, v_new,
      bf, bn, bw, mf, mn, mw)


def _s5_param_body(are_ref, aim_ref, ldt_ref, bre_ref, bim_ref, ar_ref, ai_ref, bbre_ref, bbim_ref):
    a_re, a_im = are_ref[...], aim_ref[...]
    dt = jnp.exp(ldt_ref[...])
    mag = jnp.exp(dt * a_re)
    ar, ai = mag * jnp.cos(dt * a_im), mag * jnp.sin(dt * a_im)
    nr, ni = ar - 1.0, ai
    den = a_re * a_re + a_im * a_im
    fr = (nr * a_re + ni * a_im) / den
    fi = (ni * a_re - nr * a_im) / den
    br, bi = bre_ref[...], bim_ref[...]
    ar_ref[...] = ar
    ai_ref[...] = ai
    bbre_ref[...] = fr[:, None, :] * br - fi[:, None, :] * bi
    bbim_ref[...] = fr[:, None, :] * bi + fi[:, None, :] * br


def _s5_params(a_re, a_im, log_dt, b_re, b_im):
    depth, g, p = a_re.shape
    c = b_re.shape[-1]
    n = depth * g
    full = lambda shape: pl.BlockSpec(shape, lambda: (0,) * len(shape))
    ar, ai, bbre, bbim = pl.pallas_call(
        _s5_param_body,
        in_specs=[full((n, p)), full((n, p)), full((n, 1)), full((n, c, p)), full((n, c, p))],
        out_specs=[full((n, p)), full((n, p)), full((n, c, p)), full((n, c, p))],
        out_shape=[jax.ShapeDtypeStruct((n, p), F32), jax.ShapeDtypeStruct((n, p), F32),
                   jax.ShapeDtypeStruct((n, c, p), F32), jax.ShapeDtypeStruct((n, c, p), F32)],
        name="s5_params",
    )(a_re.reshape(n, p), a_im.reshape(n, p), log_dt.reshape(n, 1),
      jnp.swapaxes(b_re, -1, -2).reshape(n, c, p), jnp.swapaxes(b_im, -1, -2).reshape(n, c, p))
    return (ar.reshape(depth, g, p), ai.reshape(depth, g, p),
            bbre.reshape(depth, g, c, p), bbim.reshape(depth, g, c, p))


def _pack_s5_weights(bb_re, bb_im, c_re, c_im):
    g, c, p = bb_re.shape
    tiles = g // GROUPS_PER_TILE
    eye = jnp.eye(GROUPS_PER_TILE, dtype=F32)

    def inp(w):
        w = w.reshape(tiles, GROUPS_PER_TILE, c, p)
        return jnp.einsum('jgcp,gh->jgchp', w, eye).reshape(tiles, LANES, STATE_TILE)

    def outp(w):
        w = w.reshape(tiles, GROUPS_PER_TILE, c, p)
        return jnp.einsum('jgcp,gh->jgphc', w, eye).reshape(tiles, STATE_TILE, LANES)

    wx = jnp.concatenate([inp(bb_re), inp(bb_im)], axis=-1).astype(BF16)
    wc = jnp.concatenate([outp(c_re), -outp(c_im)], axis=1).astype(BF16)
    return wx, wc


def _pack_state(h_re, h_im):
    b, g, p = h_re.shape
    tiles = g // GROUPS_PER_TILE
    st = jnp.stack([h_re.reshape(b, tiles, STATE_TILE), h_im.reshape(b, tiles, STATE_TILE)], axis=2)
    return st.reshape(b, tiles * 2 * STATE_TILE).astype(F32)


def _unpack_state(h, g, p):
    b = h.shape[0]
    st = h.reshape(b, g // GROUPS_PER_TILE, 2, STATE_TILE)
    return st[:, :, 0].reshape(b, g, p), st[:, :, 1].reshape(b, g, p)


def _s5_scan_body(u_ref, wx_ref, wc_ref, ar_ref, ai_ref, d_ref, h0_ref, y_ref, hout_ref, xs_ref, h_ref,
                  *, bsz, steps):
    tiles = wx_ref.shape[0]

    @pl.when(pl.program_id(0) == 0)
    def _():
        h_ref[...] = h0_ref[...]

    u = u_ref[...]
    ub = u.astype(BF16)
    for j in range(tiles):
        xs_ref[:, 2 * STATE_TILE * j:2 * STATE_TILE * (j + 1)] = jnp.dot(
            ub[:, LANES * j:LANES * (j + 1)], wx_ref[j], preferred_element_type=F32)

    for j in range(tiles):
        re = slice(2 * STATE_TILE * j, 2 * STATE_TILE * j + STATE_TILE)
        im = slice(2 * STATE_TILE * j + STATE_TILE, 2 * STATE_TILE * (j + 1))
        a_r = jnp.broadcast_to(ar_ref[:, STATE_TILE * j:STATE_TILE * (j + 1)], (bsz, STATE_TILE))
        a_i = jnp.broadcast_to(ai_ref[:, STATE_TILE * j:STATE_TILE * (j + 1)], (bsz, STATE_TILE))

        def step(t, carry, re=re, im=im, a_r=a_r, a_i=a_i):
            hr, hi = carry
            r = pl.ds(pl.multiple_of(t * bsz, bsz), bsz)
            nhr = a_r * hr - a_i * hi + xs_ref[r, re]
            nhi = a_r * hi + a_i * hr + xs_ref[r, im]
            xs_ref[r, re] = nhr
            xs_ref[r, im] = nhi
            return nhr, nhi

        hr, hi = lax.fori_loop(0, steps, step, (h_ref[:, re], h_ref[:, im]))
        h_ref[:, re] = hr
        h_ref[:, im] = hi

    for j in range(tiles):
        cols = slice(LANES * j, LANES * (j + 1))
        hb = xs_ref[:, 2 * STATE_TILE * j:2 * STATE_TILE * (j + 1)].astype(BF16)
        y_ref[:, cols] = jnp.dot(hb, wc_ref[j], preferred_element_type=F32) + d_ref[:, cols] * u[:, cols]
    hout_ref[...] = h_ref[...]


def _s5_scan(u, wx, wc, ar, ai, d_skip, h0, bsz, steps):
    n, ch = u.shape
    tiles = wx.shape[0]
    width = tiles * 2 * STATE_TILE
    rows = steps * bsz
    const = lambda shape: pl.BlockSpec(shape, lambda i: (0,) * len(shape))
    return pl.pallas_call(
        functools.partial(_s5_scan_body, bsz=bsz, steps=steps),
        grid=(n // rows,),
        in_specs=[pl.BlockSpec((rows, ch), lambda i: (i, 0)),
                  const(wx.shape), const(wc.shape), const((1, tiles * STATE_TILE)),
                  const((1, tiles * STATE_TILE)), const((1, ch)), const((bsz, width))],
        out_specs=[pl.BlockSpec((rows, ch), lambda i: (i, 0)), const((bsz, width))],
        out_shape=[jax.ShapeDtypeStruct((n, ch), F32), jax.ShapeDtypeStruct((bsz, width), F32)],
        scratch_shapes=[pltpu.VMEM((rows, width), F32), pltpu.VMEM((bsz, width), F32)],
        compiler_params=_compiler_params(("arbitrary",)),
        name="s5_scan",
    )(u, wx, wc, ar.reshape(1, -1), ai.reshape(1, -1), d_skip.reshape(1, ch), h0)


def _post_mix_body(attn_ref, y_ref, x_ref, ga_ref, gs_ref, wglu_ref, bglu_ref, wo_ref, o_ref):
    aw = attn_ref.shape[1]
    an = _rms(attn_ref[...], ga_ref[...])
    z = jax.nn.gelu(y_ref[...])
    gate = jax.nn.sigmoid(jnp.dot(z.astype(BF16), wglu_ref[...], preferred_element_type=F32) + bglu_ref[...])
    sn = _rms(z * gate, gs_ref[...])
    o_ref[...] = (x_ref[...]
                  + jnp.dot(an.astype(BF16), wo_ref[:aw, :], preferred_element_type=F32)
                  + jnp.dot(sn.astype(BF16), wo_ref[aw:, :], preferred_element_type=F32))


def _post_mix(attn, y, x, g_attn, g_ssm, w_glu, b_glu, w_out, tm):
    n, d = x.shape
    aw, sw = attn.shape[1], y.shape[1]
    row = lambda w: pl.BlockSpec((tm, w), lambda i: (i, 0))
    const = lambda shape: pl.BlockSpec(shape, lambda i: (0, 0))
    return pl.pallas_call(
        _post_mix_body,
        grid=(n // tm,),
        in_specs=[row(aw), row(sw), row(d), const((1, aw)), const((1, sw)), const((sw, sw)), const((1, sw)),
                  const((aw + sw, d))],
        out_specs=row(d),
        out_shape=jax.ShapeDtypeStruct((n, d), F32),
        compiler_params=_compiler_params(("parallel",)),
        name="post_mix",
    )(attn, y, x, g_attn.reshape(1, aw), g_ssm.reshape(1, sw), w_glu, b_glu.reshape(1, sw), w_out)


def _finish(x_ref, acc_ref, gf_ref, o_ref):
    y = x_ref[...] + acc_ref[...]
    o_ref[...] = y if gf_ref is None else _rms(y, gf_ref[...])


def _ffn_body(x_ref, g_ref, wg_ref, wu_ref, wd_ref, *rest, final):
    gf_ref = rest[0] if final else None
    o_ref, hn_ref, acc_ref = rest[-3:]
    f = pl.program_id(1)

    @pl.when(f == 0)
    def _():
        hn_ref[...] = _rms(x_ref[...], g_ref[...]).astype(BF16)
        acc_ref[...] = jnp.zeros_like(acc_ref)

    hn = hn_ref[...]
    gate = jnp.dot(hn, wg_ref[...], preferred_element_type=F32)
    up = jnp.dot(hn, wu_ref[...], preferred_element_type=F32)
    act = (jax.nn.silu(gate) * up).astype(BF16)
    acc_ref[...] += jnp.dot(act, wd_ref[...], preferred_element_type=F32)

    @pl.when(f == pl.num_programs(1) - 1)
    def _():
        _finish(x_ref, acc_ref, gf_ref, o_ref)


def _ffn(x, g, wg, wu, wd, g_final, tm, tf):
    n, d = x.shape
    ff = wg.shape[1]
    final = g_final is not None
    in_specs = [pl.BlockSpec((tm, d), lambda i, f: (i, 0)),
                pl.BlockSpec((1, d), lambda i, f: (0, 0)),
                pl.BlockSpec((d, tf), lambda i, f: (0, f)),
                pl.BlockSpec((d, tf), lambda i, f: (0, f)),
                pl.BlockSpec((tf, d), lambda i, f: (f, 0))]
    args = [x, g.reshape(1, d), wg, wu, wd]
    if final:
        in_specs.append(pl.BlockSpec((1, d), lambda i, f: (0, 0)))
        args.append(g_final.reshape(1, d))
    return pl.pallas_call(
        functools.partial(_ffn_body, final=final),
        grid=(n // tm, ff // tf),
        in_specs=in_specs,
        out_specs=pl.BlockSpec((tm, d), lambda i, f: (i, 0)),
        out_shape=jax.ShapeDtypeStruct((n, d), F32),
        scratch_shapes=[pltpu.VMEM((tm, d), BF16), pltpu.VMEM((tm, d), F32)],
        compiler_params=_compiler_params(("parallel", "arbitrary")),
        name="ffn_dense",
    )(*args)


def _moe_body(x_ref, g_ref, wr_ref, br_ref, wg_ref, wu_ref, wd_ref, *rest, n_experts, final):
    gf_ref = rest[0] if final else None
    o_ref, hn_ref, comb_ref, acc_ref = rest[-4:]
    e, f = pl.program_id(1), pl.program_id(2)
    lane = lax.broadcasted_iota(jnp.int32, comb_ref.shape, 1).astype(F32)

    @pl.when((e == 0) & (f == 0))
    def _():
        hn = _rms(x_ref[...], g_ref[...])
        hn_ref[...] = hn.astype(BF16)
        acc_ref[...] = jnp.zeros_like(acc_ref)
        logits = jnp.dot(hn, wr_ref[...], preferred_element_type=F32,
                         precision=lax.Precision.HIGHEST) + br_ref[...]
        logits = jnp.where(lane < n_experts, logits, -jnp.inf)
        m1 = jnp.max(logits, axis=-1, keepdims=True)
        i1 = jnp.min(jnp.where(logits == m1, lane, float(LANES)), axis=-1, keepdims=True)
        rest_logits = jnp.where(lane == i1, -jnp.inf, logits)
        m2 = jnp.max(rest_logits, axis=-1, keepdims=True)
        i2 = jnp.min(jnp.where(rest_logits == m2, lane, float(LANES)), axis=-1, keepdims=True)
        e2 = jnp.exp(m2 - m1)
        g1 = 1.0 / (1.0 + e2)
        g2 = e2 / (1.0 + e2)
        comb_ref[...] = jnp.where(lane == i1, g1, 0.0) + jnp.where(lane == i2, g2, 0.0)

    hn = hn_ref[...]
    gate = jnp.dot(hn, wg_ref[...], preferred_element_type=F32)
    up = jnp.dot(hn, wu_ref[...], preferred_element_type=F32)
    act = (jax.nn.silu(gate) * up).astype(BF16)
    weight = jnp.sum(jnp.where(lane == e.astype(F32), comb_ref[...], 0.0), axis=-1, keepdims=True)
    acc_ref[...] += weight * jnp.dot(act, wd_ref[...], preferred_element_type=F32)

    @pl.when((e == n_experts - 1) & (f == pl.num_programs(2) - 1))
    def _():
        _finish(x_ref, acc_ref, gf_ref, o_ref)


def _moe(x, g, w_router, b_router, wg, wu, wd, g_final, tm, tf):
    n, d = x.shape
    n_experts, _, ff = wg.shape
    final = g_final is not None
    wr = jnp.zeros((d, LANES), F32).at[:, :n_experts].set(w_router.astype(F32))
    br = jnp.zeros((1, LANES), F32).at[0, :n_experts].set(b_router.astype(F32))
    in_specs = [pl.BlockSpec((tm, d), lambda i, e, f: (i, 0)),
                pl.BlockSpec((1, d), lambda i, e, f: (0, 0)),
                pl.BlockSpec((d, LANES), lambda i, e, f: (0, 0)),
                pl.BlockSpec((1, LANES), lambda i, e, f: (0, 0)),
                pl.BlockSpec((None, d, tf), lambda i, e, f: (e, 0, f)),
                pl.BlockSpec((None, d, tf), lambda i, e, f: (e, 0, f)),
                pl.BlockSpec((None, tf, d), lambda i, e, f: (e, f, 0))]
    args = [x, g.reshape(1, d), wr, br, wg, wu, wd]
    if final:
        in_specs.append(pl.BlockSpec((1, d), lambda i, e, f: (0, 0)))
        args.append(g_final.reshape(1, d))
    return pl.pallas_call(
        functools.partial(_moe_body, n_experts=n_experts, final=final),
        grid=(n // tm, n_experts, ff // tf),
        in_specs=in_specs,
        out_specs=pl.BlockSpec((tm, d), lambda i, e, f: (i, 0)),
        out_shape=jax.ShapeDtypeStruct((n, d), F32),
        scratch_shapes=[pltpu.VMEM((tm, d), BF16), pltpu.VMEM((tm, LANES), F32), pltpu.VMEM((tm, d), F32)],
        compiler_params=_compiler_params(("parallel", "arbitrary", "arbitrary")),
        name="moe",
    )(*args)


def _time_major(a, batch, seq):
    return a.reshape(batch, seq, -1).transpose(1, 0, 2).reshape(batch * seq, -1)


def _batch_major(a, batch, seq):
    return a.reshape(seq, batch, -1).transpose(1, 0, 2).reshape(batch * seq, -1)


def kernel(x_prompt, x_sample, cache_win_k, cache_win_v, state_ssm_re, state_ssm_im, rel_bias, norm_mix, w_in, ssm_a_re, ssm_a_im, ssm_log_dt, ssm_b_re, ssm_b_im, ssm_c_re, ssm_c_im, ssm_d, ssm_w_glu, ssm_b_glu, norm_attn_out, norm_ssm_out, w_out, norm_ffn, ffn_w_gate, ffn_w_up, ffn_w_down, moe_w_router, moe_b_router, moe_w_gate, moe_w_up, moe_w_down, norm_final):
    batch, seq, d_model = x_prompt.shape
    dec_batch, dec_seq, _ = x_sample.shape
    depth = w_in.shape[0]
    n_heads = cache_win_k.shape[3]
    attn_width = n_heads * HEAD_DIM
    n_groups, n_state = ssm_a_re.shape[1:]
    ssm_width = n_groups * SSM_GROUP
    assert w_in.shape[2] == 3 * attn_width + ssm_width and seq == MAX_WINDOW
    u_col = 3 * attn_width

    band_bias = _band_bias(rel_bias)
    sample_tabs = _sample_tables(rel_bias, cache_win_k.shape[2], dec_seq, n_heads)
    ar, ai, bb_re, bb_im = _s5_params(ssm_a_re, ssm_a_im, ssm_log_dt, ssm_b_re, ssm_b_im)
    c_re_t = jnp.asarray(ssm_c_re, F32)
    c_im_t = jnp.asarray(ssm_c_im, F32)

    w_in_b, w_out_b, w_glu_b = w_in.astype(BF16), w_out.astype(BF16), ssm_w_glu.astype(BF16)
    ffn_b = [w.astype(BF16) for w in (ffn_w_gate, ffn_w_up, ffn_w_down)]
    moe_b = [w.astype(BF16) for w in (moe_w_gate, moe_w_up, moe_w_down)]
    tf = ffn_w_gate.shape[2] // 2

    def layer(x, l, bsz, length, tm, scan_steps, attend, h0, g_final):
        proj = _norm_proj(x, norm_mix[l], w_in_b[l], tm)
        attn = attend(proj)
        wx, wc = _pack_s5_weights(bb_re[l], bb_im[l], c_re_t[l], c_im_t[l])
        y, h_fin = _s5_scan(_time_major(proj[:, u_col:], bsz, length), wx, wc, ar[l], ai[l], ssm_d[l], h0,
                            bsz, scan_steps)
        y = _batch_major(y, bsz, length)
        x = _post_mix(attn, y, x, norm_attn_out[l], norm_ssm_out[l], w_glu_b[l], ssm_b_glu[l], w_out_b[l], tm)
        if l % 2 == 0:
            x = _ffn(x, norm_ffn[l], ffn_b[0][l // 2], ffn_b[1][l // 2], ffn_b[2][l // 2], g_final, tm, tf)
        else:
            x = _moe(x, norm_ffn[l], moe_w_router[l // 2], moe_b_router[l // 2],
                     moe_b[0][l // 2], moe_b[1][l // 2], moe_b[2][l // 2], g_final, tm, tf)
        new_k = proj[:, attn_width:2 * attn_width].reshape(bsz, length, n_heads, HEAD_DIM)
        new_v = proj[:, 2 * attn_width:3 * attn_width].reshape(bsz, length, n_heads, HEAD_DIM)
        h_re, h_im = _unpack_state(h_fin, n_groups, n_state)
        return x, new_k, new_v, h_re, h_im

    def sample_attend(l):
        def attend(proj):
            q = proj[:, :attn_width].reshape(dec_batch, dec_seq * n_heads, HEAD_DIM)
            k_new = proj[:, attn_width:2 * attn_width].reshape(dec_batch, dec_seq, n_heads, HEAD_DIM)
            v_new = proj[:, 2 * attn_width:3 * attn_width].reshape(dec_batch, dec_seq, n_heads, HEAD_DIM)
            o = _sample_attn(q, k_new, v_new, cache_win_k, cache_win_v, l, sample_tabs)
            return o.reshape(dec_batch * dec_seq, attn_width)
        return attend

    xp = x_prompt.reshape(batch * seq, d_model)
    xs = x_sample.reshape(dec_batch * dec_seq, d_model)
    zero_state = jnp.zeros((batch, 2 * n_groups * n_state), F32)
    outs = [[] for _ in range(8)]
    for l in range(depth):
        g_final = norm_final if l == depth - 1 else None
        res_p = layer(xp, l, batch, seq, 512, 64,
                      lambda proj: _prompt_attn(proj, band_bias, batch, seq, attn_width), zero_state, g_final)
        res_s = layer(xs, l, dec_batch, dec_seq, dec_batch * dec_seq, dec_seq, sample_attend(l),
                      _pack_state(state_ssm_re[l], state_ssm_im[l]), g_final)
        xp, xs = res_p[0], res_s[0]
        for i in range(4):
            outs[i].append(res_p[1 + i])
            outs[4 + i].append(res_s[1 + i])
    return (xp.reshape(batch, seq, d_model), xs.reshape(dec_batch, dec_seq, d_model),
            *[jnp.stack(o, 0) for o in outs])
```

```python
import functools

import numpy as np
import jax
import jax.numpy as jnp
from jax import lax
from jax.experimental import pallas as pl
from jax.experimental.pallas import tpu as pltpu

F32 = jnp.float32
BF16 = jnp.bfloat16

HEAD_DIM = 64
DILATION_PATTERNS = ((128, 1), (512, 4), (2048, 16))
MAX_WINDOW = 2048
N_BUCKETS = 32
SSM_GROUP = 16
SSM_STATE = 64
RMS_EPS = 1e-6
NEG_BIG = -1e30

LANES = 128
Q_TILE = 128
ATTN_UNROLL = 4
GROUPS_PER_TILE = LANES // SSM_GROUP
STATE_TILE = GROUPS_PER_TILE * SSM_STATE
VMEM_LIMIT_BYTES = 56 * 1024 * 1024


def _compiler_params(semantics):
    return pltpu.CompilerParams(dimension_semantics=semantics, vmem_limit_bytes=VMEM_LIMIT_BYTES)


def _rms(x, g):
    return x * lax.rsqrt(jnp.mean(x * x, axis=-1, keepdims=True) + RMS_EPS) * g


def _t5_causal_bucket(dist):
    max_exact = N_BUCKETS // 2
    d = np.asarray(dist, dtype=np.int64)
    large = max_exact + (np.log(np.maximum(d, max_exact) / max_exact)
                         / np.log(MAX_WINDOW / max_exact)
                         * (N_BUCKETS - max_exact)).astype(np.int64)
    large = np.minimum(large, N_BUCKETS - 1)
    return np.where(d < max_exact, d, large).astype(np.int32)


def _multiplicity(d):
    m = np.zeros(d.shape, np.int32)
    for window, dil in DILATION_PATTERNS:
        m += ((d >= 0) & (d <= window) & (d % dil == 0)).astype(np.int32)
    return m


def _norm_proj_body(x_ref, g_ref, w_ref, o_ref):
    hn = _rms(x_ref[...], g_ref[...])
    o_ref[...] = jnp.dot(hn.astype(BF16), w_ref[...], preferred_element_type=F32)


def _norm_proj(x, g, w, layer, tm):
    n, d = x.shape
    c = w.shape[2]
    return pl.pallas_call(
        _norm_proj_body,
        grid=(n // tm,),
        in_specs=[pl.BlockSpec((tm, d), lambda i: (i, 0)),
                  pl.BlockSpec((None, 1, d), lambda i: (layer, 0, 0)),
                  pl.BlockSpec((None, d, c), lambda i: (layer, 0, 0))],
        out_specs=pl.BlockSpec((tm, c), lambda i: (i, 0)),
        out_shape=jax.ShapeDtypeStruct((n, c), F32),
        compiler_params=_compiler_params(("parallel",)),
        name="norm_proj",
    )(x, g, w)


def _norm_proj_prompt_body(x_ref, g_ref, w_ref, qkv_ref, kt_ref, vt_ref, u_ref, *, attn_width):
    hn = _rms(x_ref[...], g_ref[...])
    proj = jnp.dot(hn.astype(BF16), w_ref[...], preferred_element_type=F32)
    qkv_ref[...] = proj[:, :3 * attn_width]
    kt_ref[...] = proj[:, attn_width:2 * attn_width].T
    vt_ref[...] = proj[:, 2 * attn_width:3 * attn_width].T
    u_ref[...] = proj[:, 3 * attn_width:]


def _norm_proj_prompt(x, g, w, layer, batch, seq, attn_width, tm):
    n, d = x.shape
    c = w.shape[2]
    ch = c - 3 * attn_width
    per_seq = seq // tm
    return pl.pallas_call(
        functools.partial(_norm_proj_prompt_body, attn_width=attn_width),
        grid=(n // tm,),
        in_specs=[pl.BlockSpec((tm, d), lambda i: (i, 0)),
                  pl.BlockSpec((None, 1, d), lambda i: (layer, 0, 0)),
                  pl.BlockSpec((None, d, c), lambda i: (layer, 0, 0))],
        out_specs=[pl.BlockSpec((tm, 3 * attn_width), lambda i: (i, 0)),
                   pl.BlockSpec((None, attn_width, tm), lambda i: (i // per_seq, 0, i % per_seq)),
                   pl.BlockSpec((None, attn_width, tm), lambda i: (i // per_seq, 0, i % per_seq)),
                   pl.BlockSpec((tm, ch), lambda i: (i % per_seq, i // per_seq))],
        out_shape=[jax.ShapeDtypeStruct((n, 3 * attn_width), F32),
                   jax.ShapeDtypeStruct((batch, attn_width, seq), F32),
                   jax.ShapeDtypeStruct((batch, attn_width, seq), F32),
                   jax.ShapeDtypeStruct((seq, batch * ch), F32)],
        compiler_params=_compiler_params(("parallel",)),
        name="norm_proj_prompt",
    )(x, g, w)


def _band_bias(rel_bias):
    period = 3 * Q_TILE
    n_heads = rel_bias.shape[1]
    rows0 = []
    for window, dil in DILATION_PATTERNS:
        assert window // dil == Q_TILE
        vec = rel_bias[_t5_causal_bucket((Q_TILE - np.arange(Q_TILE + 1)) * dil)].astype(F32)
        rows0.append(jnp.concatenate([vec.T, jnp.full((n_heads, period - Q_TILE - 1), NEG_BIG, F32)], axis=1))
    row0 = jnp.stack(rows0, 0)
    flat = jnp.tile(row0, (1, 1, Q_TILE))[:, :, :Q_TILE * (period - 1)]
    return flat.reshape(len(DILATION_PATTERNS), n_heads, Q_TILE, period - 1)[:, :, :, :2 * Q_TILE]


def _prompt_attn_body(q_ref, k_ref, v_ref, bias_ref, o_ref, op_ref, lse_ref):
    seq = q_ref.shape[0]
    lane = lax.broadcasted_iota(jnp.int32, (Q_TILE, LANES), 1)
    head0 = lane < HEAD_DIM
    scale = HEAD_DIM ** -0.5

    def rows(start, size, dil):
        return pl.ds(start, size) if dil == 1 else pl.ds(start, size, stride=dil)

    def tiles(p, dil, starts, first):
        nk = Q_TILE if first else 2 * Q_TILE
        work = []
        for q_start, k_start in starts:
            qsl, ksl = rows(q_start, Q_TILE, dil), rows(k_start, nk, dil)
            qt = q_ref[qsl, :] * scale
            kt = k_ref[ksl, :].astype(BF16)
            vt = v_ref[ksl, :].astype(BF16)
            for h in range(2):
                qh = jnp.where(head0 if h == 0 else jnp.logical_not(head0), qt, 0.0).astype(BF16)
                work.append([qsl, h, vt, qh, kt])
        for w in work:
            s = lax.dot_general(w[3], w[4], (((1,), (1,)), ((), ())), preferred_element_type=F32)
            w[3] = s + (bias_ref[p, w[1], :, Q_TILE:] if first else bias_ref[p, w[1]])
        for w in work:
            w[4] = jnp.max(w[3], axis=-1, keepdims=True)
        for w in work:
            w[3] = jnp.exp(w[3] - w[4])
        for w in work:
            l = jnp.sum(w[3], axis=-1, keepdims=True)
            o = jnp.dot(w[3].astype(BF16), w[2], preferred_element_type=F32)
            w[3] = o / l
            w[4] = w[4] + jnp.log(l)
        for w0, w1 in zip(work[0::2], work[1::2]):
            op_ref[p, w0[0], :] = jnp.where(head0, w0[3], w1[3])
            lse_ref[p, w0[0], :] = jnp.where(head0, w0[4], w1[4])

    for p, (_, dil) in enumerate(DILATION_PATTERNS):
        n_tiles = seq // dil // Q_TILE
        stride = dil * Q_TILE
        unroll = ATTN_UNROLL
        if dil < unroll:
            unroll = max(u for u in range(1, ATTN_UNROLL + 1) if (n_tiles - 1) % u == 0)
            for r in range(dil):
                tiles(p, dil, [(r, r)], True)

                def blk(t, c, p=p, dil=dil, r=r, stride=stride, unroll=unroll):
                    first_t = 1 + t * unroll
                    tiles(p, dil, [(r + stride * (first_t + j), r + stride * (first_t + j - 1))
                                   for j in range(unroll)], False)
                    return c
                lax.fori_loop(0, (n_tiles - 1) // unroll, blk, 0)
        else:
            assert dil % unroll == 0

            def classes(g, c, p=p, dil=dil, n_tiles=n_tiles, stride=stride, unroll=unroll):
                rs = [g * unroll + j for j in range(unroll)]
                tiles(p, dil, [(r, r) for r in rs], True)
                if n_tiles > 1:
                    def blk(t, c2):
                        tiles(p, dil, [(r + stride * t, r + stride * (t - 1)) for r in rs], False)
                        return c2
                    lax.fori_loop(1, n_tiles, blk, 0)
                return c
            lax.fori_loop(0, dil // unroll, classes, 0)

    def combine(t, carry):
        r = pl.ds(pl.multiple_of(t * Q_TILE, Q_TILE), Q_TILE)
        lse = [lse_ref[p, r, :] for p in range(len(DILATION_PATTERNS))]
        mx = functools.reduce(jnp.maximum, lse)
        ws = [jnp.exp(x - mx) for x in lse]
        num = sum(w * op_ref[p, r, :] for p, w in enumerate(ws))
        o_ref[r, :] = num / sum(ws)
        return carry

    lax.fori_loop(0, seq // Q_TILE, combine, 0)


def _prompt_attn(proj, bias, batch, seq, attn_width):
    n_pairs = attn_width // LANES
    n_pat = len(DILATION_PATTERNS)
    assert seq % (Q_TILE * max(d for _, d in DILATION_PATTERNS)) == 0
    return pl.pallas_call(
        _prompt_attn_body,
        grid=(batch, n_pairs),
        in_specs=[pl.BlockSpec((seq, LANES), lambda b, h: (b, h)),
                  pl.BlockSpec((seq, LANES), lambda b, h: (b, n_pairs + h)),
                  pl.BlockSpec((seq, LANES), lambda b, h: (b, 2 * n_pairs + h)),
                  pl.BlockSpec((n_pat, 2, Q_TILE, 2 * Q_TILE), lambda b, h: (0, h, 0, 0))],
        out_specs=pl.BlockSpec((seq, LANES), lambda b, h: (b, h)),
        out_shape=jax.ShapeDtypeStruct((batch * seq, attn_width), F32),
        scratch_shapes=[pltpu.VMEM((n_pat, seq, LANES), F32),
                        pltpu.VMEM((n_pat, seq, LANES), F32)],
        compiler_params=_compiler_params(("parallel", "parallel")),
        name="prompt_attn",
    )(proj, proj, proj, bias)


def _sample_tables(rel_bias, win, n_new):
    n_heads = rel_bias.shape[1]
    vec_t = rel_bias[_t5_causal_bucket(np.arange(win + n_new))].astype(F32).T
    flipped = vec_t[:, ::-1]
    t = np.arange(n_new)
    d_cache = win + t[:, None] - np.arange(win)[None, :]
    d_new = t[:, None] - t[None, :]
    b_cache = jnp.stack([flipped[:, n_new - 1 - i:n_new - 1 - i + win] for i in range(n_new)], 0)
    b_new = jnp.transpose(vec_t[:, np.maximum(d_new, 0)], (1, 0, 2))
    out = []
    for b, d in ((b_cache, d_cache), (b_new, d_new)):
        mult = _multiplicity(d)
        b = jnp.where((mult > 0)[:, None, :], b, NEG_BIG).reshape(n_new * n_heads, d.shape[1])
        m = np.broadcast_to(np.maximum(mult, 1)[:, None, :], (n_new, n_heads, d.shape[1]))
        out += [b, jnp.asarray(m.reshape(n_new * n_heads, d.shape[1]), F32)]
    return out


def _sample_attn_body(q_ref, kc_ref, kn_ref, vc_ref, vn_ref, bc_ref, mc_ref, bn_ref, mn_ref, hm_ref, o_ref):
    n_new = o_ref.shape[0]
    win = kc_ref.shape[-1]
    qb = (q_ref[...] * HEAD_DIM ** -0.5).astype(BF16)
    s_c = jnp.dot(qb, kc_ref[...].reshape(-1, win).astype(BF16), preferred_element_type=F32) + bc_ref[...]
    s_n = jnp.dot(qb, kn_ref[...].astype(BF16), preferred_element_type=F32) + bn_ref[...]
    m = jnp.maximum(jnp.max(s_c, axis=-1, keepdims=True), jnp.max(s_n, axis=-1, keepdims=True))
    e_c = jnp.exp(s_c - m) * mc_ref[...]
    e_n = jnp.exp(s_n - m) * mn_ref[...]
    den = jnp.sum(e_c, axis=-1, keepdims=True) + jnp.sum(e_n, axis=-1, keepdims=True)
    nt = (((1,), (1,)), ((), ()))
    o = (lax.dot_general(e_c.astype(BF16), vc_ref[...].reshape(-1, win).astype(BF16), nt,
                         preferred_element_type=F32)
         + lax.dot_general(e_n.astype(BF16), vn_ref[...].astype(BF16), nt, preferred_element_type=F32))
    o = o / den * hm_ref[...]
    o_ref[...] = jnp.sum(o.reshape(n_new, -1, o.shape[-1]), axis=1)


def _sample_attn(q, k_new, v_new, cache_k, cache_v, layer, tables):
    depth, batch, n_heads, hd, win = cache_k.shape
    n_new = q.shape[1]
    width = n_heads * hd
    rows = n_new * n_heads
    eye = jnp.eye(n_heads, dtype=F32)
    q_bd = jnp.einsum('bthd,hg->btghd', q, eye).reshape(batch, rows, width)
    kn = jnp.transpose(k_new, (0, 2, 3, 1)).reshape(batch, width, n_new)
    vn = jnp.transpose(v_new, (0, 2, 3, 1)).reshape(batch, width, n_new)
    head_mask = jnp.asarray(np.kron(np.tile(np.eye(n_heads), (n_new, 1)), np.ones((1, hd))), F32)
    cache_spec = pl.BlockSpec((None, None, n_heads, hd, win), lambda b: (layer, b, 0, 0, 0))
    new_spec = pl.BlockSpec((None, width, n_new), lambda b: (b, 0, 0))
    const = lambda a: pl.BlockSpec(a.shape, lambda b: (0, 0))
    return pl.pallas_call(
        _sample_attn_body,
        grid=(batch,),
        in_specs=[pl.BlockSpec((None, rows, width), lambda b: (b, 0, 0)),
                  cache_spec, new_spec, cache_spec, new_spec] + [const(t) for t in tables] + [const(head_mask)],
        out_specs=pl.BlockSpec((None, n_new, width), lambda b: (b, 0, 0)),
        out_shape=jax.ShapeDtypeStruct((batch, n_new, width), F32),
        compiler_params=_compiler_params(("parallel",)),
        name="sample_attn",
    )(q_bd, cache_k, kn, cache_v, vn, *tables, head_mask)


def _s5_param_body(are_ref, aim_ref, ldt_ref, bre_ref, bim_ref, ar_ref, ai_ref, bbre_ref, bbim_ref):
    a_re, a_im = are_ref[...], aim_ref[...]
    dt = jnp.exp(ldt_ref[...])
    mag = jnp.exp(dt * a_re)
    ar, ai = mag * jnp.cos(dt * a_im), mag * jnp.sin(dt * a_im)
    nr, ni = ar - 1.0, ai
    den = a_re * a_re + a_im * a_im
    fr = (nr * a_re + ni * a_im) / den
    fi = (ni * a_re - nr * a_im) / den
    br, bi = bre_ref[...], bim_ref[...]
    ar_ref[...] = ar
    ai_ref[...] = ai
    bbre_ref[...] = fr[:, None, :] * br - fi[:, None, :] * bi
    bbim_ref[...] = fr[:, None, :] * bi + fi[:, None, :] * br


def _s5_params(a_re, a_im, log_dt, b_re, b_im):
    depth, g, p = a_re.shape
    c = b_re.shape[-1]
    n = depth * g
    full = lambda shape: pl.BlockSpec(shape, lambda: (0,) * len(shape))
    ar, ai, bbre, bbim = pl.pallas_call(
        _s5_param_body,
        in_specs=[full((n, p)), full((n, p)), full((n, 1)), full((n, c, p)), full((n, c, p))],
        out_specs=[full((n, p)), full((n, p)), full((n, c, p)), full((n, c, p))],
        out_shape=[jax.ShapeDtypeStruct((n, p), F32), jax.ShapeDtypeStruct((n, p), F32),
                   jax.ShapeDtypeStruct((n, c, p), F32), jax.ShapeDtypeStruct((n, c, p), F32)],
        name="s5_params",
    )(a_re.reshape(n, p), a_im.reshape(n, p), log_dt.reshape(n, 1),
      jnp.swapaxes(b_re, -1, -2).reshape(n, c, p), jnp.swapaxes(b_im, -1, -2).reshape(n, c, p))
    return (ar.reshape(depth, g, p), ai.reshape(depth, g, p),
            bbre.reshape(depth, g, c, p), bbim.reshape(depth, g, c, p))


def _pack_s5_weights(bb_re, bb_im, c_re, c_im):
    g, c, p = bb_re.shape
    tiles = g // GROUPS_PER_TILE
    eye = jnp.eye(GROUPS_PER_TILE, dtype=F32)

    def inp(w):
        w = w.reshape(tiles, GROUPS_PER_TILE, c, p)
        return jnp.einsum('jgcp,gh->jgchp', w, eye).reshape(tiles, LANES, STATE_TILE)

    def outp(w):
        w = w.reshape(tiles, GROUPS_PER_TILE, c, p)
        return jnp.einsum('jgcp,gh->jgphc', w, eye).reshape(tiles, STATE_TILE, LANES)

    wx = jnp.concatenate([inp(bb_re), inp(bb_im)], axis=-1).astype(BF16)
    wc = jnp.concatenate([outp(c_re), -outp(c_im)], axis=1).astype(BF16)
    return wx, wc


def _pack_state(h_re, h_im):
    b, g, p = h_re.shape
    tiles = g // GROUPS_PER_TILE
    st = jnp.stack([h_re.reshape(b, tiles, STATE_TILE), h_im.reshape(b, tiles, STATE_TILE)], axis=2)
    return st.reshape(b, tiles * 2 * STATE_TILE).astype(F32)


def _unpack_state(h, g, p):
    b = h.shape[0]
    st = h.reshape(b, g // GROUPS_PER_TILE, 2, STATE_TILE)
    return st[:, :, 0].reshape(b, g, p), st[:, :, 1].reshape(b, g, p)


def _s5_scan_body(u_ref, wx_ref, wc_ref, ar_ref, ai_ref, d_ref, h0_ref, y_ref, hout_ref, us_ref, ys_ref, xs_ref,
                  h_ref, *, bsz, steps, batch_on_lanes):
    tiles = wx_ref.shape[0]
    ch = tiles * LANES

    @pl.when(pl.program_id(0) == 0)
    def _():
        h_ref[...] = h0_ref[...]

    for j in range(tiles):
        if batch_on_lanes:
            for b in range(bsz):
                us_ref[j, pl.ds(b, steps, stride=bsz), :] = u_ref[:, ch * b + LANES * j:ch * b + LANES * (j + 1)]
        else:
            us_ref[j] = u_ref[:, LANES * j:LANES * (j + 1)]

    for j in range(tiles):
        xs_ref[:, 2 * STATE_TILE * j:2 * STATE_TILE * (j + 1)] = jnp.dot(
            us_ref[j].astype(BF16), wx_ref[j], preferred_element_type=F32)

    for j in range(tiles):
        re = slice(2 * STATE_TILE * j, 2 * STATE_TILE * j + STATE_TILE)
        im = slice(2 * STATE_TILE * j + STATE_TILE, 2 * STATE_TILE * (j + 1))
        a_r = jnp.broadcast_to(ar_ref[:, STATE_TILE * j:STATE_TILE * (j + 1)], (bsz, STATE_TILE))
        a_i = jnp.broadcast_to(ai_ref[:, STATE_TILE * j:STATE_TILE * (j + 1)], (bsz, STATE_TILE))

        def step(t, carry, re=re, im=im, a_r=a_r, a_i=a_i):
            hr, hi = carry
            r = pl.ds(pl.multiple_of(t * bsz, bsz), bsz)
            nhr = a_r * hr - a_i * hi + xs_ref[r, re]
            nhi = a_r * hi + a_i * hr + xs_ref[r, im]
            xs_ref[r, re] = nhr
            xs_ref[r, im] = nhi
            return nhr, nhi

        hr, hi = lax.fori_loop(0, steps, step, (h_ref[:, re], h_ref[:, im]))
        h_ref[:, re] = hr
        h_ref[:, im] = hi

    for j in range(tiles):
        cols = slice(LANES * j, LANES * (j + 1))
        hb = xs_ref[:, 2 * STATE_TILE * j:2 * STATE_TILE * (j + 1)].astype(BF16)
        y = jnp.dot(hb, wc_ref[j], preferred_element_type=F32) + d_ref[:, cols] * us_ref[j]
        if batch_on_lanes:
            ys_ref[j] = y
            for b in range(bsz):
                y_ref[:, ch * b + LANES * j:ch * b + LANES * (j + 1)] = ys_ref[j, pl.ds(b, steps, stride=bsz), :]
        else:
            y_ref[:, cols] = y
    hout_ref[...] = h_ref[...]


def _s5_scan(u, wx, wc, ar, ai, d_skip, h0, bsz, steps, batch_on_lanes):
    tiles = wx.shape[0]
    ch = tiles * LANES
    width = tiles * 2 * STATE_TILE
    rows = steps * bsz
    if batch_on_lanes:
        n_chunks = u.shape[0] // steps
        io_spec = pl.BlockSpec((steps, bsz * ch), lambda i: (i, 0))
    else:
        n_chunks = 1
        assert u.shape == (rows, ch)
        io_spec = pl.BlockSpec((rows, ch), lambda i: (0, 0))
    const = lambda shape: pl.BlockSpec(shape, lambda i: (0,) * len(shape))
    return pl.pallas_call(
        functools.partial(_s5_scan_body, bsz=bsz, steps=steps, batch_on_lanes=batch_on_lanes),
        grid=(n_chunks,),
        in_specs=[io_spec, const(wx.shape), const(wc.shape), const((1, tiles * STATE_TILE)),
                  const((1, tiles * STATE_TILE)), const((1, ch)), const((bsz, width))],
        out_specs=[io_spec, const((bsz, width))],
        out_shape=[jax.ShapeDtypeStruct(u.shape, F32), jax.ShapeDtypeStruct((bsz, width), F32)],
        scratch_shapes=[pltpu.VMEM((tiles, rows, LANES), F32), pltpu.VMEM((tiles, rows, LANES), F32),
                        pltpu.VMEM((rows, width), F32), pltpu.VMEM((bsz, width), F32)],
        compiler_params=_compiler_params(("arbitrary",)),
        name="s5_scan",
    )(u, wx, wc, ar.reshape(1, -1), ai.reshape(1, -1), d_skip.reshape(1, ch), h0)


def _post_mix_body(attn_ref, y_ref, x_ref, ga_ref, gs_ref, wglu_ref, bglu_ref, wo_ref, o_ref):
    aw = attn_ref.shape[1]
    an = _rms(attn_ref[...], ga_ref[...])
    z = jax.nn.gelu(y_ref[...])
    gate = jax.nn.sigmoid(jnp.dot(z.astype(BF16), wglu_ref[...], preferred_element_type=F32) + bglu_ref[...])
    sn = _rms(z * gate, gs_ref[...])
    o_ref[...] = (x_ref[...]
                  + jnp.dot(an.astype(BF16), wo_ref[:aw, :], preferred_element_type=F32)
                  + jnp.dot(sn.astype(BF16), wo_ref[aw:, :], preferred_element_type=F32))


def _post_mix(attn, y, x, g_attn, g_ssm, w_glu, b_glu, w_out, layer, tm, y_tiles_per_seq):
    n, d = x.shape
    aw, sw = attn.shape[1], w_glu.shape[1]
    row = lambda w: pl.BlockSpec((tm, w), lambda i: (i, 0))
    vec = lambda w: pl.BlockSpec((None, 1, w), lambda i: (layer, 0, 0))
    mat = lambda r, c: pl.BlockSpec((None, r, c), lambda i: (layer, 0, 0))
    if y_tiles_per_seq is None:
        y_spec = row(sw)
    else:
        y_spec = pl.BlockSpec((tm, sw), lambda i: (i % y_tiles_per_seq, i // y_tiles_per_seq))
    return pl.pallas_call(
        _post_mix_body,
        grid=(n // tm,),
        in_specs=[row(aw), y_spec, row(d), vec(aw), vec(sw), mat(sw, sw), vec(sw), mat(aw + sw, d)],
        out_specs=row(d),
        out_shape=jax.ShapeDtypeStruct((n, d), F32),
        compiler_params=_compiler_params(("parallel",)),
        name="post_mix",
    )(attn, y, x, g_attn, g_ssm, w_glu, b_glu, w_out)


def _finish(x_ref, acc_ref, gf_ref, o_ref):
    y = x_ref[...] + acc_ref[...]
    o_ref[...] = y if gf_ref is None else _rms(y, gf_ref[...])


def _ffn_body(x_ref, g_ref, wg_ref, wu_ref, wd_ref, *rest, final):
    gf_ref = rest[0] if final else None
    o_ref, hn_ref, acc_ref = rest[-3:]
    f = pl.program_id(1)

    @pl.when(f == 0)
    def _():
        hn_ref[...] = _rms(x_ref[...], g_ref[...]).astype(BF16)
        acc_ref[...] = jnp.zeros_like(acc_ref)

    hn = hn_ref[...]
    gate = jnp.dot(hn, wg_ref[...], preferred_element_type=F32)
    up = jnp.dot(hn, wu_ref[...], preferred_element_type=F32)
    act = (jax.nn.silu(gate) * up).astype(BF16)
    acc_ref[...] += jnp.dot(act, wd_ref[...], preferred_element_type=F32)

    @pl.when(f == pl.num_programs(1) - 1)
    def _():
        _finish(x_ref, acc_ref, gf_ref, o_ref)


def _ffn(x, g, wg, wu, wd, layer, g_final, tm, tf):
    n, d = x.shape
    ff = wg.shape[2]
    final = g_final is not None
    blk = layer // 2
    in_specs = [pl.BlockSpec((tm, d), lambda i, f: (i, 0)),
                pl.BlockSpec((None, 1, d), lambda i, f: (layer, 0, 0)),
                pl.BlockSpec((None, d, tf), lambda i, f: (blk, 0, f)),
                pl.BlockSpec((None, d, tf), lambda i, f: (blk, 0, f)),
                pl.BlockSpec((None, tf, d), lambda i, f: (blk, f, 0))]
    args = [x, g, wg, wu, wd]
    if final:
        in_specs.append(pl.BlockSpec((1, d), lambda i, f: (0, 0)))
        args.append(g_final.reshape(1, d))
    return pl.pallas_call(
        functools.partial(_ffn_body, final=final),
        grid=(n // tm, ff // tf),
        in_specs=in_specs,
        out_specs=pl.BlockSpec((tm, d), lambda i, f: (i, 0)),
        out_shape=jax.ShapeDtypeStruct((n, d), F32),
        scratch_shapes=[pltpu.VMEM((tm, d), BF16), pltpu.VMEM((tm, d), F32)],
        compiler_params=_compiler_params(("parallel", "arbitrary")),
        name="ffn_dense",
    )(*args)


def _moe_body(x_ref, g_ref, wr_ref, br_ref, wg_ref, wu_ref, wd_ref, *rest, n_experts, final):
    gf_ref = rest[0] if final else None
    o_ref, hn_ref, comb_ref, acc_ref = rest[-4:]
    e, f = pl.program_id(1), pl.program_id(2)
    lane = lax.broadcasted_iota(jnp.int32, comb_ref.shape, 1).astype(F32)

    @pl.when((e == 0) & (f == 0))
    def _():
        hn = _rms(x_ref[...], g_ref[...])
        hn_ref[...] = hn.astype(BF16)
        acc_ref[...] = jnp.zeros_like(acc_ref)
        logits = jnp.dot(hn, wr_ref[...], preferred_element_type=F32,
                         precision=lax.Precision.HIGHEST) + br_ref[...]
        logits = jnp.where(lane < n_experts, logits, -jnp.inf)
        m1 = jnp.max(logits, axis=-1, keepdims=True)
        i1 = jnp.min(jnp.where(logits == m1, lane, float(LANES)), axis=-1, keepdims=True)
        rest_logits = jnp.where(lane == i1, -jnp.inf, logits)
        m2 = jnp.max(rest_logits, axis=-1, keepdims=True)
        i2 = jnp.min(jnp.where(rest_logits == m2, lane, float(LANES)), axis=-1, keepdims=True)
        e2 = jnp.exp(m2 - m1)
        g1 = 1.0 / (1.0 + e2)
        g2 = e2 / (1.0 + e2)
        comb_ref[...] = jnp.where(lane == i1, g1, 0.0) + jnp.where(lane == i2, g2, 0.0)

    hn = hn_ref[...]
    gate = jnp.dot(hn, wg_ref[...], preferred_element_type=F32)
    up = jnp.dot(hn, wu_ref[...], preferred_element_type=F32)
    act = (jax.nn.silu(gate) * up).astype(BF16)
    weight = jnp.sum(jnp.where(lane == e.astype(F32), comb_ref[...], 0.0), axis=-1, keepdims=True)
    acc_ref[...] += weight * jnp.dot(act, wd_ref[...], preferred_element_type=F32)

    @pl.when((e == n_experts - 1) & (f == pl.num_programs(2) - 1))
    def _():
        _finish(x_ref, acc_ref, gf_ref, o_ref)


def _moe(x, g, w_router, b_router, wg, wu, wd, layer, g_final, tm, tf):
    n, d = x.shape
    _, n_experts, _, ff = wg.shape
    final = g_final is not None
    blk = layer // 2
    wr = jnp.zeros((d, LANES), F32).at[:, :n_experts].set(w_router.astype(F32))
    br = jnp.zeros((1, LANES), F32).at[0, :n_experts].set(b_router.astype(F32))
    in_specs = [pl.BlockSpec((tm, d), lambda i, e, f: (i, 0)),
                pl.BlockSpec((None, 1, d), lambda i, e, f: (layer, 0, 0)),
                pl.BlockSpec((d, LANES), lambda i, e, f: (0, 0)),
                pl.BlockSpec((1, LANES), lambda i, e, f: (0, 0)),
                pl.BlockSpec((None, None, d, tf), lambda i, e, f: (blk, e, 0, f)),
                pl.BlockSpec((None, None, d, tf), lambda i, e, f: (blk, e, 0, f)),
                pl.BlockSpec((None, None, tf, d), lambda i, e, f: (blk, e, f, 0))]
    args = [x, g, wr, br, wg, wu, wd]
    if final:
        in_specs.append(pl.BlockSpec((1, d), lambda i, e, f: (0, 0)))
        args.append(g_final.reshape(1, d))
    return pl.pallas_call(
        functools.partial(_moe_body, n_experts=n_experts, final=final),
        grid=(n // tm, n_experts, ff // tf),
        in_specs=in_specs,
        out_specs=pl.BlockSpec((tm, d), lambda i, e, f: (i, 0)),
        out_shape=jax.ShapeDtypeStruct((n, d), F32),
        scratch_shapes=[pltpu.VMEM((tm, d), BF16), pltpu.VMEM((tm, LANES), F32), pltpu.VMEM((tm, d), F32)],
        compiler_params=_compiler_params(("parallel", "arbitrary", "arbitrary")),
        name="moe",
    )(*args)


PROMPT_ROW_TILE = 512
PROMPT_SCAN_STEPS = 64


def kernel(x_prompt, x_sample, cache_win_k, cache_win_v, state_ssm_re, state_ssm_im, rel_bias, norm_mix, w_in, ssm_a_re, ssm_a_im, ssm_log_dt, ssm_b_re, ssm_b_im, ssm_c_re, ssm_c_im, ssm_d, ssm_w_glu, ssm_b_glu, norm_attn_out, norm_ssm_out, w_out, norm_ffn, ffn_w_gate, ffn_w_up, ffn_w_down, moe_w_router, moe_b_router, moe_w_gate, moe_w_up, moe_w_down, norm_final):
    batch, seq, d_model = x_prompt.shape
    dec_batch, dec_seq, _ = x_sample.shape
    depth = w_in.shape[0]
    win, n_heads = cache_win_k.shape[2:4]
    attn_width = n_heads * HEAD_DIM
    n_groups, n_state = ssm_a_re.shape[1:]
    ssm_width = n_groups * SSM_GROUP
    assert w_in.shape[2] == 3 * attn_width + ssm_width and seq == MAX_WINDOW
    dec_rows = dec_batch * dec_seq

    band_bias = _band_bias(rel_bias)
    sample_tabs = _sample_tables(rel_bias, win, dec_seq)
    ar, ai, bb_re, bb_im = _s5_params(ssm_a_re, ssm_a_im, ssm_log_dt, ssm_b_re, ssm_b_im)
    cache_k = jnp.transpose(cache_win_k, (0, 1, 3, 4, 2))
    cache_v = jnp.transpose(cache_win_v, (0, 1, 3, 4, 2))

    rows3 = lambda a: a.reshape(a.shape[0], 1, a.shape[1])
    norm_mix3, norm_ffn3, b_glu3 = rows3(norm_mix), rows3(norm_ffn), rows3(ssm_b_glu)
    g_attn3, g_ssm3 = rows3(norm_attn_out), rows3(norm_ssm_out)
    w_in_b, w_out_b, w_glu_b = w_in.astype(BF16), w_out.astype(BF16), ssm_w_glu.astype(BF16)
    ffn_b = [w.astype(BF16) for w in (ffn_w_gate, ffn_w_up, ffn_w_down)]
    moe_b = [w.astype(BF16) for w in (moe_w_gate, moe_w_up, moe_w_down)]
    tf = ffn_w_gate.shape[2] // 2

    def mix_and_ffn(x, l, attn, y, tm, y_tiles_per_seq):
        g_final = norm_final if l == depth - 1 else None
        x = _post_mix(attn, y, x, g_attn3, g_ssm3, w_glu_b, b_glu3, w_out_b, l, tm, y_tiles_per_seq)
        if l % 2 == 0:
            return _ffn(x, norm_ffn3, *ffn_b, l, g_final, tm, tf)
        return _moe(x, norm_ffn3, moe_w_router[l // 2], moe_b_router[l // 2], *moe_b, l, g_final, tm, tf)

    xp = x_prompt.reshape(batch * seq, d_model)
    xs = x_sample.reshape(dec_rows, d_model)
    zero_state = jnp.zeros((batch, 2 * n_groups * n_state), F32)
    outs = [[] for _ in range(8)]
    for l in range(depth):
        wx, wc = _pack_s5_weights(bb_re[l], bb_im[l], ssm_c_re[l].astype(F32), ssm_c_im[l].astype(F32))

        qkv, k_t, v_t, u = _norm_proj_prompt(xp, norm_mix3, w_in_b, l, batch, seq, attn_width, PROMPT_ROW_TILE)
        attn = _prompt_attn(qkv, band_bias, batch, seq, attn_width)
        y, h_fin = _s5_scan(u, wx, wc, ar[l], ai[l], ssm_d[l], zero_state, batch, PROMPT_SCAN_STEPS, True)
        xp = mix_and_ffn(xp, l, attn, y, PROMPT_ROW_TILE, seq // PROMPT_ROW_TILE)
        outs[0].append(k_t)
        outs[1].append(v_t)
        h_re, h_im = _unpack_state(h_fin, n_groups, n_state)
        outs[2].append(h_re)
        outs[3].append(h_im)

        proj = _norm_proj(xs, norm_mix3, w_in_b, l, dec_rows)
        q, k_new, v_new = (proj[:, i * attn_width:(i + 1) * attn_width].reshape(dec_batch, dec_seq, n_heads, HEAD_DIM)
                           for i in range(3))
        attn = _sample_attn(q, k_new, v_new, cache_k, cache_v, l, sample_tabs).reshape(dec_rows, attn_width)
        u = proj[:, 3 * attn_width:].reshape(dec_batch, dec_seq, ssm_width).transpose(1, 0, 2)
        y, h_fin = _s5_scan(u.reshape(dec_rows, ssm_width), wx, wc, ar[l], ai[l], ssm_d[l],
                            _pack_state(state_ssm_re[l], state_ssm_im[l]), dec_batch, dec_seq, False)
        y = y.reshape(dec_seq, dec_batch, ssm_width).transpose(1, 0, 2).reshape(dec_rows, ssm_width)
        xs = mix_and_ffn(xs, l, attn, y, dec_rows, None)
        outs[4].append(k_new)
        outs[5].append(v_new)
        h_re, h_im = _unpack_state(h_fin, n_groups, n_state)
        outs[6].append(h_re)
        outs[7].append(h_im)

    def window_out(parts):
        a = jnp.stack(parts, 0).reshape(depth, batch, n_heads, HEAD_DIM, seq)
        return jnp.transpose(a, (0, 1, 4, 2, 3))

    return (xp.reshape(batch, seq, d_model), xs.reshape(dec_batch, dec_seq, d_model),
            window_out(outs[0]), window_out(outs[1]), *[jnp.stack(o, 0) for o in outs[2:]])
```

```python
import functools

import numpy as np
import jax
import jax.numpy as jnp
from jax import lax
from jax.experimental import pallas as pl
from jax.experimental.pallas import tpu as pltpu

F32 = jnp.float32
BF16 = jnp.bfloat16

HEAD_DIM = 64
DILATION_PATTERNS = ((128, 1), (512, 4), (2048, 16))
MAX_WINDOW = 2048
N_BUCKETS = 32
SSM_GROUP = 16
SSM_STATE = 64
RMS_EPS = 1e-6
NEG_BIG = -1e30

LANES = 128
Q_TILE = 128
ATTN_UNROLL = 4
GROUPS_PER_TILE = LANES // SSM_GROUP
STATE_TILE = GROUPS_PER_TILE * SSM_STATE
VMEM_LIMIT_BYTES = 56 * 1024 * 1024


def _compiler_params(semantics):
    return pltpu.CompilerParams(dimension_semantics=semantics, vmem_limit_bytes=VMEM_LIMIT_BYTES)


def _rms(x, g):
    return x * lax.rsqrt(jnp.mean(x * x, axis=-1, keepdims=True) + RMS_EPS) * g


def _t5_causal_bucket(dist):
    max_exact = N_BUCKETS // 2
    d = np.asarray(dist, dtype=np.int64)
    large = max_exact + (np.log(np.maximum(d, max_exact) / max_exact)
                         / np.log(MAX_WINDOW / max_exact)
                         * (N_BUCKETS - max_exact)).astype(np.int64)
    large = np.minimum(large, N_BUCKETS - 1)
    return np.where(d < max_exact, d, large).astype(np.int32)


def _multiplicity(d):
    m = np.zeros(d.shape, np.int32)
    for window, dil in DILATION_PATTERNS:
        m += ((d >= 0) & (d <= window) & (d % dil == 0)).astype(np.int32)
    return m


def _norm_proj_body(x_ref, g_ref, w_ref, o_ref):
    hn = _rms(x_ref[...], g_ref[...])
    o_ref[...] = jnp.dot(hn.astype(BF16), w_ref[...], preferred_element_type=F32)


def _norm_proj(x, g, w, layer, tm):
    n, d = x.shape
    c = w.shape[2]
    return pl.pallas_call(
        _norm_proj_body,
        grid=(n // tm,),
        in_specs=[pl.BlockSpec((tm, d), lambda i: (i, 0)),
                  pl.BlockSpec((None, 1, d), lambda i: (layer, 0, 0)),
                  pl.BlockSpec((None, d, c), lambda i: (layer, 0, 0))],
        out_specs=pl.BlockSpec((tm, c), lambda i: (i, 0)),
        out_shape=jax.ShapeDtypeStruct((n, c), F32),
        compiler_params=_compiler_params(("parallel",)),
        name="norm_proj",
    )(x, g, w)


def _norm_proj_prompt_body(x_ref, g_ref, w_ref, qkv_ref, kt_ref, vt_ref, u_ref, *, attn_width):
    hn = _rms(x_ref[...], g_ref[...])
    proj = jnp.dot(hn.astype(BF16), w_ref[...], preferred_element_type=F32)
    qkv_ref[...] = proj[:, :3 * attn_width]
    kt_ref[...] = proj[:, attn_width:2 * attn_width].T
    vt_ref[...] = proj[:, 2 * attn_width:3 * attn_width].T
    u_ref[...] = proj[:, 3 * attn_width:]


def _norm_proj_prompt(x, g, w, layer, batch, seq, attn_width, tm):
    n, d = x.shape
    c = w.shape[2]
    ch = c - 3 * attn_width
    per_seq = seq // tm
    return pl.pallas_call(
        functools.partial(_norm_proj_prompt_body, attn_width=attn_width),
        grid=(n // tm,),
        in_specs=[pl.BlockSpec((tm, d), lambda i: (i, 0)),
                  pl.BlockSpec((None, 1, d), lambda i: (layer, 0, 0)),
                  pl.BlockSpec((None, d, c), lambda i: (layer, 0, 0))],
        out_specs=[pl.BlockSpec((tm, 3 * attn_width), lambda i: (i, 0)),
                   pl.BlockSpec((None, attn_width, tm), lambda i: (i // per_seq, 0, i % per_seq)),
                   pl.BlockSpec((None, attn_width, tm), lambda i: (i // per_seq, 0, i % per_seq)),
                   pl.BlockSpec((tm, ch), lambda i: (i % per_seq, i // per_seq))],
        out_shape=[jax.ShapeDtypeStruct((n, 3 * attn_width), F32),
                   jax.ShapeDtypeStruct((batch, attn_width, seq), F32),
                   jax.ShapeDtypeStruct((batch, attn_width, seq), F32),
                   jax.ShapeDtypeStruct((seq, batch * ch), F32)],
        compiler_params=_compiler_params(("parallel",)),
        name="norm_proj_prompt",
    )(x, g, w)


def _band_bias(rel_bias):
    period = 3 * Q_TILE
    n_heads = rel_bias.shape[1]
    rows0 = []
    for window, dil in DILATION_PATTERNS:
        assert window // dil == Q_TILE
        vec = rel_bias[_t5_causal_bucket((Q_TILE - np.arange(Q_TILE + 1)) * dil)].astype(F32)
        rows0.append(jnp.concatenate([vec.T, jnp.full((n_heads, period - Q_TILE - 1), NEG_BIG, F32)], axis=1))
    row0 = jnp.stack(rows0, 0)
    flat = jnp.tile(row0, (1, 1, Q_TILE))[:, :, :Q_TILE * (period - 1)]
    return flat.reshape(len(DILATION_PATTERNS), n_heads, Q_TILE, period - 1)[:, :, :, :2 * Q_TILE]


def _prompt_attn_body(q_ref, k_ref, v_ref, bias_ref, o_ref, op_ref, lse_ref):
    seq = q_ref.shape[0]
    lane = lax.broadcasted_iota(jnp.int32, (Q_TILE, LANES), 1)
    head0 = lane < HEAD_DIM
    scale = HEAD_DIM ** -0.5

    def rows(start, size, dil):
        return pl.ds(start, size) if dil == 1 else pl.ds(start, size, stride=dil)

    def tiles(p, dil, starts, first):
        nk = Q_TILE if first else 2 * Q_TILE
        work = []
        for q_start, k_start in starts:
            qsl, ksl = rows(q_start, Q_TILE, dil), rows(k_start, nk, dil)
            qt = q_ref[qsl, :] * scale
            kt = k_ref[ksl, :].astype(BF16)
            vt = v_ref[ksl, :].astype(BF16)
            for h in range(2):
                qh = jnp.where(head0 if h == 0 else jnp.logical_not(head0), qt, 0.0).astype(BF16)
                work.append([qsl, h, vt, qh, kt])
        for w in work:
            s = lax.dot_general(w[3], w[4], (((1,), (1,)), ((), ())), preferred_element_type=F32)
            w[3] = s + (bias_ref[p, w[1], :, Q_TILE:] if first else bias_ref[p, w[1]])
        for w in work:
            w[4] = jnp.max(w[3], axis=-1, keepdims=True)
        for w in work:
            w[3] = jnp.exp(w[3] - w[4])
        for w in work:
            l = jnp.sum(w[3], axis=-1, keepdims=True)
            o = jnp.dot(w[3].astype(BF16), w[2], preferred_element_type=F32)
            w[3] = o / l
            w[4] = w[4] + jnp.log(l)
        for w0, w1 in zip(work[0::2], work[1::2]):
            op_ref[p, w0[0], :] = jnp.where(head0, w0[3], w1[3])
            lse_ref[p, w0[0], :] = jnp.where(head0, w0[4], w1[4])

    for p, (_, dil) in enumerate(DILATION_PATTERNS):
        n_tiles = seq // dil // Q_TILE
        stride = dil * Q_TILE
        unroll = ATTN_UNROLL
        if dil < unroll:
            unroll = max(u for u in range(1, ATTN_UNROLL + 1) if (n_tiles - 1) % u == 0)
            for r in range(dil):
                tiles(p, dil, [(r, r)], True)

                def blk(t, c, p=p, dil=dil, r=r, stride=stride, unroll=unroll):
                    first_t = 1 + t * unroll
                    tiles(p, dil, [(r + stride * (first_t + j), r + stride * (first_t + j - 1))
                                   for j in range(unroll)], False)
                    return c
                lax.fori_loop(0, (n_tiles - 1) // unroll, blk, 0)
        else:
            assert dil % unroll == 0

            def classes(g, c, p=p, dil=dil, n_tiles=n_tiles, stride=stride, unroll=unroll):
                rs = [g * unroll + j for j in range(unroll)]
                tiles(p, dil, [(r, r) for r in rs], True)
                if n_tiles > 1:
                    def blk(t, c2):
                        tiles(p, dil, [(r + stride * t, r + stride * (t - 1)) for r in rs], False)
                        return c2
                    lax.fori_loop(1, n_tiles, blk, 0)
                return c
            lax.fori_loop(0, dil // unroll, classes, 0)

    def combine(t, carry):
        r = pl.ds(pl.multiple_of(t * Q_TILE, Q_TILE), Q_TILE)
        lse = [lse_ref[p, r, :] for p in range(len(DILATION_PATTERNS))]
        mx = functools.reduce(jnp.maximum, lse)
        ws = [jnp.exp(x - mx) for x in lse]
        num = sum(w * op_ref[p, r, :] for p, w in enumerate(ws))
        o_ref[r, :] = num / sum(ws)
        return carry

    lax.fori_loop(0, seq // Q_TILE, combine, 0)


def _prompt_attn(proj, bias, batch, seq, attn_width):
    n_pairs = attn_width // LANES
    n_pat = len(DILATION_PATTERNS)
    assert seq % (Q_TILE * max(d for _, d in DILATION_PATTERNS)) == 0
    return pl.pallas_call(
        _prompt_attn_body,
        grid=(batch, n_pairs),
        in_specs=[pl.BlockSpec((seq, LANES), lambda b, h: (b, h)),
                  pl.BlockSpec((seq, LANES), lambda b, h: (b, n_pairs + h)),
                  pl.BlockSpec((seq, LANES), lambda b, h: (b, 2 * n_pairs + h)),
                  pl.BlockSpec((n_pat, 2, Q_TILE, 2 * Q_TILE), lambda b, h: (0, h, 0, 0))],
        out_specs=pl.BlockSpec((seq, LANES), lambda b, h: (b, h)),
        out_shape=jax.ShapeDtypeStruct((batch * seq, attn_width), F32),
        scratch_shapes=[pltpu.VMEM((n_pat, seq, LANES), F32),
                        pltpu.VMEM((n_pat, seq, LANES), F32)],
        compiler_params=_compiler_params(("parallel", "parallel")),
        name="prompt_attn",
    )(proj, proj, proj, bias)


def _sample_tables(rel_bias, win, n_new):
    n_heads = rel_bias.shape[1]
    vec_t = rel_bias[_t5_causal_bucket(np.arange(win + n_new))].astype(F32).T
    flipped = vec_t[:, ::-1]
    t = np.arange(n_new)
    d_cache = win + t[:, None] - np.arange(win)[None, :]
    d_new = t[:, None] - t[None, :]
    b_cache = jnp.stack([flipped[:, n_new - 1 - i:n_new - 1 - i + win] for i in range(n_new)], 0)
    b_new = jnp.transpose(vec_t[:, np.maximum(d_new, 0)], (1, 0, 2))
    out = []
    for b, d in ((b_cache, d_cache), (b_new, d_new)):
        mult = _multiplicity(d)
        b = jnp.where((mult > 0)[:, None, :], b, NEG_BIG).reshape(n_new * n_heads, d.shape[1])
        m = np.broadcast_to(np.maximum(mult, 1)[:, None, :], (n_new, n_heads, d.shape[1]))
        out += [b, jnp.asarray(m.reshape(n_new * n_heads, d.shape[1]), F32)]
    return out


def _sample_attn_body(q_ref, kc_ref, kn_ref, vc_ref, vn_ref, bc_ref, mc_ref, bn_ref, mn_ref, hm_ref, o_ref):
    n_new = o_ref.shape[0]
    win = kc_ref.shape[-1]
    qb = (q_ref[...] * HEAD_DIM ** -0.5).astype(BF16)
    s_c = jnp.dot(qb, kc_ref[...].reshape(-1, win).astype(BF16), preferred_element_type=F32) + bc_ref[...]
    s_n = jnp.dot(qb, kn_ref[...].astype(BF16), preferred_element_type=F32) + bn_ref[...]
    m = jnp.maximum(jnp.max(s_c, axis=-1, keepdims=True), jnp.max(s_n, axis=-1, keepdims=True))
    e_c = jnp.exp(s_c - m) * mc_ref[...]
    e_n = jnp.exp(s_n - m) * mn_ref[...]
    den = jnp.sum(e_c, axis=-1, keepdims=True) + jnp.sum(e_n, axis=-1, keepdims=True)
    nt = (((1,), (1,)), ((), ()))
    o = (lax.dot_general(e_c.astype(BF16), vc_ref[...].reshape(-1, win).astype(BF16), nt,
                         preferred_element_type=F32)
         + lax.dot_general(e_n.astype(BF16), vn_ref[...].astype(BF16), nt, preferred_element_type=F32))
    o = o / den * hm_ref[...]
    o_ref[...] = jnp.sum(o.reshape(n_new, -1, o.shape[-1]), axis=1)


def _sample_attn(q, k_new, v_new, cache_k, cache_v, layer, tables):
    depth, batch, n_heads, hd, win = cache_k.shape
    n_new = q.shape[1]
    width = n_heads * hd
    rows = n_new * n_heads
    eye = jnp.eye(n_heads, dtype=F32)
    q_bd = jnp.einsum('bthd,hg->btghd', q, eye).reshape(batch, rows, width)
    kn = jnp.transpose(k_new, (0, 2, 3, 1)).reshape(batch, width, n_new)
    vn = jnp.transpose(v_new, (0, 2, 3, 1)).reshape(batch, width, n_new)
    head_mask = jnp.asarray(np.kron(np.tile(np.eye(n_heads), (n_new, 1)), np.ones((1, hd))), F32)
    cache_spec = pl.BlockSpec((None, None, n_heads, hd, win), lambda b: (layer, b, 0, 0, 0))
    new_spec = pl.BlockSpec((None, width, n_new), lambda b: (b, 0, 0))
    const = lambda a: pl.BlockSpec(a.shape, lambda b: (0, 0))
    return pl.pallas_call(
        _sample_attn_body,
        grid=(batch,),
        in_specs=[pl.BlockSpec((None, rows, width), lambda b: (b, 0, 0)),
                  cache_spec, new_spec, cache_spec, new_spec] + [const(t) for t in tables] + [const(head_mask)],
        out_specs=pl.BlockSpec((None, n_new, width), lambda b: (b, 0, 0)),
        out_shape=jax.ShapeDtypeStruct((batch, n_new, width), F32),
        compiler_params=_compiler_params(("parallel",)),
        name="sample_attn",
    )(q_bd, cache_k, kn, cache_v, vn, *tables, head_mask)


def _s5_param_body(are_ref, aim_ref, ldt_ref, bre_ref, bim_ref, ar_ref, ai_ref, bbre_ref, bbim_ref):
    a_re, a_im = are_ref[...], aim_ref[...]
    dt = jnp.exp(ldt_ref[...])
    mag = jnp.exp(dt * a_re)
    ar, ai = mag * jnp.cos(dt * a_im), mag * jnp.sin(dt * a_im)
    nr, ni = ar - 1.0, ai
    den = a_re * a_re + a_im * a_im
    fr = (nr * a_re + ni * a_im) / den
    fi = (ni * a_re - nr * a_im) / den
    br, bi = bre_ref[...], bim_ref[...]
    ar_ref[...] = ar
    ai_ref[...] = ai
    bbre_ref[...] = fr[:, None, :] * br - fi[:, None, :] * bi
    bbim_ref[...] = fr[:, None, :] * bi + fi[:, None, :] * br


def _s5_params(a_re, a_im, log_dt, b_re, b_im):
    depth, g, p = a_re.shape
    c = b_re.shape[-1]
    n = depth * g
    full = lambda shape: pl.BlockSpec(shape, lambda: (0,) * len(shape))
    ar, ai, bbre, bbim = pl.pallas_call(
        _s5_param_body,
        in_specs=[full((n, p)), full((n, p)), full((n, 1)), full((n, c, p)), full((n, c, p))],
        out_specs=[full((n, p)), full((n, p)), full((n, c, p)), full((n, c, p))],
        out_shape=[jax.ShapeDtypeStruct((n, p), F32), jax.ShapeDtypeStruct((n, p), F32),
                   jax.ShapeDtypeStruct((n, c, p), F32), jax.ShapeDtypeStruct((n, c, p), F32)],
        name="s5_params",
    )(a_re.reshape(n, p), a_im.reshape(n, p), log_dt.reshape(n, 1),
      jnp.swapaxes(b_re, -1, -2).reshape(n, c, p), jnp.swapaxes(b_im, -1, -2).reshape(n, c, p))
    return (ar.reshape(depth, g, p), ai.reshape(depth, g, p),
            bbre.reshape(depth, g, c, p), bbim.reshape(depth, g, c, p))


def _pack_s5_weights(bb_re, bb_im, c_re, c_im):
    g, c, p = bb_re.shape
    tiles = g // GROUPS_PER_TILE
    eye = jnp.eye(GROUPS_PER_TILE, dtype=F32)

    def inp(w):
        w = w.reshape(tiles, GROUPS_PER_TILE, c, p)
        return jnp.einsum('jgcp,gh->jgchp', w, eye).reshape(tiles, LANES, STATE_TILE)

    def outp(w):
        w = w.reshape(tiles, GROUPS_PER_TILE, c, p)
        return jnp.einsum('jgcp,gh->jgphc', w, eye).reshape(tiles, STATE_TILE, LANES)

    wx = jnp.concatenate([inp(bb_re), inp(bb_im)], axis=-1).astype(BF16)
    wc = jnp.concatenate([outp(c_re), -outp(c_im)], axis=1).astype(BF16)
    return wx, wc


def _pack_state(h_re, h_im):
    b, g, p = h_re.shape
    tiles = g // GROUPS_PER_TILE
    st = jnp.stack([h_re.reshape(b, tiles, STATE_TILE), h_im.reshape(b, tiles, STATE_TILE)], axis=2)
    return st.reshape(b, tiles * 2 * STATE_TILE).astype(F32)


def _unpack_state(h, g, p):
    b = h.shape[0]
    st = h.reshape(b, g // GROUPS_PER_TILE, 2, STATE_TILE)
    return st[:, :, 0].reshape(b, g, p), st[:, :, 1].reshape(b, g, p)


def _s5_scan_body(u_ref, wx_ref, wc_ref, ar_ref, ai_ref, d_ref, h0_ref, y_ref, hout_ref, us_ref, ys_ref, xs_ref,
                  h_ref, *, bsz, steps, batch_on_lanes):
    tiles = wx_ref.shape[0]
    ch = tiles * LANES

    @pl.when(pl.program_id(0) == 0)
    def _():
        h_ref[...] = h0_ref[...]

    for j in range(tiles):
        if batch_on_lanes:
            for b in range(bsz):
                us_ref[j, pl.ds(b, steps, stride=bsz), :] = u_ref[:, ch * b + LANES * j:ch * b + LANES * (j + 1)]
        else:
            us_ref[j] = u_ref[:, LANES * j:LANES * (j + 1)]

    for j in range(tiles):
        xs_ref[:, 2 * STATE_TILE * j:2 * STATE_TILE * (j + 1)] = jnp.dot(
            us_ref[j].astype(BF16), wx_ref[j], preferred_element_type=F32)

    for j in range(tiles):
        re = slice(2 * STATE_TILE * j, 2 * STATE_TILE * j + STATE_TILE)
        im = slice(2 * STATE_TILE * j + STATE_TILE, 2 * STATE_TILE * (j + 1))
        a_r = jnp.broadcast_to(ar_ref[:, STATE_TILE * j:STATE_TILE * (j + 1)], (bsz, STATE_TILE))
        a_i = jnp.broadcast_to(ai_ref[:, STATE_TILE * j:STATE_TILE * (j + 1)], (bsz, STATE_TILE))

        def step(t, carry, re=re, im=im, a_r=a_r, a_i=a_i):
            hr, hi = carry
            r = pl.ds(pl.multiple_of(t * bsz, bsz), bsz)
            nhr = a_r * hr - a_i * hi + xs_ref[r, re]
            nhi = a_r * hi + a_i * hr + xs_ref[r, im]
            xs_ref[r, re] = nhr
            xs_ref[r, im] = nhi
            return nhr, nhi

        hr, hi = lax.fori_loop(0, steps, step, (h_ref[:, re], h_ref[:, im]))
        h_ref[:, re] = hr
        h_ref[:, im] = hi

    for j in range(tiles):
        cols = slice(LANES * j, LANES * (j + 1))
        hb = xs_ref[:, 2 * STATE_TILE * j:2 * STATE_TILE * (j + 1)].astype(BF16)
        y = jnp.dot(hb, wc_ref[j], preferred_element_type=F32) + d_ref[:, cols] * us_ref[j]
        if batch_on_lanes:
            ys_ref[j] = y
            for b in range(bsz):
                y_ref[:, ch * b + LANES * j:ch * b + LANES * (j + 1)] = ys_ref[j, pl.ds(b, steps, stride=bsz), :]
        else:
            y_ref[:, cols] = y
    hout_ref[...] = h_ref[...]


def _s5_scan(u, wx, wc, ar, ai, d_skip, h0, bsz, steps, batch_on_lanes):
    tiles = wx.shape[0]
    ch = tiles * LANES
    width = tiles * 2 * STATE_TILE
    rows = steps * bsz
    if batch_on_lanes:
        n_chunks = u.shape[0] // steps
        io_spec = pl.BlockSpec((steps, bsz * ch), lambda i: (i, 0))
    else:
        n_chunks = 1
        assert u.shape == (rows, ch)
        io_spec = pl.BlockSpec((rows, ch), lambda i: (0, 0))
    const = lambda shape: pl.BlockSpec(shape, lambda i: (0,) * len(shape))
    return pl.pallas_call(
        functools.partial(_s5_scan_body, bsz=bsz, steps=steps, batch_on_lanes=batch_on_lanes),
        grid=(n_chunks,),
        in_specs=[io_spec, const(wx.shape), const(wc.shape), const((1, tiles * STATE_TILE)),
                  const((1, tiles * STATE_TILE)), const((1, ch)), const((bsz, width))],
        out_specs=[io_spec, const((bsz, width))],
        out_shape=[jax.ShapeDtypeStruct(u.shape, F32), jax.ShapeDtypeStruct((bsz, width), F32)],
        scratch_shapes=[pltpu.VMEM((tiles, rows, LANES), F32), pltpu.VMEM((tiles, rows, LANES), F32),
                        pltpu.VMEM((rows, width), F32), pltpu.VMEM((bsz, width), F32)],
        compiler_params=_compiler_params(("arbitrary",)),
        name="s5_scan",
    )(u, wx, wc, ar.reshape(1, -1), ai.reshape(1, -1), d_skip.reshape(1, ch), h0)


def _post_mix_body(attn_ref, y_ref, x_ref, ga_ref, gs_ref, wglu_ref, bglu_ref, wo_ref, o_ref):
    aw = attn_ref.shape[1]
    an = _rms(attn_ref[...], ga_ref[...])
    z = jax.nn.gelu(y_ref[...])
    gate = jax.nn.sigmoid(jnp.dot(z.astype(BF16), wglu_ref[...], preferred_element_type=F32) + bglu_ref[...])
    sn = _rms(z * gate, gs_ref[...])
    o_ref[...] = (x_ref[...]
                  + jnp.dot(an.astype(BF16), wo_ref[:aw, :], preferred_element_type=F32)
                  + jnp.dot(sn.astype(BF16), wo_ref[aw:, :], preferred_element_type=F32))


def _post_mix(attn, y, x, g_attn, g_ssm, w_glu, b_glu, w_out, layer, tm, y_tiles_per_seq):
    n, d = x.shape
    aw, sw = attn.shape[1], w_glu.shape[1]
    row = lambda w: pl.BlockSpec((tm, w), lambda i: (i, 0))
    vec = lambda w: pl.BlockSpec((None, 1, w), lambda i: (layer, 0, 0))
    mat = lambda r, c: pl.BlockSpec((None, r, c), lambda i: (layer, 0, 0))
    if y_tiles_per_seq is None:
        y_spec = row(sw)
    else:
        y_spec = pl.BlockSpec((tm, sw), lambda i: (i % y_tiles_per_seq, i // y_tiles_per_seq))
    return pl.pallas_call(
        _post_mix_body,
        grid=(n // tm,),
        in_specs=[row(aw), y_spec, row(d), vec(aw), vec(sw), mat(sw, sw), vec(sw), mat(aw + sw, d)],
        out_specs=row(d),
        out_shape=jax.ShapeDtypeStruct((n, d), F32),
        compiler_params=_compiler_params(("parallel",)),
        name="post_mix",
    )(attn, y, x, g_attn, g_ssm, w_glu, b_glu, w_out)


def _finish(x_ref, acc_ref, gf_ref, o_ref):
    y = x_ref[...] + acc_ref[...]
    o_ref[...] = y if gf_ref is None else _rms(y, gf_ref[...])


def _ffn_body(x_ref, g_ref, wg_ref, wu_ref, wd_ref, *rest, final):
    gf_ref = rest[0] if final else None
    o_ref, hn_ref, acc_ref = rest[-3:]
    f = pl.program_id(1)

    @pl.when(f == 0)
    def _():
        hn_ref[...] = _rms(x_ref[...], g_ref[...]).astype(BF16)
        acc_ref[...] = jnp.zeros_like(acc_ref)

    hn = hn_ref[...]
    gate = jnp.dot(hn, wg_ref[...], preferred_element_type=F32)
    up = jnp.dot(hn, wu_ref[...], preferred_element_type=F32)
    act = (jax.nn.silu(gate) * up).astype(BF16)
    acc_ref[...] += jnp.dot(act, wd_ref[...], preferred_element_type=F32)

    @pl.when(f == pl.num_programs(1) - 1)
    def _():
        _finish(x_ref, acc_ref, gf_ref, o_ref)


def _ffn(x, g, wg, wu, wd, layer, g_final, tm, tf):
    n, d = x.shape
    ff = wg.shape[2]
    final = g_final is not None
    blk = layer // 2
    in_specs = [pl.BlockSpec((tm, d), lambda i, f: (i, 0)),
                pl.BlockSpec((None, 1, d), lambda i, f: (layer, 0, 0)),
                pl.BlockSpec((None, d, tf), lambda i, f: (blk, 0, f)),
                pl.BlockSpec((None, d, tf), lambda i, f: (blk, 0, f)),
                pl.BlockSpec((None, tf, d), lambda i, f: (blk, f, 0))]
    args = [x, g, wg, wu, wd]
    if final:
        in_specs.append(pl.BlockSpec((1, d), lambda i, f: (0, 0)))
        args.append(g_final.reshape(1, d))
    return pl.pallas_call(
        functools.partial(_ffn_body, final=final),
        grid=(n // tm, ff // tf),
        in_specs=in_specs,
        out_specs=pl.BlockSpec((tm, d), lambda i, f: (i, 0)),
        out_shape=jax.ShapeDtypeStruct((n, d), F32),
        scratch_shapes=[pltpu.VMEM((tm, d), BF16), pltpu.VMEM((tm, d), F32)],
        compiler_params=_compiler_params(("parallel", "arbitrary")),
        name="ffn_dense",
    )(*args)


def _route(hn, wr_ref, br_ref, n_experts, lane):
    logits = jnp.dot(hn, wr_ref[...], preferred_element_type=F32, precision=lax.Precision.HIGHEST) + br_ref[...]
    logits = jnp.where(lane < n_experts, logits, -jnp.inf)
    m1 = jnp.max(logits, axis=-1, keepdims=True)
    i1 = jnp.min(jnp.where(logits == m1, lane, float(LANES)), axis=-1, keepdims=True)
    rest_logits = jnp.where(lane == i1, -jnp.inf, logits)
    m2 = jnp.max(rest_logits, axis=-1, keepdims=True)
    i2 = jnp.min(jnp.where(rest_logits == m2, lane, float(LANES)), axis=-1, keepdims=True)
    e2 = jnp.exp(m2 - m1)
    return i1, i2, 1.0 / (1.0 + e2), e2 / (1.0 + e2)


def _moe_body(x_ref, g_ref, wr_ref, br_ref, wg_ref, wu_ref, wd_ref, *rest, n_experts, final):
    gf_ref = rest[0] if final else None
    o_ref, hn_ref, comb_ref, acc_ref = rest[-4:]
    e, f = pl.program_id(1), pl.program_id(2)
    lane = lax.broadcasted_iota(jnp.int32, comb_ref.shape, 1).astype(F32)

    @pl.when((e == 0) & (f == 0))
    def _():
        hn = _rms(x_ref[...], g_ref[...])
        hn_ref[...] = hn.astype(BF16)
        acc_ref[...] = jnp.zeros_like(acc_ref)
        i1, i2, g1, g2 = _route(hn, wr_ref, br_ref, n_experts, lane)
        comb_ref[...] = jnp.where(lane == i1, g1, 0.0) + jnp.where(lane == i2, g2, 0.0)

    hn = hn_ref[...]
    gate = jnp.dot(hn, wg_ref[...], preferred_element_type=F32)
    up = jnp.dot(hn, wu_ref[...], preferred_element_type=F32)
    act = (jax.nn.silu(gate) * up).astype(BF16)
    weight = jnp.sum(jnp.where(lane == e.astype(F32), comb_ref[...], 0.0), axis=-1, keepdims=True)
    acc_ref[...] += weight * jnp.dot(act, wd_ref[...], preferred_element_type=F32)

    @pl.when((e == n_experts - 1) & (f == pl.num_programs(2) - 1))
    def _():
        _finish(x_ref, acc_ref, gf_ref, o_ref)


def _moe(x, g, w_router, b_router, wg, wu, wd, layer, g_final, tm, tf):
    n, d = x.shape
    _, n_experts, _, ff = wg.shape
    final = g_final is not None
    blk = layer // 2
    wr = jnp.zeros((d, LANES), F32).at[:, :n_experts].set(w_router.astype(F32))
    br = jnp.zeros((1, LANES), F32).at[0, :n_experts].set(b_router.astype(F32))
    in_specs = [pl.BlockSpec((tm, d), lambda i, e, f: (i, 0)),
                pl.BlockSpec((None, 1, d), lambda i, e, f: (layer, 0, 0)),
                pl.BlockSpec((d, LANES), lambda i, e, f: (0, 0)),
                pl.BlockSpec((1, LANES), lambda i, e, f: (0, 0)),
                pl.BlockSpec((None, None, d, tf), lambda i, e, f: (blk, e, 0, f)),
                pl.BlockSpec((None, None, d, tf), lambda i, e, f: (blk, e, 0, f)),
                pl.BlockSpec((None, None, tf, d), lambda i, e, f: (blk, e, f, 0))]
    args = [x, g, wr, br, wg, wu, wd]
    if final:
        in_specs.append(pl.BlockSpec((1, d), lambda i, e, f: (0, 0)))
        args.append(g_final.reshape(1, d))
    return pl.pallas_call(
        functools.partial(_moe_body, n_experts=n_experts, final=final),
        grid=(n // tm, n_experts, ff // tf),
        in_specs=in_specs,
        out_specs=pl.BlockSpec((tm, d), lambda i, e, f: (i, 0)),
        out_shape=jax.ShapeDtypeStruct((n, d), F32),
        scratch_shapes=[pltpu.VMEM((tm, d), BF16), pltpu.VMEM((tm, LANES), F32), pltpu.VMEM((tm, d), F32)],
        compiler_params=_compiler_params(("parallel", "arbitrary", "arbitrary")),
        name="moe",
    )(*args)


EXPERT_ROW_TILE = 512
MOE_DMA_ROWS = 256


def _router_body(x_ref, g_ref, wr_ref, br_ref, hn_ref, route_ref, *, n_experts):
    lane = lax.broadcasted_iota(jnp.int32, route_ref.shape, 1).astype(F32)
    hn = _rms(x_ref[...], g_ref[...])
    hn_ref[...] = hn
    i1, i2, g1, g2 = _route(hn, wr_ref, br_ref, n_experts, lane)
    route_ref[...] = jnp.where(lane == 0.0, i1, jnp.where(lane == 1.0, i2, jnp.where(
        lane == 2.0, g1, jnp.where(lane == 3.0, g2, 0.0))))


def _router(x, g, wr, br, layer, n_experts, tm):
    n, d = x.shape
    return pl.pallas_call(
        functools.partial(_router_body, n_experts=n_experts),
        grid=(n // tm,),
        in_specs=[pl.BlockSpec((tm, d), lambda i: (i, 0)),
                  pl.BlockSpec((None, 1, d), lambda i: (layer, 0, 0)),
                  pl.BlockSpec((d, LANES), lambda i: (0, 0)),
                  pl.BlockSpec((1, LANES), lambda i: (0, 0))],
        out_specs=[pl.BlockSpec((tm, d), lambda i: (i, 0)), pl.BlockSpec((tm, LANES), lambda i: (i, 0))],
        out_shape=[jax.ShapeDtypeStruct((n, d), F32), jax.ShapeDtypeStruct((n, LANES), F32)],
        compiler_params=_compiler_params(("parallel",)),
        name="moe_router",
    )(x, g, wr, br)


def _dispatch_plan(experts, n_experts, n_tiles):
    flat = experts.reshape(-1)
    onehot = (flat[:, None] == jnp.arange(n_experts, dtype=jnp.int32)[None, :]).astype(jnp.int32)
    running = jnp.cumsum(onehot, axis=0)
    rank = jnp.sum(onehot * running, axis=1) - 1
    tiles_per_expert = (running[-1] + EXPERT_ROW_TILE - 1) // EXPERT_ROW_TILE
    tile_end = jnp.cumsum(tiles_per_expert)
    group_start = (tile_end - tiles_per_expert) * EXPERT_ROW_TILE
    slot = jnp.sum(onehot * group_start[None, :], axis=1) + rank
    tile = jnp.arange(n_tiles, dtype=jnp.int32)
    tile_expert = jnp.minimum(jnp.sum((tile[:, None] >= tile_end[None, :]).astype(jnp.int32), axis=1),
                              n_experts - 1)
    tile_valid = (tile < tile_end[-1]).astype(jnp.int32)
    return slot.astype(jnp.int32), tile_expert.astype(jnp.int32), tile_valid


def _row_copy(src_ref, src_row, dst_ref, dst_row, sem):
    return pltpu.make_async_copy(src_ref.at[pl.ds(src_row, 1)], dst_ref.at[pl.ds(dst_row, 1)], sem)


def _scatter_rows_body(slot_ref, hn_ref, _, xs_ref, sem):
    base = pl.program_id(0) * MOE_DMA_ROWS

    def copies(j):
        t = base + j
        return [_row_copy(hn_ref, t, xs_ref, slot_ref[2 * t + k], sem) for k in range(2)]

    def start(j, c):
        for cp in copies(j):
            cp.start()
        return c

    def wait(j, c):
        for cp in copies(j):
            cp.wait()
        return c

    lax.fori_loop(0, MOE_DMA_ROWS, start, 0)
    lax.fori_loop(0, MOE_DMA_ROWS, wait, 0)


def _scatter_rows(slot, hn, n_slots):
    n, d = hn.shape
    grouped = jnp.zeros((n_slots, d), hn.dtype)
    return pl.pallas_call(
        _scatter_rows_body,
        grid_spec=pltpu.PrefetchScalarGridSpec(
            num_scalar_prefetch=1,
            grid=(n // MOE_DMA_ROWS,),
            in_specs=[pl.BlockSpec(memory_space=pl.ANY), pl.BlockSpec(memory_space=pl.ANY)],
            out_specs=pl.BlockSpec(memory_space=pl.ANY),
            scratch_shapes=[pltpu.SemaphoreType.DMA(())]),
        out_shape=jax.ShapeDtypeStruct((n_slots, d), hn.dtype),
        input_output_aliases={2: 0},
        compiler_params=pltpu.CompilerParams(dimension_semantics=("arbitrary",), disable_bounds_checks=True),
        name="moe_scatter",
    )(slot, hn, grouped)


def _experts_body(te_ref, tv_ref, x_ref, wg_ref, wu_ref, wd_ref, y_ref, xb_ref, acc_ref):
    t, f = pl.program_id(0), pl.program_id(1)

    @pl.when(f == 0)
    def _():
        xb_ref[...] = x_ref[...].astype(BF16)
        acc_ref[...] = jnp.zeros_like(acc_ref)

    @pl.when(tv_ref[t] != 0)
    def _():
        xb = xb_ref[...]
        gate = jnp.dot(xb, wg_ref[...], preferred_element_type=F32)
        up = jnp.dot(xb, wu_ref[...], preferred_element_type=F32)
        act = (jax.nn.silu(gate) * up).astype(BF16)
        acc_ref[...] += jnp.dot(act, wd_ref[...], preferred_element_type=F32)

    @pl.when(f == pl.num_programs(1) - 1)
    def _():
        y_ref[...] = acc_ref[...]


def _experts(xs, tile_expert, tile_valid, wg, wu, wd, layer, tf):
    n_slots, d = xs.shape
    ff = wg.shape[3]
    blk = layer // 2
    n_f = ff // tf
    chunk = lambda f, tv, t: jnp.where(tv[t] != 0, f, n_f - 1)
    return pl.pallas_call(
        _experts_body,
        grid_spec=pltpu.PrefetchScalarGridSpec(
            num_scalar_prefetch=2,
            grid=(n_slots // EXPERT_ROW_TILE, n_f),
            in_specs=[pl.BlockSpec((EXPERT_ROW_TILE, d), lambda t, f, te, tv: (t, 0)),
                      pl.BlockSpec((None, None, d, tf), lambda t, f, te, tv: (blk, te[t], 0, chunk(f, tv, t))),
                      pl.BlockSpec((None, None, d, tf), lambda t, f, te, tv: (blk, te[t], 0, chunk(f, tv, t))),
                      pl.BlockSpec((None, None, tf, d), lambda t, f, te, tv: (blk, te[t], chunk(f, tv, t), 0))],
            out_specs=pl.BlockSpec((EXPERT_ROW_TILE, d), lambda t, f, te, tv: (t, 0)),
            scratch_shapes=[pltpu.VMEM((EXPERT_ROW_TILE, d), BF16), pltpu.VMEM((EXPERT_ROW_TILE, d), F32)]),
        out_shape=jax.ShapeDtypeStruct((n_slots, d), F32),
        compiler_params=_compiler_params(("arbitrary", "arbitrary")),
        name="moe_experts",
    )(tile_expert, tile_valid, xs, wg, wu, wd)


def _combine_body(slot_ref, x_ref, route_ref, y_ref, *rest, final):
    gf_ref = rest[0] if final else None
    o_ref, ya_ref, yb_ref, sem = rest[-4:]
    base = pl.program_id(0) * MOE_DMA_ROWS

    def copies(j):
        t = base + j
        return [_row_copy(y_ref, slot_ref[2 * t], ya_ref, j, sem.at[0]),
                _row_copy(y_ref, slot_ref[2 * t + 1], yb_ref, j, sem.at[1])]

    def start(j, c):
        for cp in copies(j):
            cp.start()
        return c

    def wait(j, c):
        for cp in copies(j):
            cp.wait()
        return c

    lax.fori_loop(0, MOE_DMA_ROWS, start, 0)
    lax.fori_loop(0, MOE_DMA_ROWS, wait, 0)
    y = x_ref[...] + route_ref[:, 2:3] * ya_ref[...] + route_ref[:, 3:4] * yb_ref[...]
    o_ref[...] = y if gf_ref is None else _rms(y, gf_ref[...])


def _combine(slot, x, route, y, g_final):
    n, d = x.shape
    final = g_final is not None
    in_specs = [pl.BlockSpec((MOE_DMA_ROWS, d), lambda i, s: (i, 0)),
                pl.BlockSpec((MOE_DMA_ROWS, LANES), lambda i, s: (i, 0)),
                pl.BlockSpec(memory_space=pl.ANY)]
    args = [slot, x, route, y]
    if final:
        in_specs.append(pl.BlockSpec((1, d), lambda i, s: (0, 0)))
        args.append(g_final.reshape(1, d))
    return pl.pallas_call(
        functools.partial(_combine_body, final=final),
        grid_spec=pltpu.PrefetchScalarGridSpec(
            num_scalar_prefetch=1,
            grid=(n // MOE_DMA_ROWS,),
            in_specs=in_specs,
            out_specs=pl.BlockSpec((MOE_DMA_ROWS, d), lambda i, s: (i, 0)),
            scratch_shapes=[pltpu.VMEM((MOE_DMA_ROWS, d), F32), pltpu.VMEM((MOE_DMA_ROWS, d), F32),
                            pltpu.SemaphoreType.DMA((2,))]),
        out_shape=jax.ShapeDtypeStruct((n, d), F32),
        compiler_params=pltpu.CompilerParams(dimension_semantics=("arbitrary",), disable_bounds_checks=True,
                                             vmem_limit_bytes=VMEM_LIMIT_BYTES),
        name="moe_combine",
    )(*args)


def _moe_sorted(x, g, w_router, b_router, wg, wu, wd, layer, g_final, tm, tf):
    n, d = x.shape
    n_experts = wg.shape[1]
    wr = jnp.zeros((d, LANES), F32).at[:, :n_experts].set(w_router.astype(F32))
    br = jnp.zeros((1, LANES), F32).at[0, :n_experts].set(b_router.astype(F32))
    hn, route = _router(x, g, wr, br, layer, n_experts, tm)
    n_tiles = 2 * n // EXPERT_ROW_TILE + n_experts
    slot, tile_expert, tile_valid = _dispatch_plan(route[:, :2].astype(jnp.int32), n_experts, n_tiles)
    xs = _scatter_rows(slot, hn, n_tiles * EXPERT_ROW_TILE)
    y = _experts(xs, tile_expert, tile_valid, wg, wu, wd, layer, tf)
    return _combine(slot, x, route, y, g_final)


PROMPT_ROW_TILE = 512
PROMPT_SCAN_STEPS = 64


def kernel(x_prompt, x_sample, cache_win_k, cache_win_v, state_ssm_re, state_ssm_im, rel_bias, norm_mix, w_in, ssm_a_re, ssm_a_im, ssm_log_dt, ssm_b_re, ssm_b_im, ssm_c_re, ssm_c_im, ssm_d, ssm_w_glu, ssm_b_glu, norm_attn_out, norm_ssm_out, w_out, norm_ffn, ffn_w_gate, ffn_w_up, ffn_w_down, moe_w_router, moe_b_router, moe_w_gate, moe_w_up, moe_w_down, norm_final):
    batch, seq, d_model = x_prompt.shape
    dec_batch, dec_seq, _ = x_sample.shape
    depth = w_in.shape[0]
    win, n_heads = cache_win_k.shape[2:4]
    attn_width = n_heads * HEAD_DIM
    n_groups, n_state = ssm_a_re.shape[1:]
    ssm_width = n_groups * SSM_GROUP
    assert w_in.shape[2] == 3 * attn_width + ssm_width and seq == MAX_WINDOW
    dec_rows = dec_batch * dec_seq

    band_bias = _band_bias(rel_bias)
    sample_tabs = _sample_tables(rel_bias, win, dec_seq)
    ar, ai, bb_re, bb_im = _s5_params(ssm_a_re, ssm_a_im, ssm_log_dt, ssm_b_re, ssm_b_im)
    cache_k = jnp.transpose(cache_win_k, (0, 1, 3, 4, 2))
    cache_v = jnp.transpose(cache_win_v, (0, 1, 3, 4, 2))

    rows3 = lambda a: a.reshape(a.shape[0], 1, a.shape[1])
    norm_mix3, norm_ffn3, b_glu3 = rows3(norm_mix), rows3(norm_ffn), rows3(ssm_b_glu)
    g_attn3, g_ssm3 = rows3(norm_attn_out), rows3(norm_ssm_out)
    w_in_b, w_out_b, w_glu_b = w_in.astype(BF16), w_out.astype(BF16), ssm_w_glu.astype(BF16)
    ffn_b = [w.astype(BF16) for w in (ffn_w_gate, ffn_w_up, ffn_w_down)]
    moe_b = [w.astype(BF16) for w in (moe_w_gate, moe_w_up, moe_w_down)]
    tf = ffn_w_gate.shape[2] // 2

    def mix_and_ffn(x, l, attn, y, tm, y_tiles_per_seq):
        g_final = norm_final if l == depth - 1 else None
        x = _post_mix(attn, y, x, g_attn3, g_ssm3, w_glu_b, b_glu3, w_out_b, l, tm, y_tiles_per_seq)
        if l % 2 == 0:
            return _ffn(x, norm_ffn3, *ffn_b, l, g_final, tm, tf)
        n_experts = moe_w_gate.shape[1]
        moe = _moe_sorted if 2 * x.shape[0] >= n_experts * EXPERT_ROW_TILE else _moe
        return moe(x, norm_ffn3, moe_w_router[l // 2], moe_b_router[l // 2], *moe_b, l, g_final, tm, tf)

    xp = x_prompt.reshape(batch * seq, d_model)
    xs = x_sample.reshape(dec_rows, d_model)
    zero_state = jnp.zeros((batch, 2 * n_groups * n_state), F32)
    outs = [[] for _ in range(8)]
    for l in range(depth):
        wx, wc = _pack_s5_weights(bb_re[l], bb_im[l], ssm_c_re[l].astype(F32), ssm_c_im[l].astype(F32))

        qkv, k_t, v_t, u = _norm_proj_prompt(xp, norm_mix3, w_in_b, l, batch, seq, attn_width, PROMPT_ROW_TILE)
        attn = _prompt_attn(qkv, band_bias, batch, seq, attn_width)
        y, h_fin = _s5_scan(u, wx, wc, ar[l], ai[l], ssm_d[l], zero_state, batch, PROMPT_SCAN_STEPS, True)
        xp = mix_and_ffn(xp, l, attn, y, PROMPT_ROW_TILE, seq // PROMPT_ROW_TILE)
        outs[0].append(k_t)
        outs[1].append(v_t)
        h_re, h_im = _unpack_state(h_fin, n_groups, n_state)
        outs[2].append(h_re)
        outs[3].append(h_im)

        proj = _norm_proj(xs, norm_mix3, w_in_b, l, dec_rows)
        q, k_new, v_new = (proj[:, i * attn_width:(i + 1) * attn_width].reshape(dec_batch, dec_seq, n_heads, HEAD_DIM)
                           for i in range(3))
        attn = _sample_attn(q, k_new, v_new, cache_k, cache_v, l, sample_tabs).reshape(dec_rows, attn_width)
        u = proj[:, 3 * attn_width:].reshape(dec_batch, dec_seq, ssm_width).transpose(1, 0, 2)
        y, h_fin = _s5_scan(u.reshape(dec_rows, ssm_width), wx, wc, ar[l], ai[l], ssm_d[l],
                            _pack_state(state_ssm_re[l], state_ssm_im[l]), dec_batch, dec_seq, False)
        y = y.reshape(dec_seq, dec_batch, ssm_width).transpose(1, 0, 2).reshape(dec_rows, ssm_width)
        xs = mix_and_ffn(xs, l, attn, y, dec_rows, None)
        outs[4].append(k_new)
        outs[5].append(v_new)
        h_re, h_im = _unpack_state(h_fin, n_groups, n_state)
        outs[6].append(h_re)
        outs[7].append(h_im)

    def window_out(parts):
        a = jnp.stack(parts, 0).reshape(depth, batch, n_heads, HEAD_DIM, seq)
        return jnp.transpose(a, (0, 1, 4, 2, 3))

    return (xp.reshape(batch, seq, d_model), xs.reshape(dec_batch, dec_seq, d_model),
            window_out(outs[0]), window_out(outs[1]), *[jnp.stack(o, 0) for o in outs[2:]])
```

```python
import functools

import numpy as np
import jax
import jax.numpy as jnp
from jax import lax
from jax.experimental import pallas as pl
from jax.experimental.pallas import tpu as pltpu

F32 = jnp.float32
BF16 = jnp.bfloat16

HEAD_DIM = 64
DILATION_PATTERNS = ((128, 1), (512, 4), (2048, 16))
MAX_WINDOW = 2048
N_BUCKETS = 32
SSM_GROUP = 16
SSM_STATE = 64
RMS_EPS = 1e-6
NEG_BIG = -1e30

LANES = 128
Q_TILE = 128
ATTN_UNROLL = 4
GROUPS_PER_TILE = LANES // SSM_GROUP
STATE_TILE = GROUPS_PER_TILE * SSM_STATE
VMEM_LIMIT_BYTES = 56 * 1024 * 1024


def _compiler_params(semantics):
    return pltpu.CompilerParams(dimension_semantics=semantics, vmem_limit_bytes=VMEM_LIMIT_BYTES)


def _rms(x, g):
    return x * lax.rsqrt(jnp.mean(x * x, axis=-1, keepdims=True) + RMS_EPS) * g


def _t5_causal_bucket(dist):
    max_exact = N_BUCKETS // 2
    d = np.asarray(dist, dtype=np.int64)
    large = max_exact + (np.log(np.maximum(d, max_exact) / max_exact)
                         / np.log(MAX_WINDOW / max_exact)
                         * (N_BUCKETS - max_exact)).astype(np.int64)
    large = np.minimum(large, N_BUCKETS - 1)
    return np.where(d < max_exact, d, large).astype(np.int32)


def _multiplicity(d):
    m = np.zeros(d.shape, np.int32)
    for window, dil in DILATION_PATTERNS:
        m += ((d >= 0) & (d <= window) & (d % dil == 0)).astype(np.int32)
    return m


def _norm_proj_body(x_ref, g_ref, w_ref, o_ref):
    hn = _rms(x_ref[...], g_ref[...])
    o_ref[...] = jnp.dot(hn.astype(BF16), w_ref[...], preferred_element_type=F32)


def _norm_proj(x, g, w, layer, tm):
    n, d = x.shape
    c = w.shape[2]
    return pl.pallas_call(
        _norm_proj_body,
        grid=(n // tm,),
        in_specs=[pl.BlockSpec((tm, d), lambda i: (i, 0)),
                  pl.BlockSpec((None, 1, d), lambda i: (layer, 0, 0)),
                  pl.BlockSpec((None, d, c), lambda i: (layer, 0, 0))],
        out_specs=pl.BlockSpec((tm, c), lambda i: (i, 0)),
        out_shape=jax.ShapeDtypeStruct((n, c), F32),
        compiler_params=_compiler_params(("parallel",)),
        name="norm_proj",
    )(x, g, w)


def _norm_proj_prompt_body(x_ref, g_ref, w_ref, qkv_ref, kt_ref, vt_ref, u_ref, *, attn_width):
    hn = _rms(x_ref[...], g_ref[...])
    proj = jnp.dot(hn.astype(BF16), w_ref[...], preferred_element_type=F32)
    qkv_ref[...] = proj[:, :3 * attn_width]
    kt_ref[...] = proj[:, attn_width:2 * attn_width].T
    vt_ref[...] = proj[:, 2 * attn_width:3 * attn_width].T
    u_ref[...] = proj[:, 3 * attn_width:]


def _norm_proj_prompt(x, g, w, layer, batch, seq, attn_width, tm):
    n, d = x.shape
    c = w.shape[2]
    ch = c - 3 * attn_width
    per_seq = seq // tm
    return pl.pallas_call(
        functools.partial(_norm_proj_prompt_body, attn_width=attn_width),
        grid=(n // tm,),
        in_specs=[pl.BlockSpec((tm, d), lambda i: (i, 0)),
                  pl.BlockSpec((None, 1, d), lambda i: (layer, 0, 0)),
                  pl.BlockSpec((None, d, c), lambda i: (layer, 0, 0))],
        out_specs=[pl.BlockSpec((tm, 3 * attn_width), lambda i: (i, 0)),
                   pl.BlockSpec((None, attn_width, tm), lambda i: (i // per_seq, 0, i % per_seq)),
                   pl.BlockSpec((None, attn_width, tm), lambda i: (i // per_seq, 0, i % per_seq)),
                   pl.BlockSpec((tm, ch), lambda i: (i % per_seq, i // per_seq))],
        out_shape=[jax.ShapeDtypeStruct((n, 3 * attn_width), F32),
                   jax.ShapeDtypeStruct((batch, attn_width, seq), F32),
                   jax.ShapeDtypeStruct((batch, attn_width, seq), F32),
                   jax.ShapeDtypeStruct((seq, batch * ch), F32)],
        compiler_params=_compiler_params(("parallel",)),
        name="norm_proj_prompt",
    )(x, g, w)


def _band_bias(rel_bias):
    period = 3 * Q_TILE
    n_heads = rel_bias.shape[1]
    rows0 = []
    for window, dil in DILATION_PATTERNS:
        assert window // dil == Q_TILE
        vec = rel_bias[_t5_causal_bucket((Q_TILE - np.arange(Q_TILE + 1)) * dil)].astype(F32)
        rows0.append(jnp.concatenate([vec.T, jnp.full((n_heads, period - Q_TILE - 1), NEG_BIG, F32)], axis=1))
    row0 = jnp.stack(rows0, 0)
    flat = jnp.tile(row0, (1, 1, Q_TILE))[:, :, :Q_TILE * (period - 1)]
    return flat.reshape(len(DILATION_PATTERNS), n_heads, Q_TILE, period - 1)[:, :, :, :2 * Q_TILE]


def _prompt_attn_body(q_ref, k_ref, v_ref, bias_ref, o_ref, op_ref, lse_ref):
    seq = q_ref.shape[0]
    lane = lax.broadcasted_iota(jnp.int32, (Q_TILE, LANES), 1)
    head0 = lane < HEAD_DIM
    scale = HEAD_DIM ** -0.5

    def rows(start, size, dil):
        return pl.ds(start, size) if dil == 1 else pl.ds(start, size, stride=dil)

    def tiles(p, dil, starts, first):
        nk = Q_TILE if first else 2 * Q_TILE
        work = []
        for q_start, k_start in starts:
            qsl, ksl = rows(q_start, Q_TILE, dil), rows(k_start, nk, dil)
            qt = q_ref[qsl, :] * scale
            kt = k_ref[ksl, :].astype(BF16)
            vt = v_ref[ksl, :].astype(BF16)
            for h in range(2):
                qh = jnp.where(head0 if h == 0 else jnp.logical_not(head0), qt, 0.0).astype(BF16)
                work.append([qsl, h, vt, qh, kt])
        for w in work:
            s = lax.dot_general(w[3], w[4], (((1,), (1,)), ((), ())), preferred_element_type=F32)
            w[3] = s + (bias_ref[p, w[1], :, Q_TILE:] if first else bias_ref[p, w[1]])
        for w in work:
            w[4] = jnp.max(w[3], axis=-1, keepdims=True)
        for w in work:
            w[3] = jnp.exp(w[3] - w[4])
        for w in work:
            l = jnp.sum(w[3], axis=-1, keepdims=True)
            o = jnp.dot(w[3].astype(BF16), w[2], preferred_element_type=F32)
            w[3] = o / l
            w[4] = w[4] + jnp.log(l)
        for w0, w1 in zip(work[0::2], work[1::2]):
            op_ref[p, w0[0], :] = jnp.where(head0, w0[3], w1[3])
            lse_ref[p, w0[0], :] = jnp.where(head0, w0[4], w1[4])

    for p, (_, dil) in enumerate(DILATION_PATTERNS):
        n_tiles = seq // dil // Q_TILE
        stride = dil * Q_TILE
        unroll = ATTN_UNROLL
        if dil < unroll:
            unroll = max(u for u in range(1, ATTN_UNROLL + 1) if (n_tiles - 1) % u == 0)
            for r in range(dil):
                tiles(p, dil, [(r, r)], True)

                def blk(t, c, p=p, dil=dil, r=r, stride=stride, unroll=unroll):
                    first_t = 1 + t * unroll
                    tiles(p, dil, [(r + stride * (first_t + j), r + stride * (first_t + j - 1))
                                   for j in range(unroll)], False)
                    return c
                lax.fori_loop(0, (n_tiles - 1) // unroll, blk, 0)
        else:
            assert dil % unroll == 0

            def classes(g, c, p=p, dil=dil, n_tiles=n_tiles, stride=stride, unroll=unroll):
                rs = [g * unroll + j for j in range(unroll)]
                tiles(p, dil, [(r, r) for r in rs], True)
                if n_tiles > 1:
                    def blk(t, c2):
                        tiles(p, dil, [(r + stride * t, r + stride * (t - 1)) for r in rs], False)
                        return c2
                    lax.fori_loop(1, n_tiles, blk, 0)
                return c
            lax.fori_loop(0, dil // unroll, classes, 0)

    def combine(t, carry):
        r = pl.ds(pl.multiple_of(t * Q_TILE, Q_TILE), Q_TILE)
        lse = [lse_ref[p, r, :] for p in range(len(DILATION_PATTERNS))]
        mx = functools.reduce(jnp.maximum, lse)
        ws = [jnp.exp(x - mx) for x in lse]
        num = sum(w * op_ref[p, r, :] for p, w in enumerate(ws))
        o_ref[r, :] = num / sum(ws)
        return carry

    lax.fori_loop(0, seq // Q_TILE, combine, 0)


def _prompt_attn(proj, bias, batch, seq, attn_width):
    n_pairs = attn_width // LANES
    n_pat = len(DILATION_PATTERNS)
    assert seq % (Q_TILE * max(d for _, d in DILATION_PATTERNS)) == 0
    return pl.pallas_call(
        _prompt_attn_body,
        grid=(batch, n_pairs),
        in_specs=[pl.BlockSpec((seq, LANES), lambda b, h: (b, h)),
                  pl.BlockSpec((seq, LANES), lambda b, h: (b, n_pairs + h)),
                  pl.BlockSpec((seq, LANES), lambda b, h: (b, 2 * n_pairs + h)),
                  pl.BlockSpec((n_pat, 2, Q_TILE, 2 * Q_TILE), lambda b, h: (0, h, 0, 0))],
        out_specs=pl.BlockSpec((seq, LANES), lambda b, h: (b, h)),
        out_shape=jax.ShapeDtypeStruct((batch * seq, attn_width), F32),
        scratch_shapes=[pltpu.VMEM((n_pat, seq, LANES), F32),
                        pltpu.VMEM((n_pat, seq, LANES), F32)],
        compiler_params=_compiler_params(("parallel", "parallel")),
        name="prompt_attn",
    )(proj, proj, proj, bias)


def _sample_tables(rel_bias, win, n_new):
    n_heads = rel_bias.shape[1]
    vec_t = rel_bias[_t5_causal_bucket(np.arange(win + n_new))].astype(F32).T
    flipped = vec_t[:, ::-1]
    t = np.arange(n_new)
    d_cache = win + t[:, None] - np.arange(win)[None, :]
    d_new = t[:, None] - t[None, :]
    b_cache = jnp.stack([flipped[:, n_new - 1 - i:n_new - 1 - i + win] for i in range(n_new)], 0)
    b_new = jnp.transpose(vec_t[:, np.maximum(d_new, 0)], (1, 0, 2))
    out = []
    for b, d in ((b_cache, d_cache), (b_new, d_new)):
        mult = _multiplicity(d)
        b = jnp.where((mult > 0)[:, None, :], b, NEG_BIG).reshape(n_new * n_heads, d.shape[1])
        m = np.broadcast_to(np.maximum(mult, 1)[:, None, :], (n_new, n_heads, d.shape[1]))
        out += [b, jnp.asarray(m.reshape(n_new * n_heads, d.shape[1]), F32)]
    return out


def _sample_attn_body(q_ref, kc_ref, kn_ref, vc_ref, vn_ref, bc_ref, mc_ref, bn_ref, mn_ref, hm_ref, o_ref):
    n_new = o_ref.shape[0]
    win = kc_ref.shape[-1]
    qb = (q_ref[...] * HEAD_DIM ** -0.5).astype(BF16)
    s_c = jnp.dot(qb, kc_ref[...].reshape(-1, win).astype(BF16), preferred_element_type=F32) + bc_ref[...]
    s_n = jnp.dot(qb, kn_ref[...].astype(BF16), preferred_element_type=F32) + bn_ref[...]
    m = jnp.maximum(jnp.max(s_c, axis=-1, keepdims=True), jnp.max(s_n, axis=-1, keepdims=True))
    e_c = jnp.exp(s_c - m) * mc_ref[...]
    e_n = jnp.exp(s_n - m) * mn_ref[...]
    den = jnp.sum(e_c, axis=-1, keepdims=True) + jnp.sum(e_n, axis=-1, keepdims=True)
    nt = (((1,), (1,)), ((), ()))
    o = (lax.dot_general(e_c.astype(BF16), vc_ref[...].reshape(-1, win).astype(BF16), nt,
                         preferred_element_type=F32)
         + lax.dot_general(e_n.astype(BF16), vn_ref[...].astype(BF16), nt, preferred_element_type=F32))
    o = o / den * hm_ref[...]
    o_ref[...] = jnp.sum(o.reshape(n_new, -1, o.shape[-1]), axis=1)


def _sample_attn(q, k_new, v_new, cache_k, cache_v, layer, tables):
    depth, batch, n_heads, hd, win = cache_k.shape
    n_new = q.shape[1]
    width = n_heads * hd
    rows = n_new * n_heads
    eye = jnp.eye(n_heads, dtype=F32)
    q_bd = jnp.einsum('bthd,hg->btghd', q, eye).reshape(batch, rows, width)
    kn = jnp.transpose(k_new, (0, 2, 3, 1)).reshape(batch, width, n_new)
    vn = jnp.transpose(v_new, (0, 2, 3, 1)).reshape(batch, width, n_new)
    head_mask = jnp.asarray(np.kron(np.tile(np.eye(n_heads), (n_new, 1)), np.ones((1, hd))), F32)
    cache_spec = pl.BlockSpec((None, None, n_heads, hd, win), lambda b: (layer, b, 0, 0, 0))
    new_spec = pl.BlockSpec((None, width, n_new), lambda b: (b, 0, 0))
    const = lambda a: pl.BlockSpec(a.shape, lambda b: (0, 0))
    return pl.pallas_call(
        _sample_attn_body,
        grid=(batch,),
        in_specs=[pl.BlockSpec((None, rows, width), lambda b: (b, 0, 0)),
                  cache_spec, new_spec, cache_spec, new_spec] + [const(t) for t in tables] + [const(head_mask)],
        out_specs=pl.BlockSpec((None, n_new, width), lambda b: (b, 0, 0)),
        out_shape=jax.ShapeDtypeStruct((batch, n_new, width), F32),
        compiler_params=_compiler_params(("parallel",)),
        name="sample_attn",
    )(q_bd, cache_k, kn, cache_v, vn, *tables, head_mask)


def _s5_param_body(are_ref, aim_ref, ldt_ref, bre_ref, bim_ref, ar_ref, ai_ref, bbre_ref, bbim_ref):
    a_re, a_im = are_ref[...], aim_ref[...]
    dt = jnp.exp(ldt_ref[...])
    mag = jnp.exp(dt * a_re)
    ar, ai = mag * jnp.cos(dt * a_im), mag * jnp.sin(dt * a_im)
    nr, ni = ar - 1.0, ai
    den = a_re * a_re + a_im * a_im
    fr = (nr * a_re + ni * a_im) / den
    fi = (ni * a_re - nr * a_im) / den
    br, bi = bre_ref[...], bim_ref[...]
    ar_ref[...] = ar
    ai_ref[...] = ai
    bbre_ref[...] = fr[:, None, :] * br - fi[:, None, :] * bi
    bbim_ref[...] = fr[:, None, :] * bi + fi[:, None, :] * br


def _s5_params(a_re, a_im, log_dt, b_re, b_im):
    depth, g, p = a_re.shape
    c = b_re.shape[-1]
    n = depth * g
    full = lambda shape: pl.BlockSpec(shape, lambda: (0,) * len(shape))
    ar, ai, bbre, bbim = pl.pallas_call(
        _s5_param_body,
        in_specs=[full((n, p)), full((n, p)), full((n, 1)), full((n, c, p)), full((n, c, p))],
        out_specs=[full((n, p)), full((n, p)), full((n, c, p)), full((n, c, p))],
        out_shape=[jax.ShapeDtypeStruct((n, p), F32), jax.ShapeDtypeStruct((n, p), F32),
                   jax.ShapeDtypeStruct((n, c, p), F32), jax.ShapeDtypeStruct((n, c, p), F32)],
        name="s5_params",
    )(a_re.reshape(n, p), a_im.reshape(n, p), log_dt.reshape(n, 1),
      jnp.swapaxes(b_re, -1, -2).reshape(n, c, p), jnp.swapaxes(b_im, -1, -2).reshape(n, c, p))
    return (ar.reshape(depth, g, p), ai.reshape(depth, g, p),
            bbre.reshape(depth, g, c, p), bbim.reshape(depth, g, c, p))


def _pack_s5_weights(bb_re, bb_im, c_re, c_im):
    g, c, p = bb_re.shape
    tiles = g // GROUPS_PER_TILE
    eye = jnp.eye(GROUPS_PER_TILE, dtype=F32)

    def inp(w):
        w = w.reshape(tiles, GROUPS_PER_TILE, c, p)
        return jnp.einsum('jgcp,gh->jgchp', w, eye).reshape(tiles, LANES, STATE_TILE)

    def outp(w):
        w = w.reshape(tiles, GROUPS_PER_TILE, c, p)
        return jnp.einsum('jgcp,gh->jgphc', w, eye).reshape(tiles, STATE_TILE, LANES)

    wx = jnp.concatenate([inp(bb_re), inp(bb_im)], axis=-1).astype(BF16)
    wc = jnp.concatenate([outp(c_re), -outp(c_im)], axis=1).astype(BF16)
    return wx, wc


def _pack_state(h_re, h_im):
    b, g, p = h_re.shape
    tiles = g // GROUPS_PER_TILE
    st = jnp.stack([h_re.reshape(b, tiles, STATE_TILE), h_im.reshape(b, tiles, STATE_TILE)], axis=2)
    return st.reshape(b, tiles * 2 * STATE_TILE).astype(F32)


def _unpack_state(h, g, p):
    b = h.shape[0]
    st = h.reshape(b, g // GROUPS_PER_TILE, 2, STATE_TILE)
    return st[:, :, 0].reshape(b, g, p), st[:, :, 1].reshape(b, g, p)


def _s5_scan_body(u_ref, wx_ref, wc_ref, ar_ref, ai_ref, d_ref, h0_ref, y_ref, hout_ref, us_ref, ys_ref, xs_ref,
                  h_ref, *, bsz, steps, batch_on_lanes):
    tiles = wx_ref.shape[0]
    ch = tiles * LANES

    @pl.when(pl.program_id(0) == 0)
    def _():
        h_ref[...] = h0_ref[...]

    for j in range(tiles):
        if batch_on_lanes:
            for b in range(bsz):
                us_ref[j, pl.ds(b, steps, stride=bsz), :] = u_ref[:, ch * b + LANES * j:ch * b + LANES * (j + 1)]
        else:
            us_ref[j] = u_ref[:, LANES * j:LANES * (j + 1)]

    for j in range(tiles):
        xs_ref[:, 2 * STATE_TILE * j:2 * STATE_TILE * (j + 1)] = jnp.dot(
            us_ref[j].astype(BF16), wx_ref[j], preferred_element_type=F32)

    for j in range(tiles):
        re = slice(2 * STATE_TILE * j, 2 * STATE_TILE * j + STATE_TILE)
        im = slice(2 * STATE_TILE * j + STATE_TILE, 2 * STATE_TILE * (j + 1))
        a_r = jnp.broadcast_to(ar_ref[:, STATE_TILE * j:STATE_TILE * (j + 1)], (bsz, STATE_TILE))
        a_i = jnp.broadcast_to(ai_ref[:, STATE_TILE * j:STATE_TILE * (j + 1)], (bsz, STATE_TILE))

        def step(t, carry, re=re, im=im, a_r=a_r, a_i=a_i):
            hr, hi = carry
            r = pl.ds(pl.multiple_of(t * bsz, bsz), bsz)
            nhr = a_r * hr - a_i * hi + xs_ref[r, re]
            nhi = a_r * hi + a_i * hr + xs_ref[r, im]
            xs_ref[r, re] = nhr
            xs_ref[r, im] = nhi
            return nhr, nhi

        hr, hi = lax.fori_loop(0, steps, step, (h_ref[:, re], h_ref[:, im]))
        h_ref[:, re] = hr
        h_ref[:, im] = hi

    for j in range(tiles):
        cols = slice(LANES * j, LANES * (j + 1))
        hb = xs_ref[:, 2 * STATE_TILE * j:2 * STATE_TILE * (j + 1)].astype(BF16)
        y = jnp.dot(hb, wc_ref[j], preferred_element_type=F32) + d_ref[:, cols] * us_ref[j]
        if batch_on_lanes:
            ys_ref[j] = y
            for b in range(bsz):
                y_ref[:, ch * b + LANES * j:ch * b + LANES * (j + 1)] = ys_ref[j, pl.ds(b, steps, stride=bsz), :]
        else:
            y_ref[:, cols] = y
    hout_ref[...] = h_ref[...]


def _s5_scan(u, wx, wc, ar, ai, d_skip, h0, bsz, steps, batch_on_lanes):
    tiles = wx.shape[0]
    ch = tiles * LANES
    width = tiles * 2 * STATE_TILE
    rows = steps * bsz
    if batch_on_lanes:
        n_chunks = u.shape[0] // steps
        io_spec = pl.BlockSpec((steps, bsz * ch), lambda i: (i, 0))
    else:
        n_chunks = 1
        assert u.shape == (rows, ch)
        io_spec = pl.BlockSpec((rows, ch), lambda i: (0, 0))
    const = lambda shape: pl.BlockSpec(shape, lambda i: (0,) * len(shape))
    return pl.pallas_call(
        functools.partial(_s5_scan_body, bsz=bsz, steps=steps, batch_on_lanes=batch_on_lanes),
        grid=(n_chunks,),
        in_specs=[io_spec, const(wx.shape), const(wc.shape), const((1, tiles * STATE_TILE)),
                  const((1, tiles * STATE_TILE)), const((1, ch)), const((bsz, width))],
        out_specs=[io_spec, const((bsz, width))],
        out_shape=[jax.ShapeDtypeStruct(u.shape, F32), jax.ShapeDtypeStruct((bsz, width), F32)],
        scratch_shapes=[pltpu.VMEM((tiles, rows, LANES), F32), pltpu.VMEM((tiles, rows, LANES), F32),
                        pltpu.VMEM((rows, width), F32), pltpu.VMEM((bsz, width), F32)],
        compiler_params=_compiler_params(("arbitrary",)),
        name="s5_scan",
    )(u, wx, wc, ar.reshape(1, -1), ai.reshape(1, -1), d_skip.reshape(1, ch), h0)


def _post_mix_body(attn_ref, y_ref, x_ref, ga_ref, gs_ref, wglu_ref, bglu_ref, wo_ref, o_ref):
    aw = attn_ref.shape[1]
    an = _rms(attn_ref[...], ga_ref[...])
    z = jax.nn.gelu(y_ref[...])
    gate = jax.nn.sigmoid(jnp.dot(z.astype(BF16), wglu_ref[...], preferred_element_type=F32) + bglu_ref[...])
    sn = _rms(z * gate, gs_ref[...])
    o_ref[...] = (x_ref[...]
                  + jnp.dot(an.astype(BF16), wo_ref[:aw, :], preferred_element_type=F32)
                  + jnp.dot(sn.astype(BF16), wo_ref[aw:, :], preferred_element_type=F32))


def _post_mix(attn, y, x, g_attn, g_ssm, w_glu, b_glu, w_out, layer, tm, y_tiles_per_seq):
    n, d = x.shape
    aw, sw = attn.shape[1], w_glu.shape[1]
    row = lambda w: pl.BlockSpec((tm, w), lambda i: (i, 0))
    vec = lambda w: pl.BlockSpec((None, 1, w), lambda i: (layer, 0, 0))
    mat = lambda r, c: pl.BlockSpec((None, r, c), lambda i: (layer, 0, 0))
    if y_tiles_per_seq is None:
        y_spec = row(sw)
    else:
        y_spec = pl.BlockSpec((tm, sw), lambda i: (i % y_tiles_per_seq, i // y_tiles_per_seq))
    return pl.pallas_call(
        _post_mix_body,
        grid=(n // tm,),
        in_specs=[row(aw), y_spec, row(d), vec(aw), vec(sw), mat(sw, sw), vec(sw), mat(aw + sw, d)],
        out_specs=row(d),
        out_shape=jax.ShapeDtypeStruct((n, d), F32),
        compiler_params=_compiler_params(("parallel",)),
        name="post_mix",
    )(attn, y, x, g_attn, g_ssm, w_glu, b_glu, w_out)


def _finish(x_ref, acc_ref, gf_ref, o_ref):
    y = x_ref[...] + acc_ref[...]
    o_ref[...] = y if gf_ref is None else _rms(y, gf_ref[...])


def _ffn_body(x_ref, g_ref, wg_ref, wu_ref, wd_ref, *rest, final):
    gf_ref = rest[0] if final else None
    o_ref, hn_ref, acc_ref = rest[-3:]
    f = pl.program_id(1)

    @pl.when(f == 0)
    def _():
        hn_ref[...] = _rms(x_ref[...], g_ref[...]).astype(BF16)
        acc_ref[...] = jnp.zeros_like(acc_ref)

    hn = hn_ref[...]
    gate = jnp.dot(hn, wg_ref[...], preferred_element_type=F32)
    up = jnp.dot(hn, wu_ref[...], preferred_element_type=F32)
    act = (jax.nn.silu(gate) * up).astype(BF16)
    acc_ref[...] += jnp.dot(act, wd_ref[...], preferred_element_type=F32)

    @pl.when(f == pl.num_programs(1) - 1)
    def _():
        _finish(x_ref, acc_ref, gf_ref, o_ref)


def _ffn(x, g, wg, wu, wd, layer, g_final, tm, tf):
    n, d = x.shape
    ff = wg.shape[2]
    final = g_final is not None
    blk = layer // 2
    in_specs = [pl.BlockSpec((tm, d), lambda i, f: (i, 0)),
                pl.BlockSpec((None, 1, d), lambda i, f: (layer, 0, 0)),
                pl.BlockSpec((None, d, tf), lambda i, f: (blk, 0, f)),
                pl.BlockSpec((None, d, tf), lambda i, f: (blk, 0, f)),
                pl.BlockSpec((None, tf, d), lambda i, f: (blk, f, 0))]
    args = [x, g, wg, wu, wd]
    if final:
        in_specs.append(pl.BlockSpec((1, d), lambda i, f: (0, 0)))
        args.append(g_final.reshape(1, d))
    return pl.pallas_call(
        functools.partial(_ffn_body, final=final),
        grid=(n // tm, ff // tf),
        in_specs=in_specs,
        out_specs=pl.BlockSpec((tm, d), lambda i, f: (i, 0)),
        out_shape=jax.ShapeDtypeStruct((n, d), F32),
        scratch_shapes=[pltpu.VMEM((tm, d), BF16), pltpu.VMEM((tm, d), F32)],
        compiler_params=_compiler_params(("parallel", "arbitrary")),
        name="ffn_dense",
    )(*args)


def _route(hn, wr_ref, br_ref, n_experts, lane):
    logits = jnp.dot(hn, wr_ref[...], preferred_element_type=F32, precision=lax.Precision.HIGHEST) + br_ref[...]
    logits = jnp.where(lane < n_experts, logits, -jnp.inf)
    m1 = jnp.max(logits, axis=-1, keepdims=True)
    i1 = jnp.min(jnp.where(logits == m1, lane, float(LANES)), axis=-1, keepdims=True)
    rest_logits = jnp.where(lane == i1, -jnp.inf, logits)
    m2 = jnp.max(rest_logits, axis=-1, keepdims=True)
    i2 = jnp.min(jnp.where(rest_logits == m2, lane, float(LANES)), axis=-1, keepdims=True)
    e2 = jnp.exp(m2 - m1)
    return i1, i2, 1.0 / (1.0 + e2), e2 / (1.0 + e2)


def _moe_body(x_ref, g_ref, wr_ref, br_ref, wg_ref, wu_ref, wd_ref, *rest, n_experts, final):
    gf_ref = rest[0] if final else None
    o_ref, hn_ref, comb_ref, acc_ref = rest[-4:]
    e, f = pl.program_id(1), pl.program_id(2)
    lane = lax.broadcasted_iota(jnp.int32, comb_ref.shape, 1).astype(F32)

    @pl.when((e == 0) & (f == 0))
    def _():
        hn = _rms(x_ref[...], g_ref[...])
        hn_ref[...] = hn.astype(BF16)
        acc_ref[...] = jnp.zeros_like(acc_ref)
        i1, i2, g1, g2 = _route(hn, wr_ref, br_ref, n_experts, lane)
        comb_ref[...] = jnp.where(lane == i1, g1, 0.0) + jnp.where(lane == i2, g2, 0.0)

    hn = hn_ref[...]
    gate = jnp.dot(hn, wg_ref[...], preferred_element_type=F32)
    up = jnp.dot(hn, wu_ref[...], preferred_element_type=F32)
    act = (jax.nn.silu(gate) * up).astype(BF16)
    weight = jnp.sum(jnp.where(lane == e.astype(F32), comb_ref[...], 0.0), axis=-1, keepdims=True)
    acc_ref[...] += weight * jnp.dot(act, wd_ref[...], preferred_element_type=F32)

    @pl.when((e == n_experts - 1) & (f == pl.num_programs(2) - 1))
    def _():
        _finish(x_ref, acc_ref, gf_ref, o_ref)


def _moe(x, g, w_router, b_router, wg, wu, wd, layer, g_final, tm, tf):
    n, d = x.shape
    _, n_experts, _, ff = wg.shape
    final = g_final is not None
    blk = layer // 2
    wr = jnp.zeros((d, LANES), F32).at[:, :n_experts].set(w_router.astype(F32))
    br = jnp.zeros((1, LANES), F32).at[0, :n_experts].set(b_router.astype(F32))
    in_specs = [pl.BlockSpec((tm, d), lambda i, e, f: (i, 0)),
                pl.BlockSpec((None, 1, d), lambda i, e, f: (layer, 0, 0)),
                pl.BlockSpec((d, LANES), lambda i, e, f: (0, 0)),
                pl.BlockSpec((1, LANES), lambda i, e, f: (0, 0)),
                pl.BlockSpec((None, None, d, tf), lambda i, e, f: (blk, e, 0, f)),
                pl.BlockSpec((None, None, d, tf), lambda i, e, f: (blk, e, 0, f)),
                pl.BlockSpec((None, None, tf, d), lambda i, e, f: (blk, e, f, 0))]
    args = [x, g, wr, br, wg, wu, wd]
    if final:
        in_specs.append(pl.BlockSpec((1, d), lambda i, e, f: (0, 0)))
        args.append(g_final.reshape(1, d))
    return pl.pallas_call(
        functools.partial(_moe_body, n_experts=n_experts, final=final),
        grid=(n // tm, n_experts, ff // tf),
        in_specs=in_specs,
        out_specs=pl.BlockSpec((tm, d), lambda i, e, f: (i, 0)),
        out_shape=jax.ShapeDtypeStruct((n, d), F32),
        scratch_shapes=[pltpu.VMEM((tm, d), BF16), pltpu.VMEM((tm, LANES), F32), pltpu.VMEM((tm, d), F32)],
        compiler_params=_compiler_params(("parallel", "arbitrary", "arbitrary")),
        name="moe",
    )(*args)


EXPERT_ROW_TILE = 512
MOE_DMA_ROWS = 256


def _router_body(x_ref, g_ref, wr_ref, br_ref, hn_ref, route_ref, *, n_experts):
    lane = lax.broadcasted_iota(jnp.int32, route_ref.shape, 1).astype(F32)
    hn = _rms(x_ref[...], g_ref[...])
    hn_ref[...] = hn
    i1, i2, g1, g2 = _route(hn, wr_ref, br_ref, n_experts, lane)
    route_ref[...] = jnp.where(lane == 0.0, i1, jnp.where(lane == 1.0, i2, jnp.where(
        lane == 2.0, g1, jnp.where(lane == 3.0, g2, 0.0))))


def _router(x, g, wr, br, layer, n_experts, tm):
    n, d = x.shape
    return pl.pallas_call(
        functools.partial(_router_body, n_experts=n_experts),
        grid=(n // tm,),
        in_specs=[pl.BlockSpec((tm, d), lambda i: (i, 0)),
                  pl.BlockSpec((None, 1, d), lambda i: (layer, 0, 0)),
                  pl.BlockSpec((d, LANES), lambda i: (0, 0)),
                  pl.BlockSpec((1, LANES), lambda i: (0, 0))],
        out_specs=[pl.BlockSpec((tm, d), lambda i: (i, 0)), pl.BlockSpec((tm, LANES), lambda i: (i, 0))],
        out_shape=[jax.ShapeDtypeStruct((n, d), F32), jax.ShapeDtypeStruct((n, LANES), F32)],
        compiler_params=_compiler_params(("parallel",)),
        name="moe_router",
    )(x, g, wr, br)


def _dispatch_plan(experts, n_experts, n_tiles):
    flat = experts.reshape(-1)
    onehot = (flat[:, None] == jnp.arange(n_experts, dtype=jnp.int32)[None, :]).astype(jnp.int32)
    running = jnp.cumsum(onehot, axis=0)
    rank = jnp.sum(onehot * running, axis=1) - 1
    tiles_per_expert = (running[-1] + EXPERT_ROW_TILE - 1) // EXPERT_ROW_TILE
    tile_end = jnp.cumsum(tiles_per_expert)
    group_start = (tile_end - tiles_per_expert) * EXPERT_ROW_TILE
    slot = jnp.sum(onehot * group_start[None, :], axis=1) + rank
    tile = jnp.arange(n_tiles, dtype=jnp.int32)
    tile_expert = jnp.minimum(jnp.sum((tile[:, None] >= tile_end[None, :]).astype(jnp.int32), axis=1),
                              n_experts - 1)
    tile_valid = (tile < tile_end[-1]).astype(jnp.int32)
    return slot.astype(jnp.int32), tile_expert.astype(jnp.int32), tile_valid


def _row_copy(src_ref, src_row, dst_ref, dst_row, sem):
    return pltpu.make_async_copy(src_ref.at[pl.ds(src_row, 1)], dst_ref.at[pl.ds(dst_row, 1)], sem)


def _slot_tokens_body(slot_ref, tok_ref):
    def clear(i, c):
        for k in range(8):
            tok_ref[8 * i + k] = 0
        return c

    def fill(i, c):
        for k in range(8):
            a = 8 * i + k
            tok_ref[slot_ref[a]] = a // 2
        return c

    lax.fori_loop(0, tok_ref.shape[0] // 8, clear, 0)
    lax.fori_loop(0, slot_ref.shape[0] // 8, fill, 0)


def _slot_tokens(slot, n_slots):
    smem = pl.BlockSpec(memory_space=pltpu.SMEM)
    return pl.pallas_call(
        _slot_tokens_body,
        in_specs=[smem],
        out_specs=smem,
        out_shape=jax.ShapeDtypeStruct((n_slots,), jnp.int32),
        name="moe_slot_tokens",
    )(slot)


def _experts_body(tok_ref, te_ref, tv_ref, hn_ref, wg_ref, wu_ref, wd_ref, y_ref, xbuf_ref, acc_ref, sem, *, tf):
    t = pl.program_id(0)
    last = pl.num_programs(0) - 1
    rows = y_ref.shape[0]
    n_f = wg_ref.shape[1] // tf
    cur = t % 2

    def copy(tile, buf, j):
        return _row_copy(hn_ref, tok_ref[tile * rows + j], xbuf_ref.at[buf], j, sem.at[buf])

    def start_all(tile, buf):
        def body(j, c):
            copy(tile, buf, j).start()
            return c
        lax.fori_loop(0, rows, body, 0)

    def wait_all(tile, buf):
        def body(j, c):
            copy(tile, buf, j).wait()
            return c
        lax.fori_loop(0, rows, body, 0)

    @pl.when(t == 0)
    def _():
        start_all(0, 0)

    @pl.when((t == 0) | (tv_ref[jnp.maximum(t - 1, 0)] != 0))
    def _():
        wait_all(t, cur)

    live = tv_ref[t] != 0

    @pl.when(live)
    def _():
        nxt = jnp.minimum(t + 1, last)
        xb = xbuf_ref[cur].astype(BF16)
        for f in range(n_f):
            cols = slice(f * tf, (f + 1) * tf)
            gate = jnp.dot(xb, wg_ref[:, cols], preferred_element_type=F32)
            up = jnp.dot(xb, wu_ref[:, cols], preferred_element_type=F32)
            act = (jax.nn.silu(gate) * up).astype(BF16)
            part = jnp.dot(act, wd_ref[cols, :], preferred_element_type=F32)
            if f == 0:
                acc_ref[...] = part
            else:
                acc_ref[...] += part
            for j in range(f * rows // n_f, (f + 1) * rows // n_f):
                copy(nxt, 1 - cur, j).start()
        y_ref[...] = acc_ref[...]

    @pl.when(jnp.logical_not(live))
    def _():
        y_ref[...] = jnp.zeros_like(y_ref)

    @pl.when((t == last) & live)
    def _():
        wait_all(last, 1 - cur)


def _experts(tok, tile_expert, tile_valid, hn, wg, wu, wd, layer, tf):
    d = hn.shape[1]
    ff = wg.shape[3]
    blk = layer // 2
    n_tiles = tile_expert.shape[0]
    weights = lambda r, c: pl.BlockSpec((None, None, r, c), lambda t, tok, te, tv: (blk, te[t], 0, 0))
    return pl.pallas_call(
        functools.partial(_experts_body, tf=tf),
        grid_spec=pltpu.PrefetchScalarGridSpec(
            num_scalar_prefetch=3,
            grid=(n_tiles,),
            in_specs=[pl.BlockSpec(memory_space=pl.ANY), weights(d, ff), weights(d, ff), weights(ff, d)],
            out_specs=pl.BlockSpec((EXPERT_ROW_TILE, d), lambda t, tok, te, tv: (t, 0)),
            scratch_shapes=[pltpu.VMEM((2, EXPERT_ROW_TILE, d), F32), pltpu.VMEM((EXPERT_ROW_TILE, d), F32),
                            pltpu.SemaphoreType.DMA((2,))]),
        out_shape=jax.ShapeDtypeStruct((n_tiles * EXPERT_ROW_TILE, d), F32),
        compiler_params=pltpu.CompilerParams(dimension_semantics=("arbitrary",), disable_bounds_checks=True,
                                             vmem_limit_bytes=VMEM_LIMIT_BYTES),
        name="moe_experts",
    )(tok, tile_expert, tile_valid, hn, wg, wu, wd)


def _combine_body(slot_ref, x_ref, route_ref, y_ref, *rest, final):
    gf_ref = rest[0] if final else None
    o_ref, ya_ref, yb_ref, sem = rest[-4:]
    base = pl.program_id(0) * MOE_DMA_ROWS

    def copies(j):
        t = base + j
        return [_row_copy(y_ref, slot_ref[2 * t], ya_ref, j, sem.at[0]),
                _row_copy(y_ref, slot_ref[2 * t + 1], yb_ref, j, sem.at[1])]

    def start(j, c):
        for cp in copies(j):
            cp.start()
        return c

    def wait(j, c):
        for cp in copies(j):
            cp.wait()
        return c

    lax.fori_loop(0, MOE_DMA_ROWS, start, 0)
    lax.fori_loop(0, MOE_DMA_ROWS, wait, 0)
    y = x_ref[...] + route_ref[:, 2:3] * ya_ref[...] + route_ref[:, 3:4] * yb_ref[...]
    o_ref[...] = y if gf_ref is None else _rms(y, gf_ref[...])


def _combine(slot, x, route, y, g_final):
    n, d = x.shape
    final = g_final is not None
    in_specs = [pl.BlockSpec((MOE_DMA_ROWS, d), lambda i, s: (i, 0)),
                pl.BlockSpec((MOE_DMA_ROWS, LANES), lambda i, s: (i, 0)),
                pl.BlockSpec(memory_space=pl.ANY)]
    args = [slot, x, route, y]
    if final:
        in_specs.append(pl.BlockSpec((1, d), lambda i, s: (0, 0)))
        args.append(g_final.reshape(1, d))
    return pl.pallas_call(
        functools.partial(_combine_body, final=final),
        grid_spec=pltpu.PrefetchScalarGridSpec(
            num_scalar_prefetch=1,
            grid=(n // MOE_DMA_ROWS,),
            in_specs=in_specs,
            out_specs=pl.BlockSpec((MOE_DMA_ROWS, d), lambda i, s: (i, 0)),
            scratch_shapes=[pltpu.VMEM((MOE_DMA_ROWS, d), F32), pltpu.VMEM((MOE_DMA_ROWS, d), F32),
                            pltpu.SemaphoreType.DMA((2,))]),
        out_shape=jax.ShapeDtypeStruct((n, d), F32),
        compiler_params=pltpu.CompilerParams(dimension_semantics=("arbitrary",), disable_bounds_checks=True,
                                             vmem_limit_bytes=VMEM_LIMIT_BYTES),
        name="moe_combine",
    )(*args)


def _moe_sorted(x, g, w_router, b_router, wg, wu, wd, layer, g_final, tm, tf):
    n, d = x.shape
    n_experts = wg.shape[1]
    wr = jnp.zeros((d, LANES), F32).at[:, :n_experts].set(w_router.astype(F32))
    br = jnp.zeros((1, LANES), F32).at[0, :n_experts].set(b_router.astype(F32))
    hn, route = _router(x, g, wr, br, layer, n_experts, tm)
    n_tiles = 2 * n // EXPERT_ROW_TILE + n_experts
    slot, tile_expert, tile_valid = _dispatch_plan(route[:, :2].astype(jnp.int32), n_experts, n_tiles)
    tok = _slot_tokens(slot, n_tiles * EXPERT_ROW_TILE)
    y = _experts(tok, tile_expert, tile_valid, hn, wg, wu, wd, layer, tf)
    return _combine(slot, x, route, y, g_final)


PROMPT_ROW_TILE = 512
PROMPT_SCAN_STEPS = 64


def kernel(x_prompt, x_sample, cache_win_k, cache_win_v, state_ssm_re, state_ssm_im, rel_bias, norm_mix, w_in, ssm_a_re, ssm_a_im, ssm_log_dt, ssm_b_re, ssm_b_im, ssm_c_re, ssm_c_im, ssm_d, ssm_w_glu, ssm_b_glu, norm_attn_out, norm_ssm_out, w_out, norm_ffn, ffn_w_gate, ffn_w_up, ffn_w_down, moe_w_router, moe_b_router, moe_w_gate, moe_w_up, moe_w_down, norm_final):
    batch, seq, d_model = x_prompt.shape
    dec_batch, dec_seq, _ = x_sample.shape
    depth = w_in.shape[0]
    win, n_heads = cache_win_k.shape[2:4]
    attn_width = n_heads * HEAD_DIM
    n_groups, n_state = ssm_a_re.shape[1:]
    ssm_width = n_groups * SSM_GROUP
    assert w_in.shape[2] == 3 * attn_width + ssm_width and seq == MAX_WINDOW
    dec_rows = dec_batch * dec_seq

    band_bias = _band_bias(rel_bias)
    sample_tabs = _sample_tables(rel_bias, win, dec_seq)
    ar, ai, bb_re, bb_im = _s5_params(ssm_a_re, ssm_a_im, ssm_log_dt, ssm_b_re, ssm_b_im)
    cache_k = jnp.transpose(cache_win_k, (0, 1, 3, 4, 2))
    cache_v = jnp.transpose(cache_win_v, (0, 1, 3, 4, 2))

    rows3 = lambda a: a.reshape(a.shape[0], 1, a.shape[1])
    norm_mix3, norm_ffn3, b_glu3 = rows3(norm_mix), rows3(norm_ffn), rows3(ssm_b_glu)
    g_attn3, g_ssm3 = rows3(norm_attn_out), rows3(norm_ssm_out)
    w_in_b, w_out_b, w_glu_b = w_in.astype(BF16), w_out.astype(BF16), ssm_w_glu.astype(BF16)
    ffn_b = [w.astype(BF16) for w in (ffn_w_gate, ffn_w_up, ffn_w_down)]
    moe_b = [w.astype(BF16) for w in (moe_w_gate, moe_w_up, moe_w_down)]
    tf = ffn_w_gate.shape[2] // 2

    def mix_and_ffn(x, l, attn, y, tm, y_tiles_per_seq):
        g_final = norm_final if l == depth - 1 else None
        x = _post_mix(attn, y, x, g_attn3, g_ssm3, w_glu_b, b_glu3, w_out_b, l, tm, y_tiles_per_seq)
        if l % 2 == 0:
            return _ffn(x, norm_ffn3, *ffn_b, l, g_final, tm, tf)
        n_experts = moe_w_gate.shape[1]
        moe = _moe_sorted if 2 * x.shape[0] >= n_experts * EXPERT_ROW_TILE else _moe
        return moe(x, norm_ffn3, moe_w_router[l // 2], moe_b_router[l // 2], *moe_b, l, g_final, tm, tf)

    xp = x_prompt.reshape(batch * seq, d_model)
    xs = x_sample.reshape(dec_rows, d_model)
    zero_state = jnp.zeros((batch, 2 * n_groups * n_state), F32)
    outs = [[] for _ in range(8)]
    for l in range(depth):
        wx, wc = _pack_s5_weights(bb_re[l], bb_im[l], ssm_c_re[l].astype(F32), ssm_c_im[l].astype(F32))

        qkv, k_t, v_t, u = _norm_proj_prompt(xp, norm_mix3, w_in_b, l, batch, seq, attn_width, PROMPT_ROW_TILE)
        attn = _prompt_attn(qkv, band_bias, batch, seq, attn_width)
        y, h_fin = _s5_scan(u, wx, wc, ar[l], ai[l], ssm_d[l], zero_state, batch, PROMPT_SCAN_STEPS, True)
        xp = mix_and_ffn(xp, l, attn, y, PROMPT_ROW_TILE, seq // PROMPT_ROW_TILE)
        outs[0].append(k_t)
        outs[1].append(v_t)
        h_re, h_im = _unpack_state(h_fin, n_groups, n_state)
        outs[2].append(h_re)
        outs[3].append(h_im)

        proj = _norm_proj(xs, norm_mix3, w_in_b, l, dec_rows)
        q, k_new, v_new = (proj[:, i * attn_width:(i + 1) * attn_width].reshape(dec_batch, dec_seq, n_heads, HEAD_DIM)
                           for i in range(3))
        attn = _sample_attn(q, k_new, v_new, cache_k, cache_v, l, sample_tabs).reshape(dec_rows, attn_width)
        u = proj[:, 3 * attn_width:].reshape(dec_batch, dec_seq, ssm_width).transpose(1, 0, 2)
        y, h_fin = _s5_scan(u.reshape(dec_rows, ssm_width), wx, wc, ar[l], ai[l], ssm_d[l],
                            _pack_state(state_ssm_re[l], state_ssm_im[l]), dec_batch, dec_seq, False)
        y = y.reshape(dec_seq, dec_batch, ssm_width).transpose(1, 0, 2).reshape(dec_rows, ssm_width)
        xs = mix_and_ffn(xs, l, attn, y, dec_rows, None)
        outs[4].append(k_new)
        outs[5].append(v_new)
        h_re, h_im = _unpack_state(h_fin, n_groups, n_state)
        outs[6].append(h_re)
        outs[7].append(h_im)

    def window_out(parts):
        a = jnp.stack(parts, 0).reshape(depth, batch, n_heads, HEAD_DIM, seq)
        return jnp.transpose(a, (0, 1, 4, 2, 3))

    return (xp.reshape(batch, seq, d_model), xs.reshape(dec_batch, dec_seq, d_model),
            window_out(outs[0]), window_out(outs[1]), *[jnp.stack(o, 0) for o in outs[2:]])
```

```python
import functools

import numpy as np
import jax
import jax.numpy as jnp
from jax import lax
from jax.experimental import pallas as pl
from jax.experimental.pallas import tpu as pltpu

F32 = jnp.float32
BF16 = jnp.bfloat16

HEAD_DIM = 64
DILATION_PATTERNS = ((128, 1), (512, 4), (2048, 16))
MAX_WINDOW = 2048
N_BUCKETS = 32
SSM_GROUP = 16
SSM_STATE = 64
RMS_EPS = 1e-6
NEG_BIG = -1e30
LOG2_E = 1.4426950408889634

LANES = 128
Q_TILE = 128
ATTN_UNROLL = 4
GROUPS_PER_TILE = LANES // SSM_GROUP
STATE_TILE = GROUPS_PER_TILE * SSM_STATE
VMEM_LIMIT_BYTES = 56 * 1024 * 1024


def _compiler_params(semantics):
    return pltpu.CompilerParams(dimension_semantics=semantics, vmem_limit_bytes=VMEM_LIMIT_BYTES)


def _rms(x, g):
    return x * lax.rsqrt(jnp.mean(x * x, axis=-1, keepdims=True) + RMS_EPS) * g


def _t5_causal_bucket(dist):
    max_exact = N_BUCKETS // 2
    d = np.asarray(dist, dtype=np.int64)
    large = max_exact + (np.log(np.maximum(d, max_exact) / max_exact)
                         / np.log(MAX_WINDOW / max_exact)
                         * (N_BUCKETS - max_exact)).astype(np.int64)
    large = np.minimum(large, N_BUCKETS - 1)
    return np.where(d < max_exact, d, large).astype(np.int32)


def _multiplicity(d):
    m = np.zeros(d.shape, np.int32)
    for window, dil in DILATION_PATTERNS:
        m += ((d >= 0) & (d <= window) & (d % dil == 0)).astype(np.int32)
    return m


def _norm_proj_body(x_ref, g_ref, w_ref, o_ref):
    hn = _rms(x_ref[...], g_ref[...])
    o_ref[...] = jnp.dot(hn.astype(BF16), w_ref[...], preferred_element_type=F32)


def _norm_proj(x, g, w, layer, tm):
    n, d = x.shape
    c = w.shape[2]
    return pl.pallas_call(
        _norm_proj_body,
        grid=(n // tm,),
        in_specs=[pl.BlockSpec((tm, d), lambda i: (i, 0)),
                  pl.BlockSpec((None, 1, d), lambda i: (layer, 0, 0)),
                  pl.BlockSpec((None, d, c), lambda i: (layer, 0, 0))],
        out_specs=pl.BlockSpec((tm, c), lambda i: (i, 0)),
        out_shape=jax.ShapeDtypeStruct((n, c), F32),
        compiler_params=_compiler_params(("parallel",)),
        name="norm_proj",
    )(x, g, w)


def _norm_proj_prompt_body(x_ref, g_ref, w_ref, qkv_ref, kt_ref, vt_ref, u_ref, *, attn_width):
    hn = _rms(x_ref[...], g_ref[...])
    proj = jnp.dot(hn.astype(BF16), w_ref[...], preferred_element_type=F32)
    qkv_ref[...] = proj[:, :3 * attn_width]
    kt_ref[...] = proj[:, attn_width:2 * attn_width].T
    vt_ref[...] = proj[:, 2 * attn_width:3 * attn_width].T
    u_ref[...] = proj[:, 3 * attn_width:]


def _norm_proj_prompt(x, g, w, layer, batch, seq, attn_width, tm):
    n, d = x.shape
    c = w.shape[2]
    ch = c - 3 * attn_width
    per_seq = seq // tm
    return pl.pallas_call(
        functools.partial(_norm_proj_prompt_body, attn_width=attn_width),
        grid=(n // tm,),
        in_specs=[pl.BlockSpec((tm, d), lambda i: (i, 0)),
                  pl.BlockSpec((None, 1, d), lambda i: (layer, 0, 0)),
                  pl.BlockSpec((None, d, c), lambda i: (layer, 0, 0))],
        out_specs=[pl.BlockSpec((tm, 3 * attn_width), lambda i: (i, 0)),
                   pl.BlockSpec((None, attn_width, tm), lambda i: (i // per_seq, 0, i % per_seq)),
                   pl.BlockSpec((None, attn_width, tm), lambda i: (i // per_seq, 0, i % per_seq)),
                   pl.BlockSpec((tm, ch), lambda i: (i % per_seq, i // per_seq))],
        out_shape=[jax.ShapeDtypeStruct((n, 3 * attn_width), F32),
                   jax.ShapeDtypeStruct((batch, attn_width, seq), F32),
                   jax.ShapeDtypeStruct((batch, attn_width, seq), F32),
                   jax.ShapeDtypeStruct((seq, batch * ch), F32)],
        compiler_params=_compiler_params(("parallel",)),
        name="norm_proj_prompt",
    )(x, g, w)


def _band_bias(rel_bias):
    period = 3 * Q_TILE
    n_heads = rel_bias.shape[1]
    rows0 = []
    for window, dil in DILATION_PATTERNS:
        assert window // dil == Q_TILE
        vec = rel_bias[_t5_causal_bucket((Q_TILE - np.arange(Q_TILE + 1)) * dil)].astype(F32)
        rows0.append(jnp.concatenate([vec.T, jnp.full((n_heads, period - Q_TILE - 1), NEG_BIG, F32)], axis=1))
    row0 = jnp.stack(rows0, 0)
    flat = jnp.tile(row0, (1, 1, Q_TILE))[:, :, :Q_TILE * (period - 1)]
    band = flat.reshape(len(DILATION_PATTERNS), n_heads, Q_TILE, period - 1)[:, :, :, :2 * Q_TILE]
    return band * LOG2_E


def _prompt_attn_body(q_ref, k_ref, v_ref, bias_ref, o_ref, op_ref, m_ref, l_ref):
    seq = q_ref.shape[0]
    lane = lax.broadcasted_iota(jnp.int32, (Q_TILE, LANES), 1)
    head0 = lane < HEAD_DIM
    scale = HEAD_DIM ** -0.5 * LOG2_E

    def rows(start, size, dil):
        return pl.ds(start, size) if dil == 1 else pl.ds(start, size, stride=dil)

    def tiles(p, dil, starts, first):
        nk = Q_TILE if first else 2 * Q_TILE
        bias = (bias_ref[p, :, :, Q_TILE:] if first else bias_ref[p]).reshape(2 * Q_TILE, nk)
        work = []
        for q_start, k_start in starts:
            qsl, ksl = rows(q_start, Q_TILE, dil), rows(k_start, nk, dil)
            qt = q_ref[qsl, :] * scale
            q2 = jnp.concatenate([jnp.where(head0, qt, 0.0), jnp.where(head0, 0.0, qt)], axis=0).astype(BF16)
            work.append([qsl, v_ref[ksl, :].astype(BF16), q2, k_ref[ksl, :].astype(BF16)])
        for w in work:
            w[2] = lax.dot_general(w[2], w[3], (((1,), (1,)), ((), ())), preferred_element_type=F32) + bias
        for w in work:
            w[3] = jnp.max(w[2], axis=-1, keepdims=True)
        for w in work:
            w[2] = jnp.exp2(w[2] - w[3])
        for w in work:
            w.append(jnp.sum(w[2], axis=-1, keepdims=True))
            w[2] = jnp.dot(w[2].astype(BF16), w[1], preferred_element_type=F32)
        for qsl, _, o, m, l in work:
            op_ref[p, qsl, :] = jnp.where(head0, o[:Q_TILE], o[Q_TILE:])
            m_ref[p, qsl, :] = jnp.where(head0, m[:Q_TILE], m[Q_TILE:])
            l_ref[p, qsl, :] = jnp.where(head0, l[:Q_TILE], l[Q_TILE:])

    for p, (_, dil) in enumerate(DILATION_PATTERNS):
        n_tiles = seq // dil // Q_TILE
        stride = dil * Q_TILE
        unroll = min(ATTN_UNROLL, dil)
        if dil == 1:
            unroll = max(u for u in range(1, 2 * ATTN_UNROLL + 1) if (n_tiles - 1) % u == 0)
            for r in range(dil):
                tiles(p, dil, [(r, r)], True)

                def blk(t, c, p=p, dil=dil, r=r, stride=stride, unroll=unroll):
                    first_t = 1 + t * unroll
                    tiles(p, dil, [(r + stride * (first_t + j), r + stride * (first_t + j - 1))
                                   for j in range(unroll)], False)
                    return c
                lax.fori_loop(0, (n_tiles - 1) // unroll, blk, 0)
        else:
            assert dil % unroll == 0

            def classes(g, c, p=p, dil=dil, n_tiles=n_tiles, stride=stride, unroll=unroll):
                rs = [g * unroll + j for j in range(unroll)]
                tiles(p, dil, [(r, r) for r in rs], True)
                if n_tiles > 1:
                    def blk(t, c2):
                        tiles(p, dil, [(r + stride * t, r + stride * (t - 1)) for r in rs], False)
                        return c2
                    lax.fori_loop(1, n_tiles, blk, 0)
                return c
            lax.fori_loop(0, dil // unroll, classes, 0)

    def combine(t, carry):
        r = pl.ds(pl.multiple_of(t * Q_TILE, Q_TILE), Q_TILE)
        ms = [m_ref[p, r, :] for p in range(len(DILATION_PATTERNS))]
        mx = functools.reduce(jnp.maximum, ms)
        ws = [jnp.exp2(m - mx) for m in ms]
        num = sum(w * op_ref[p, r, :] for p, w in enumerate(ws))
        den = sum(w * l_ref[p, r, :] for p, w in enumerate(ws))
        o_ref[r, :] = num / den
        return carry

    lax.fori_loop(0, seq // Q_TILE, combine, 0)


def _prompt_attn(proj, bias, batch, seq, attn_width):
    n_pairs = attn_width // LANES
    n_pat = len(DILATION_PATTERNS)
    assert seq % (Q_TILE * max(d for _, d in DILATION_PATTERNS)) == 0
    return pl.pallas_call(
        _prompt_attn_body,
        grid=(batch, n_pairs),
        in_specs=[pl.BlockSpec((seq, LANES), lambda b, h: (b, h)),
                  pl.BlockSpec((seq, LANES), lambda b, h: (b, n_pairs + h)),
                  pl.BlockSpec((seq, LANES), lambda b, h: (b, 2 * n_pairs + h)),
                  pl.BlockSpec((n_pat, 2, Q_TILE, 2 * Q_TILE), lambda b, h: (0, h, 0, 0))],
        out_specs=pl.BlockSpec((seq, LANES), lambda b, h: (b, h)),
        out_shape=jax.ShapeDtypeStruct((batch * seq, attn_width), F32),
        scratch_shapes=[pltpu.VMEM((n_pat, seq, LANES), F32)] * 3,
        compiler_params=_compiler_params(("parallel", "parallel")),
        name="prompt_attn",
    )(proj, proj, proj, bias)


def _sample_tables(rel_bias, win, n_new):
    n_heads = rel_bias.shape[1]
    vec_t = rel_bias[_t5_causal_bucket(np.arange(win + n_new))].astype(F32).T
    flipped = vec_t[:, ::-1]
    t = np.arange(n_new)
    d_cache = win + t[:, None] - np.arange(win)[None, :]
    d_new = t[:, None] - t[None, :]
    b_cache = jnp.stack([flipped[:, n_new - 1 - i:n_new - 1 - i + win] for i in range(n_new)], 0)
    b_new = jnp.transpose(vec_t[:, np.maximum(d_new, 0)], (1, 0, 2))
    out = []
    for b, d in ((b_cache, d_cache), (b_new, d_new)):
        mult = _multiplicity(d)
        b = jnp.where((mult > 0)[:, None, :], b, NEG_BIG).reshape(n_new * n_heads, d.shape[1])
        m = np.broadcast_to(np.maximum(mult, 1)[:, None, :], (n_new, n_heads, d.shape[1]))
        out += [b, jnp.asarray(m.reshape(n_new * n_heads, d.shape[1]), F32)]
    return out


def _sample_attn_body(q_ref, kc_ref, kn_ref, vc_ref, vn_ref, bc_ref, mc_ref, bn_ref, mn_ref, hm_ref, o_ref):
    n_new = o_ref.shape[0]
    win = kc_ref.shape[-1]
    qb = (q_ref[...] * HEAD_DIM ** -0.5).astype(BF16)
    s_c = jnp.dot(qb, kc_ref[...].reshape(-1, win).astype(BF16), preferred_element_type=F32) + bc_ref[...]
    s_n = jnp.dot(qb, kn_ref[...].astype(BF16), preferred_element_type=F32) + bn_ref[...]
    m = jnp.maximum(jnp.max(s_c, axis=-1, keepdims=True), jnp.max(s_n, axis=-1, keepdims=True))
    e_c = jnp.exp(s_c - m) * mc_ref[...]
    e_n = jnp.exp(s_n - m) * mn_ref[...]
    den = jnp.sum(e_c, axis=-1, keepdims=True) + jnp.sum(e_n, axis=-1, keepdims=True)
    nt = (((1,), (1,)), ((), ()))
    o = (lax.dot_general(e_c.astype(BF16), vc_ref[...].reshape(-1, win).astype(BF16), nt,
                         preferred_element_type=F32)
         + lax.dot_general(e_n.astype(BF16), vn_ref[...].astype(BF16), nt, preferred_element_type=F32))
    o = o / den * hm_ref[...]
    o_ref[...] = jnp.sum(o.reshape(n_new, -1, o.shape[-1]), axis=1)


def _sample_attn(q, k_new, v_new, cache_k, cache_v, layer, tables):
    depth, batch, n_heads, hd, win = cache_k.shape
    n_new = q.shape[1]
    width = n_heads * hd
    rows = n_new * n_heads
    eye = jnp.eye(n_heads, dtype=F32)
    q_bd = jnp.einsum('bthd,hg->btghd', q, eye).reshape(batch, rows, width)
    kn = jnp.transpose(k_new, (0, 2, 3, 1)).reshape(batch, width, n_new)
    vn = jnp.transpose(v_new, (0, 2, 3, 1)).reshape(batch, width, n_new)
    head_mask = jnp.asarray(np.kron(np.tile(np.eye(n_heads), (n_new, 1)), np.ones((1, hd))), F32)
    cache_spec = pl.BlockSpec((None, None, n_heads, hd, win), lambda b: (layer, b, 0, 0, 0))
    new_spec = pl.BlockSpec((None, width, n_new), lambda b: (b, 0, 0))
    const = lambda a: pl.BlockSpec(a.shape, lambda b: (0, 0))
    return pl.pallas_call(
        _sample_attn_body,
        grid=(batch,),
        in_specs=[pl.BlockSpec((None, rows, width), lambda b: (b, 0, 0)),
                  cache_spec, new_spec, cache_spec, new_spec] + [const(t) for t in tables] + [const(head_mask)],
        out_specs=pl.BlockSpec((None, n_new, width), lambda b: (b, 0, 0)),
        out_shape=jax.ShapeDtypeStruct((batch, n_new, width), F32),
        compiler_params=_compiler_params(("parallel",)),
        name="sample_attn",
    )(q_bd, cache_k, kn, cache_v, vn, *tables, head_mask)


def _s5_param_body(are_ref, aim_ref, ldt_ref, bre_ref, bim_ref, ar_ref, ai_ref, bbre_ref, bbim_ref):
    a_re, a_im = are_ref[...], aim_ref[...]
    dt = jnp.exp(ldt_ref[...])
    mag = jnp.exp(dt * a_re)
    ar, ai = mag * jnp.cos(dt * a_im), mag * jnp.sin(dt * a_im)
    nr, ni = ar - 1.0, ai
    den = a_re * a_re + a_im * a_im
    fr = (nr * a_re + ni * a_im) / den
    fi = (ni * a_re - nr * a_im) / den
    br, bi = bre_ref[...], bim_ref[...]
    ar_ref[...] = ar
    ai_ref[...] = ai
    bbre_ref[...] = fr[:, None, :] * br - fi[:, None, :] * bi
    bbim_ref[...] = fr[:, None, :] * bi + fi[:, None, :] * br


def _s5_params(a_re, a_im, log_dt, b_re, b_im):
    depth, g, p = a_re.shape
    c = b_re.shape[-1]
    n = depth * g
    full = lambda shape: pl.BlockSpec(shape, lambda: (0,) * len(shape))
    ar, ai, bbre, bbim = pl.pallas_call(
        _s5_param_body,
        in_specs=[full((n, p)), full((n, p)), full((n, 1)), full((n, c, p)), full((n, c, p))],
        out_specs=[full((n, p)), full((n, p)), full((n, c, p)), full((n, c, p))],
        out_shape=[jax.ShapeDtypeStruct((n, p), F32), jax.ShapeDtypeStruct((n, p), F32),
                   jax.ShapeDtypeStruct((n, c, p), F32), jax.ShapeDtypeStruct((n, c, p), F32)],
        name="s5_params",
    )(a_re.reshape(n, p), a_im.reshape(n, p), log_dt.reshape(n, 1),
      jnp.swapaxes(b_re, -1, -2).reshape(n, c, p), jnp.swapaxes(b_im, -1, -2).reshape(n, c, p))
    return (ar.reshape(depth, g, p), ai.reshape(depth, g, p),
            bbre.reshape(depth, g, c, p), bbim.reshape(depth, g, c, p))


def _pack_s5_weights(bb_re, bb_im, c_re, c_im):
    g, c, p = bb_re.shape
    tiles = g // GROUPS_PER_TILE
    eye = jnp.eye(GROUPS_PER_TILE, dtype=F32)

    def inp(w):
        w = w.reshape(tiles, GROUPS_PER_TILE, c, p)
        return jnp.einsum('jgcp,gh->jgchp', w, eye).reshape(tiles, LANES, STATE_TILE)

    def outp(w):
        w = w.reshape(tiles, GROUPS_PER_TILE, c, p)
        return jnp.einsum('jgcp,gh->jgphc', w, eye).reshape(tiles, STATE_TILE, LANES)

    wx = jnp.concatenate([inp(bb_re), inp(bb_im)], axis=-1).astype(BF16)
    wc = jnp.concatenate([outp(c_re), -outp(c_im)], axis=1).astype(BF16)
    return wx, wc


def _pack_state(h_re, h_im):
    b, g, p = h_re.shape
    tiles = g // GROUPS_PER_TILE
    st = jnp.stack([h_re.reshape(b, tiles, STATE_TILE), h_im.reshape(b, tiles, STATE_TILE)], axis=2)
    return st.reshape(b, tiles * 2 * STATE_TILE).astype(F32)


def _unpack_state(h, g, p):
    b = h.shape[0]
    st = h.reshape(b, g // GROUPS_PER_TILE, 2, STATE_TILE)
    return st[:, :, 0].reshape(b, g, p), st[:, :, 1].reshape(b, g, p)


def _s5_scan_body(u_ref, wx_ref, wc_ref, ar_ref, ai_ref, d_ref, h0_ref, y_ref, hout_ref, us_ref, ys_ref, xs_ref,
                  h_ref, *, bsz, steps, batch_on_lanes):
    tiles = wx_ref.shape[0]
    ch = tiles * LANES

    @pl.when(pl.program_id(0) == 0)
    def _():
        h_ref[...] = h0_ref[...]

    for j in range(tiles):
        if batch_on_lanes:
            for b in range(bsz):
                us_ref[j, pl.ds(b, steps, stride=bsz), :] = u_ref[:, ch * b + LANES * j:ch * b + LANES * (j + 1)]
        else:
            us_ref[j] = u_ref[:, LANES * j:LANES * (j + 1)]

    for j in range(tiles):
        xs_ref[:, 2 * STATE_TILE * j:2 * STATE_TILE * (j + 1)] = jnp.dot(
            us_ref[j].astype(BF16), wx_ref[j], preferred_element_type=F32)

    for j in range(tiles):
        re = slice(2 * STATE_TILE * j, 2 * STATE_TILE * j + STATE_TILE)
        im = slice(2 * STATE_TILE * j + STATE_TILE, 2 * STATE_TILE * (j + 1))
        a_r = jnp.broadcast_to(ar_ref[:, STATE_TILE * j:STATE_TILE * (j + 1)], (bsz, STATE_TILE))
        a_i = jnp.broadcast_to(ai_ref[:, STATE_TILE * j:STATE_TILE * (j + 1)], (bsz, STATE_TILE))

        def step(t, carry, re=re, im=im, a_r=a_r, a_i=a_i):
            hr, hi = carry
            r = pl.ds(pl.multiple_of(t * bsz, bsz), bsz)
            nhr = a_r * hr - a_i * hi + xs_ref[r, re]
            nhi = a_r * hi + a_i * hr + xs_ref[r, im]
            xs_ref[r, re] = nhr
            xs_ref[r, im] = nhi
            return nhr, nhi

        hr, hi = lax.fori_loop(0, steps, step, (h_ref[:, re], h_ref[:, im]))
        h_ref[:, re] = hr
        h_ref[:, im] = hi

    for j in range(tiles):
        cols = slice(LANES * j, LANES * (j + 1))
        hb = xs_ref[:, 2 * STATE_TILE * j:2 * STATE_TILE * (j + 1)].astype(BF16)
        y = jnp.dot(hb, wc_ref[j], preferred_element_type=F32) + d_ref[:, cols] * us_ref[j]
        if batch_on_lanes:
            ys_ref[j] = y
            for b in range(bsz):
                y_ref[:, ch * b + LANES * j:ch * b + LANES * (j + 1)] = ys_ref[j, pl.ds(b, steps, stride=bsz), :]
        else:
            y_ref[:, cols] = y
    hout_ref[...] = h_ref[...]


def _s5_scan(u, wx, wc, ar, ai, d_skip, h0, bsz, steps, batch_on_lanes):
    tiles = wx.shape[0]
    ch = tiles * LANES
    width = tiles * 2 * STATE_TILE
    rows = steps * bsz
    if batch_on_lanes:
        n_chunks = u.shape[0] // steps
        io_spec = pl.BlockSpec((steps, bsz * ch), lambda i: (i, 0))
    else:
        n_chunks = 1
        assert u.shape == (rows, ch)
        io_spec = pl.BlockSpec((rows, ch), lambda i: (0, 0))
    const = lambda shape: pl.BlockSpec(shape, lambda i: (0,) * len(shape))
    return pl.pallas_call(
        functools.partial(_s5_scan_body, bsz=bsz, steps=steps, batch_on_lanes=batch_on_lanes),
        grid=(n_chunks,),
        in_specs=[io_spec, const(wx.shape), const(wc.shape), const((1, tiles * STATE_TILE)),
                  const((1, tiles * STATE_TILE)), const((1, ch)), const((bsz, width))],
        out_specs=[io_spec, const((bsz, width))],
        out_shape=[jax.ShapeDtypeStruct(u.shape, F32), jax.ShapeDtypeStruct((bsz, width), F32)],
        scratch_shapes=[pltpu.VMEM((tiles, rows, LANES), F32), pltpu.VMEM((tiles, rows, LANES), F32),
                        pltpu.VMEM((rows, width), F32), pltpu.VMEM((bsz, width), F32)],
        compiler_params=_compiler_params(("arbitrary",)),
        name="s5_scan",
    )(u, wx, wc, ar.reshape(1, -1), ai.reshape(1, -1), d_skip.reshape(1, ch), h0)


def _post_mix_body(attn_ref, y_ref, x_ref, ga_ref, gs_ref, wglu_ref, bglu_ref, wo_ref, o_ref):
    aw = attn_ref.shape[1]
    an = _rms(attn_ref[...], ga_ref[...])
    z = jax.nn.gelu(y_ref[...])
    gate = jax.nn.sigmoid(jnp.dot(z.astype(BF16), wglu_ref[...], preferred_element_type=F32) + bglu_ref[...])
    sn = _rms(z * gate, gs_ref[...])
    o_ref[...] = (x_ref[...]
                  + jnp.dot(an.astype(BF16), wo_ref[:aw, :], preferred_element_type=F32)
                  + jnp.dot(sn.astype(BF16), wo_ref[aw:, :], preferred_element_type=F32))


def _post_mix(attn, y, x, g_attn, g_ssm, w_glu, b_glu, w_out, layer, tm, y_tiles_per_seq):
    n, d = x.shape
    aw, sw = attn.shape[1], w_glu.shape[1]
    row = lambda w: pl.BlockSpec((tm, w), lambda i: (i, 0))
    vec = lambda w: pl.BlockSpec((None, 1, w), lambda i: (layer, 0, 0))
    mat = lambda r, c: pl.BlockSpec((None, r, c), lambda i: (layer, 0, 0))
    if y_tiles_per_seq is None:
        y_spec = row(sw)
    else:
        y_spec = pl.BlockSpec((tm, sw), lambda i: (i % y_tiles_per_seq, i // y_tiles_per_seq))
    return pl.pallas_call(
        _post_mix_body,
        grid=(n // tm,),
        in_specs=[row(aw), y_spec, row(d), vec(aw), vec(sw), mat(sw, sw), vec(sw), mat(aw + sw, d)],
        out_specs=row(d),
        out_shape=jax.ShapeDtypeStruct((n, d), F32),
        compiler_params=_compiler_params(("parallel",)),
        name="post_mix",
    )(attn, y, x, g_attn, g_ssm, w_glu, b_glu, w_out)


def _finish(x_ref, acc_ref, gf_ref, o_ref):
    y = x_ref[...] + acc_ref[...]
    o_ref[...] = y if gf_ref is None else _rms(y, gf_ref[...])


def _ffn_body(x_ref, g_ref, wg_ref, wu_ref, wd_ref, *rest, final):
    gf_ref = rest[0] if final else None
    o_ref, hn_ref, acc_ref = rest[-3:]
    f = pl.program_id(1)

    @pl.when(f == 0)
    def _():
        hn_ref[...] = _rms(x_ref[...], g_ref[...]).astype(BF16)
        acc_ref[...] = jnp.zeros_like(acc_ref)

    hn = hn_ref[...]
    gate = jnp.dot(hn, wg_ref[...], preferred_element_type=F32)
    up = jnp.dot(hn, wu_ref[...], preferred_element_type=F32)
    act = (jax.nn.silu(gate) * up).astype(BF16)
    acc_ref[...] += jnp.dot(act, wd_ref[...], preferred_element_type=F32)

    @pl.when(f == pl.num_programs(1) - 1)
    def _():
        _finish(x_ref, acc_ref, gf_ref, o_ref)


def _ffn(x, g, wg, wu, wd, layer, g_final, tm, tf):
    n, d = x.shape
    ff = wg.shape[2]
    final = g_final is not None
    blk = layer // 2
    in_specs = [pl.BlockSpec((tm, d), lambda i, f: (i, 0)),
                pl.BlockSpec((None, 1, d), lambda i, f: (layer, 0, 0)),
                pl.BlockSpec((None, d, tf), lambda i, f: (blk, 0, f)),
                pl.BlockSpec((None, d, tf), lambda i, f: (blk, 0, f)),
                pl.BlockSpec((None, tf, d), lambda i, f: (blk, f, 0))]
    args = [x, g, wg, wu, wd]
    if final:
        in_specs.append(pl.BlockSpec((1, d), lambda i, f: (0, 0)))
        args.append(g_final.reshape(1, d))
    return pl.pallas_call(
        functools.partial(_ffn_body, final=final),
        grid=(n // tm, ff // tf),
        in_specs=in_specs,
        out_specs=pl.BlockSpec((tm, d), lambda i, f: (i, 0)),
        out_shape=jax.ShapeDtypeStruct((n, d), F32),
        scratch_shapes=[pltpu.VMEM((tm, d), BF16), pltpu.VMEM((tm, d), F32)],
        compiler_params=_compiler_params(("parallel", "arbitrary")),
        name="ffn_dense",
    )(*args)


def _route(hn, wr_ref, br_ref, n_experts, lane):
    logits = jnp.dot(hn, wr_ref[...], preferred_element_type=F32, precision=lax.Precision.HIGHEST) + br_ref[...]
    logits = jnp.where(lane < n_experts, logits, -jnp.inf)
    m1 = jnp.max(logits, axis=-1, keepdims=True)
    i1 = jnp.min(jnp.where(logits == m1, lane, float(LANES)), axis=-1, keepdims=True)
    rest_logits = jnp.where(lane == i1, -jnp.inf, logits)
    m2 = jnp.max(rest_logits, axis=-1, keepdims=True)
    i2 = jnp.min(jnp.where(rest_logits == m2, lane, float(LANES)), axis=-1, keepdims=True)
    e2 = jnp.exp(m2 - m1)
    return i1, i2, 1.0 / (1.0 + e2), e2 / (1.0 + e2)


def _moe_body(x_ref, g_ref, wr_ref, br_ref, wg_ref, wu_ref, wd_ref, *rest, n_experts, final):
    gf_ref = rest[0] if final else None
    o_ref, hn_ref, comb_ref, acc_ref = rest[-4:]
    e, f = pl.program_id(1), pl.program_id(2)
    lane = lax.broadcasted_iota(jnp.int32, comb_ref.shape, 1).astype(F32)

    @pl.when((e == 0) & (f == 0))
    def _():
        hn = _rms(x_ref[...], g_ref[...])
        hn_ref[...] = hn.astype(BF16)
        acc_ref[...] = jnp.zeros_like(acc_ref)
        i1, i2, g1, g2 = _route(hn, wr_ref, br_ref, n_experts, lane)
        comb_ref[...] = jnp.where(lane == i1, g1, 0.0) + jnp.where(lane == i2, g2, 0.0)

    hn = hn_ref[...]
    gate = jnp.dot(hn, wg_ref[...], preferred_element_type=F32)
    up = jnp.dot(hn, wu_ref[...], preferred_element_type=F32)
    act = (jax.nn.silu(gate) * up).astype(BF16)
    weight = jnp.sum(jnp.where(lane == e.astype(F32), comb_ref[...], 0.0), axis=-1, keepdims=True)
    acc_ref[...] += weight * jnp.dot(act, wd_ref[...], preferred_element_type=F32)

    @pl.when((e == n_experts - 1) & (f == pl.num_programs(2) - 1))
    def _():
        _finish(x_ref, acc_ref, gf_ref, o_ref)


def _moe(x, g, w_router, b_router, wg, wu, wd, layer, g_final, tm, tf):
    n, d = x.shape
    _, n_experts, _, ff = wg.shape
    final = g_final is not None
    blk = layer // 2
    wr = jnp.zeros((d, LANES), F32).at[:, :n_experts].set(w_router.astype(F32))
    br = jnp.zeros((1, LANES), F32).at[0, :n_experts].set(b_router.astype(F32))
    in_specs = [pl.BlockSpec((tm, d), lambda i, e, f: (i, 0)),
                pl.BlockSpec((None, 1, d), lambda i, e, f: (layer, 0, 0)),
                pl.BlockSpec((d, LANES), lambda i, e, f: (0, 0)),
                pl.BlockSpec((1, LANES), lambda i, e, f: (0, 0)),
                pl.BlockSpec((None, None, d, tf), lambda i, e, f: (blk, e, 0, f)),
                pl.BlockSpec((None, None, d, tf), lambda i, e, f: (blk, e, 0, f)),
                pl.BlockSpec((None, None, tf, d), lambda i, e, f: (blk, e, f, 0))]
    args = [x, g, wr, br, wg, wu, wd]
    if final:
        in_specs.append(pl.BlockSpec((1, d), lambda i, e, f: (0, 0)))
        args.append(g_final.reshape(1, d))
    return pl.pallas_call(
        functools.partial(_moe_body, n_experts=n_experts, final=final),
        grid=(n // tm, n_experts, ff // tf),
        in_specs=in_specs,
        out_specs=pl.BlockSpec((tm, d), lambda i, e, f: (i, 0)),
        out_shape=jax.ShapeDtypeStruct((n, d), F32),
        scratch_shapes=[pltpu.VMEM((tm, d), BF16), pltpu.VMEM((tm, LANES), F32), pltpu.VMEM((tm, d), F32)],
        compiler_params=_compiler_params(("parallel", "arbitrary", "arbitrary")),
        name="moe",
    )(*args)


EXPERT_ROW_TILE = 512
MOE_DMA_ROWS = 256


def _router_body(x_ref, g_ref, wr_ref, br_ref, hn_ref, route_ref, *, n_experts):
    lane = lax.broadcasted_iota(jnp.int32, route_ref.shape, 1).astype(F32)
    hn = _rms(x_ref[...], g_ref[...])
    hn_ref[...] = hn
    i1, i2, g1, g2 = _route(hn, wr_ref, br_ref, n_experts, lane)
    route_ref[...] = jnp.where(lane == 0.0, i1, jnp.where(lane == 1.0, i2, jnp.where(
        lane == 2.0, g1, jnp.where(lane == 3.0, g2, 0.0))))


def _router(x, g, wr, br, layer, n_experts, tm):
    n, d = x.shape
    return pl.pallas_call(
        functools.partial(_router_body, n_experts=n_experts),
        grid=(n // tm,),
        in_specs=[pl.BlockSpec((tm, d), lambda i: (i, 0)),
                  pl.BlockSpec((None, 1, d), lambda i: (layer, 0, 0)),
                  pl.BlockSpec((d, LANES), lambda i: (0, 0)),
                  pl.BlockSpec((1, LANES), lambda i: (0, 0))],
        out_specs=[pl.BlockSpec((tm, d), lambda i: (i, 0)), pl.BlockSpec((tm, LANES), lambda i: (i, 0))],
        out_shape=[jax.ShapeDtypeStruct((n, d), F32), jax.ShapeDtypeStruct((n, LANES), F32)],
        compiler_params=_compiler_params(("parallel",)),
        name="moe_router",
    )(x, g, wr, br)


def _dispatch_plan(experts, n_experts, n_tiles):
    flat = experts.reshape(-1)
    onehot = (flat[:, None] == jnp.arange(n_experts, dtype=jnp.int32)[None, :]).astype(jnp.int32)
    running = jnp.cumsum(onehot, axis=0)
    rank = jnp.sum(onehot * running, axis=1) - 1
    tiles_per_expert = (running[-1] + EXPERT_ROW_TILE - 1) // EXPERT_ROW_TILE
    tile_end = jnp.cumsum(tiles_per_expert)
    group_start = (tile_end - tiles_per_expert) * EXPERT_ROW_TILE
    slot = jnp.sum(onehot * group_start[None, :], axis=1) + rank
    tile = jnp.arange(n_tiles, dtype=jnp.int32)
    tile_expert = jnp.minimum(jnp.sum((tile[:, None] >= tile_end[None, :]).astype(jnp.int32), axis=1),
                              n_experts - 1)
    tile_valid = (tile < tile_end[-1]).astype(jnp.int32)
    return slot.astype(jnp.int32), tile_expert.astype(jnp.int32), tile_valid


def _row_copy(src_ref, src_row, dst_ref, dst_row, sem):
    return pltpu.make_async_copy(src_ref.at[pl.ds(src_row, 1)], dst_ref.at[pl.ds(dst_row, 1)], sem)


def _slot_tokens_body(slot_ref, tok_ref):
    def clear(i, c):
        for k in range(8):
            tok_ref[8 * i + k] = 0
        return c

    def fill(i, c):
        for k in range(8):
            tok_ref[slot_ref[8 * i + k]] = 4 * i + k // 2
        return c

    lax.fori_loop(0, tok_ref.shape[0] // 8, clear, 0)
    lax.fori_loop(0, slot_ref.shape[0] // 8, fill, 0)


def _slot_tokens(slot, n_slots):
    smem = pl.BlockSpec(memory_space=pltpu.SMEM)
    return pl.pallas_call(
        _slot_tokens_body,
        in_specs=[smem],
        out_specs=smem,
        out_shape=jax.ShapeDtypeStruct((n_slots,), jnp.int32),
        compiler_params=pltpu.CompilerParams(disable_bounds_checks=True),
        name="moe_slot_tokens",
    )(slot)


def _experts_body(tok_ref, te_ref, tv_ref, hn_ref, wg_ref, wu_ref, wd_ref, y_ref, xbuf_ref, acc_ref, sem, *, tf):
    t = pl.program_id(0)
    last = pl.num_programs(0) - 1
    rows = y_ref.shape[0]
    n_f = wg_ref.shape[1] // tf
    cur = t % 2

    def copy(tile, buf, j):
        return _row_copy(hn_ref, tok_ref[tile * rows + j], xbuf_ref.at[buf], j, sem.at[buf])

    def start_all(tile, buf):
        def body(j, c):
            copy(tile, buf, j).start()
            return c
        lax.fori_loop(0, rows, body, 0)

    def wait_all(buf):
        pltpu.make_async_copy(hn_ref.at[pl.ds(0, rows)], xbuf_ref.at[buf], sem.at[buf]).wait()

    @pl.when(t == 0)
    def _():
        start_all(0, 0)

    @pl.when((t == 0) | (tv_ref[jnp.maximum(t - 1, 0)] != 0))
    def _():
        wait_all(cur)

    live = tv_ref[t] != 0

    @pl.when(live)
    def _():
        nxt = jnp.minimum(t + 1, last)
        xb = xbuf_ref[cur].astype(BF16)
        for f in range(n_f):
            cols = slice(f * tf, (f + 1) * tf)
            gate = jnp.dot(xb, wg_ref[:, cols], preferred_element_type=F32)
            up = jnp.dot(xb, wu_ref[:, cols], preferred_element_type=F32)
            act = (jax.nn.silu(gate) * up).astype(BF16)
            part = jnp.dot(act, wd_ref[cols, :], preferred_element_type=F32)
            if f == 0:
                acc_ref[...] = part
            else:
                acc_ref[...] += part
            for j in range(f * rows // n_f, (f + 1) * rows // n_f):
                copy(nxt, 1 - cur, j).start()
        y_ref[...] = acc_ref[...]

    @pl.when(jnp.logical_not(live))
    def _():
        y_ref[...] = jnp.zeros_like(y_ref)

    @pl.when((t == last) & live)
    def _():
        wait_all(1 - cur)


def _experts(tok, tile_expert, tile_valid, hn, wg, wu, wd, layer, tf):
    d = hn.shape[1]
    ff = wg.shape[3]
    blk = layer // 2
    n_tiles = tile_expert.shape[0]
    weights = lambda r, c: pl.BlockSpec((None, None, r, c), lambda t, tok, te, tv: (blk, te[t], 0, 0))
    return pl.pallas_call(
        functools.partial(_experts_body, tf=tf),
        grid_spec=pltpu.PrefetchScalarGridSpec(
            num_scalar_prefetch=3,
            grid=(n_tiles,),
            in_specs=[pl.BlockSpec(memory_space=pl.ANY), weights(d, ff), weights(d, ff), weights(ff, d)],
            out_specs=pl.BlockSpec((EXPERT_ROW_TILE, d), lambda t, tok, te, tv: (t, 0)),
            scratch_shapes=[pltpu.VMEM((2, EXPERT_ROW_TILE, d), F32), pltpu.VMEM((EXPERT_ROW_TILE, d), F32),
                            pltpu.SemaphoreType.DMA((2,))]),
        out_shape=jax.ShapeDtypeStruct((n_tiles * EXPERT_ROW_TILE, d), F32),
        compiler_params=pltpu.CompilerParams(dimension_semantics=("arbitrary",), disable_bounds_checks=True,
                                             vmem_limit_bytes=VMEM_LIMIT_BYTES),
        name="moe_experts",
    )(tok, tile_expert, tile_valid, hn, wg, wu, wd)


def _combine_body(slot_ref, x_ref, route_ref, y_ref, *rest, final):
    gf_ref = rest[0] if final else None
    o_ref, ya_ref, yb_ref, sem = rest[-4:]
    base = pl.program_id(0) * MOE_DMA_ROWS

    def copies(j):
        t = base + j
        return [_row_copy(y_ref, slot_ref[2 * t], ya_ref, j, sem.at[0]),
                _row_copy(y_ref, slot_ref[2 * t + 1], yb_ref, j, sem.at[1])]

    def start(j, c):
        for cp in copies(j):
            cp.start()
        return c

    lax.fori_loop(0, MOE_DMA_ROWS, start, 0, unroll=4)
    for buf_ref, s in ((ya_ref, sem.at[0]), (yb_ref, sem.at[1])):
        pltpu.make_async_copy(y_ref.at[pl.ds(0, MOE_DMA_ROWS)], buf_ref, s).wait()
    y = x_ref[...] + route_ref[:, 2:3] * ya_ref[...] + route_ref[:, 3:4] * yb_ref[...]
    o_ref[...] = y if gf_ref is None else _rms(y, gf_ref[...])


def _combine(slot, x, route, y, g_final):
    n, d = x.shape
    final = g_final is not None
    in_specs = [pl.BlockSpec((MOE_DMA_ROWS, d), lambda i, s: (i, 0)),
                pl.BlockSpec((MOE_DMA_ROWS, LANES), lambda i, s: (i, 0)),
                pl.BlockSpec(memory_space=pl.ANY)]
    args = [slot, x, route, y]
    if final:
        in_specs.append(pl.BlockSpec((1, d), lambda i, s: (0, 0)))
        args.append(g_final.reshape(1, d))
    return pl.pallas_call(
        functools.partial(_combine_body, final=final),
        grid_spec=pltpu.PrefetchScalarGridSpec(
            num_scalar_prefetch=1,
            grid=(n // MOE_DMA_ROWS,),
            in_specs=in_specs,
            out_specs=pl.BlockSpec((MOE_DMA_ROWS, d), lambda i, s: (i, 0)),
            scratch_shapes=[pltpu.VMEM((MOE_DMA_ROWS, d), F32), pltpu.VMEM((MOE_DMA_ROWS, d), F32),
                            pltpu.SemaphoreType.DMA((2,))]),
        out_shape=jax.ShapeDtypeStruct((n, d), F32),
        compiler_params=pltpu.CompilerParams(dimension_semantics=("arbitrary",), disable_bounds_checks=True,
                                             vmem_limit_bytes=VMEM_LIMIT_BYTES),
        name="moe_combine",
    )(*args)


def _moe_sorted(x, g, w_router, b_router, wg, wu, wd, layer, g_final, tm, tf):
    n, d = x.shape
    n_experts = wg.shape[1]
    wr = jnp.zeros((d, LANES), F32).at[:, :n_experts].set(w_router.astype(F32))
    br = jnp.zeros((1, LANES), F32).at[0, :n_experts].set(b_router.astype(F32))
    hn, route = _router(x, g, wr, br, layer, n_experts, tm)
    n_tiles = 2 * n // EXPERT_ROW_TILE + n_experts
    slot, tile_expert, tile_valid = _dispatch_plan(route[:, :2].astype(jnp.int32), n_experts, n_tiles)
    tok = _slot_tokens(slot, n_tiles * EXPERT_ROW_TILE)
    y = _experts(tok, tile_expert, tile_valid, hn, wg, wu, wd, layer, tf)
    return _combine(slot, x, route, y, g_final)


PROMPT_ROW_TILE = 512
PROMPT_SCAN_STEPS = 128


def kernel(x_prompt, x_sample, cache_win_k, cache_win_v, state_ssm_re, state_ssm_im, rel_bias, norm_mix, w_in, ssm_a_re, ssm_a_im, ssm_log_dt, ssm_b_re, ssm_b_im, ssm_c_re, ssm_c_im, ssm_d, ssm_w_glu, ssm_b_glu, norm_attn_out, norm_ssm_out, w_out, norm_ffn, ffn_w_gate, ffn_w_up, ffn_w_down, moe_w_router, moe_b_router, moe_w_gate, moe_w_up, moe_w_down, norm_final):
    batch, seq, d_model = x_prompt.shape
    dec_batch, dec_seq, _ = x_sample.shape
    depth = w_in.shape[0]
    win, n_heads = cache_win_k.shape[2:4]
    attn_width = n_heads * HEAD_DIM
    n_groups, n_state = ssm_a_re.shape[1:]
    ssm_width = n_groups * SSM_GROUP
    assert w_in.shape[2] == 3 * attn_width + ssm_width and seq == MAX_WINDOW
    dec_rows = dec_batch * dec_seq

    band_bias = _band_bias(rel_bias)
    sample_tabs = _sample_tables(rel_bias, win, dec_seq)
    ar, ai, bb_re, bb_im = _s5_params(ssm_a_re, ssm_a_im, ssm_log_dt, ssm_b_re, ssm_b_im)
    cache_k = jnp.transpose(cache_win_k, (0, 1, 3, 4, 2))
    cache_v = jnp.transpose(cache_win_v, (0, 1, 3, 4, 2))

    rows3 = lambda a: a.reshape(a.shape[0], 1, a.shape[1])
    norm_mix3, norm_ffn3, b_glu3 = rows3(norm_mix), rows3(norm_ffn), rows3(ssm_b_glu)
    g_attn3, g_ssm3 = rows3(norm_attn_out), rows3(norm_ssm_out)
    w_in_b, w_out_b, w_glu_b = w_in.astype(BF16), w_out.astype(BF16), ssm_w_glu.astype(BF16)
    ffn_b = [w.astype(BF16) for w in (ffn_w_gate, ffn_w_up, ffn_w_down)]
    moe_b = [w.astype(BF16) for w in (moe_w_gate, moe_w_up, moe_w_down)]
    tf = ffn_w_gate.shape[2] // 2

    def mix_and_ffn(x, l, attn, y, tm, y_tiles_per_seq):
        g_final = norm_final if l == depth - 1 else None
        x = _post_mix(attn, y, x, g_attn3, g_ssm3, w_glu_b, b_glu3, w_out_b, l, tm, y_tiles_per_seq)
        if l % 2 == 0:
            return _ffn(x, norm_ffn3, *ffn_b, l, g_final, tm, tf)
        n_experts = moe_w_gate.shape[1]
        moe = _moe_sorted if 2 * x.shape[0] >= n_experts * EXPERT_ROW_TILE else _moe
        return moe(x, norm_ffn3, moe_w_router[l // 2], moe_b_router[l // 2], *moe_b, l, g_final, tm, tf)

    xp = x_prompt.reshape(batch * seq, d_model)
    xs = x_sample.reshape(dec_rows, d_model)
    zero_state = jnp.zeros((batch, 2 * n_groups * n_state), F32)
    outs = [[] for _ in range(8)]
    for l in range(depth):
        wx, wc = _pack_s5_weights(bb_re[l], bb_im[l], ssm_c_re[l].astype(F32), ssm_c_im[l].astype(F32))

        qkv, k_t, v_t, u = _norm_proj_prompt(xp, norm_mix3, w_in_b, l, batch, seq, attn_width, PROMPT_ROW_TILE)
        attn = _prompt_attn(qkv, band_bias, batch, seq, attn_width)
        y, h_fin = _s5_scan(u, wx, wc, ar[l], ai[l], ssm_d[l], zero_state, batch, PROMPT_SCAN_STEPS, True)
        xp = mix_and_ffn(xp, l, attn, y, PROMPT_ROW_TILE, seq // PROMPT_ROW_TILE)
        outs[0].append(k_t)
        outs[1].append(v_t)
        h_re, h_im = _unpack_state(h_fin, n_groups, n_state)
        outs[2].append(h_re)
        outs[3].append(h_im)

        proj = _norm_proj(xs, norm_mix3, w_in_b, l, dec_rows)
        q, k_new, v_new = (proj[:, i * attn_width:(i + 1) * attn_width].reshape(dec_batch, dec_seq, n_heads, HEAD_DIM)
                           for i in range(3))
        attn = _sample_attn(q, k_new, v_new, cache_k, cache_v, l, sample_tabs).reshape(dec_rows, attn_width)
        u = proj[:, 3 * attn_width:].reshape(dec_batch, dec_seq, ssm_width).transpose(1, 0, 2)
        y, h_fin = _s5_scan(u.reshape(dec_rows, ssm_width), wx, wc, ar[l], ai[l], ssm_d[l],
                            _pack_state(state_ssm_re[l], state_ssm_im[l]), dec_batch, dec_seq, False)
        y = y.reshape(dec_seq, dec_batch, ssm_width).transpose(1, 0, 2).reshape(dec_rows, ssm_width)
        xs = mix_and_ffn(xs, l, attn, y, dec_rows, None)
        outs[4].append(k_new)
        outs[5].append(v_new)
        h_re, h_im = _unpack_state(h_fin, n_groups, n_state)
        outs[6].append(h_re)
        outs[7].append(h_im)

    def window_out(parts):
        a = jnp.stack(parts, 0).reshape(depth, batch, n_heads, HEAD_DIM, seq)
        return jnp.transpose(a, (0, 1, 4, 2, 3))

    return (xp.reshape(batch, seq, d_model), xs.reshape(dec_batch, dec_seq, d_model),
            window_out(outs[0]), window_out(outs[1]), *[jnp.stack(o, 0) for o in outs[2:]])
```

```python
import functools

import numpy as np
import jax
import jax.numpy as jnp
from jax import lax
from jax.experimental import pallas as pl
from jax.experimental.pallas import tpu as pltpu

F32 = jnp.float32
BF16 = jnp.bfloat16

HEAD_DIM = 64
DILATION_PATTERNS = ((128, 1), (512, 4), (2048, 16))
MAX_WINDOW = 2048
N_BUCKETS = 32
SSM_GROUP = 16
SSM_STATE = 64
RMS_EPS = 1e-6
NEG_BIG = -1e30
LOG2_E = 1.4426950408889634

LANES = 128
Q_TILE = 128
ATTN_UNROLL = 4
GROUPS_PER_TILE = LANES // SSM_GROUP
STATE_TILE = GROUPS_PER_TILE * SSM_STATE
VMEM_LIMIT_BYTES = 56 * 1024 * 1024


def _compiler_params(semantics):
    return pltpu.CompilerParams(dimension_semantics=semantics, vmem_limit_bytes=VMEM_LIMIT_BYTES)


def _rms(x, g):
    return x * lax.rsqrt(jnp.mean(x * x, axis=-1, keepdims=True) + RMS_EPS) * g


def _t5_causal_bucket(dist):
    max_exact = N_BUCKETS // 2
    d = np.asarray(dist, dtype=np.int64)
    large = max_exact + (np.log(np.maximum(d, max_exact) / max_exact)
                         / np.log(MAX_WINDOW / max_exact)
                         * (N_BUCKETS - max_exact)).astype(np.int64)
    large = np.minimum(large, N_BUCKETS - 1)
    return np.where(d < max_exact, d, large).astype(np.int32)


def _multiplicity(d):
    m = np.zeros(d.shape, np.int32)
    for window, dil in DILATION_PATTERNS:
        m += ((d >= 0) & (d <= window) & (d % dil == 0)).astype(np.int32)
    return m


def _norm_proj_body(x_ref, g_ref, w_ref, o_ref):
    hn = _rms(x_ref[...], g_ref[...])
    o_ref[...] = jnp.dot(hn.astype(BF16), w_ref[...], preferred_element_type=F32)


def _norm_proj(x, g, w, layer, tm):
    n, d = x.shape
    c = w.shape[2]
    return pl.pallas_call(
        _norm_proj_body,
        grid=(n // tm,),
        in_specs=[pl.BlockSpec((tm, d), lambda i: (i, 0)),
                  pl.BlockSpec((None, 1, d), lambda i: (layer, 0, 0)),
                  pl.BlockSpec((None, d, c), lambda i: (layer, 0, 0))],
        out_specs=pl.BlockSpec((tm, c), lambda i: (i, 0)),
        out_shape=jax.ShapeDtypeStruct((n, c), F32),
        compiler_params=_compiler_params(("parallel",)),
        name="norm_proj",
    )(x, g, w)


def _norm_proj_prompt_body(x_ref, g_ref, w_ref, _k_all, _v_all, qkv_ref, kt_ref, vt_ref, u_ref, *, attn_width):
    hn = _rms(x_ref[...], g_ref[...])
    proj = jnp.dot(hn.astype(BF16), w_ref[...], preferred_element_type=F32)
    qkv_ref[...] = proj[:, :3 * attn_width]
    kt_ref[...] = proj[:, attn_width:2 * attn_width].T
    vt_ref[...] = proj[:, 2 * attn_width:3 * attn_width].T
    u_ref[...] = proj[:, 3 * attn_width:]


def _norm_proj_prompt(x, g, w, layer, k_all, v_all, tm):
    n, d = x.shape
    c = w.shape[2]
    _, batch, attn_width, seq = k_all.shape
    ch = c - 3 * attn_width
    per_seq = seq // tm
    window = pl.BlockSpec((None, None, attn_width, tm), lambda i: (layer, i // per_seq, 0, i % per_seq))
    untouched = pl.BlockSpec(memory_space=pl.ANY)
    return pl.pallas_call(
        functools.partial(_norm_proj_prompt_body, attn_width=attn_width),
        grid=(n // tm,),
        in_specs=[pl.BlockSpec((tm, d), lambda i: (i, 0)),
                  pl.BlockSpec((None, 1, d), lambda i: (layer, 0, 0)),
                  pl.BlockSpec((None, d, c), lambda i: (layer, 0, 0)),
                  untouched, untouched],
        out_specs=[pl.BlockSpec((tm, 3 * attn_width), lambda i: (i, 0)), window, window,
                   pl.BlockSpec((tm, ch), lambda i: (i % per_seq, i // per_seq))],
        out_shape=[jax.ShapeDtypeStruct((n, 3 * attn_width), F32),
                   jax.ShapeDtypeStruct(k_all.shape, F32),
                   jax.ShapeDtypeStruct(v_all.shape, F32),
                   jax.ShapeDtypeStruct((seq, batch * ch), F32)],
        input_output_aliases={3: 1, 4: 2},
        compiler_params=_compiler_params(("parallel",)),
        name="norm_proj_prompt",
    )(x, g, w, k_all, v_all)


def _band_bias(rel_bias):
    period = 3 * Q_TILE
    n_heads = rel_bias.shape[1]
    rows0 = []
    for window, dil in DILATION_PATTERNS:
        assert window // dil == Q_TILE
        vec = rel_bias[_t5_causal_bucket((Q_TILE - np.arange(Q_TILE + 1)) * dil)].astype(F32)
        rows0.append(jnp.concatenate([vec.T, jnp.full((n_heads, period - Q_TILE - 1), NEG_BIG, F32)], axis=1))
    row0 = jnp.stack(rows0, 0)
    flat = jnp.tile(row0, (1, 1, Q_TILE))[:, :, :Q_TILE * (period - 1)]
    band = flat.reshape(len(DILATION_PATTERNS), n_heads, Q_TILE, period - 1)[:, :, :, :2 * Q_TILE]
    return band * LOG2_E


def _prompt_attn_body(q_ref, k_ref, v_ref, bias_ref, o_ref, op_ref, m_ref, l_ref):
    seq = q_ref.shape[0]
    lane = lax.broadcasted_iota(jnp.int32, (Q_TILE, LANES), 1)
    head0 = lane < HEAD_DIM
    scale = HEAD_DIM ** -0.5 * LOG2_E

    def rows(start, size, dil):
        return pl.ds(start, size) if dil == 1 else pl.ds(start, size, stride=dil)

    def tiles(p, dil, starts, first):
        nk = Q_TILE if first else 2 * Q_TILE
        bias = (bias_ref[p, :, :, Q_TILE:] if first else bias_ref[p]).reshape(2 * Q_TILE, nk)
        work = []
        for q_start, k_start in starts:
            qsl, ksl = rows(q_start, Q_TILE, dil), rows(k_start, nk, dil)
            qt = q_ref[qsl, :] * scale
            q2 = jnp.concatenate([jnp.where(head0, qt, 0.0), jnp.where(head0, 0.0, qt)], axis=0).astype(BF16)
            work.append([qsl, v_ref[ksl, :].astype(BF16), q2, k_ref[ksl, :].astype(BF16)])
        for w in work:
            w[2] = lax.dot_general(w[2], w[3], (((1,), (1,)), ((), ())), preferred_element_type=F32) + bias
        for w in work:
            w[3] = jnp.max(w[2], axis=-1, keepdims=True)
        for w in work:
            w[2] = jnp.exp2(w[2] - w[3])
        for w in work:
            w.append(jnp.sum(w[2], axis=-1, keepdims=True))
            w[2] = jnp.dot(w[2].astype(BF16), w[1], preferred_element_type=F32)
        for qsl, _, o, m, l in work:
            op_ref[p, qsl, :] = jnp.where(head0, o[:Q_TILE], o[Q_TILE:])
            m_ref[p, qsl, :] = jnp.where(head0, m[:Q_TILE], m[Q_TILE:])
            l_ref[p, qsl, :] = jnp.where(head0, l[:Q_TILE], l[Q_TILE:])

    for p, (_, dil) in enumerate(DILATION_PATTERNS):
        n_tiles = seq // dil // Q_TILE
        stride = dil * Q_TILE
        unroll = min(ATTN_UNROLL, dil)
        if dil == 1:
            unroll = max(u for u in range(1, 2 * ATTN_UNROLL + 1) if (n_tiles - 1) % u == 0)
            for r in range(dil):
                tiles(p, dil, [(r, r)], True)

                def blk(t, c, p=p, dil=dil, r=r, stride=stride, unroll=unroll):
                    first_t = 1 + t * unroll
                    tiles(p, dil, [(r + stride * (first_t + j), r + stride * (first_t + j - 1))
                                   for j in range(unroll)], False)
                    return c
                lax.fori_loop(0, (n_tiles - 1) // unroll, blk, 0)
        else:
            assert dil % unroll == 0

            def classes(g, c, p=p, dil=dil, n_tiles=n_tiles, stride=stride, unroll=unroll):
                rs = [g * unroll + j for j in range(unroll)]
                tiles(p, dil, [(r, r) for r in rs], True)
                if n_tiles > 1:
                    def blk(t, c2):
                        tiles(p, dil, [(r + stride * t, r + stride * (t - 1)) for r in rs], False)
                        return c2
                    lax.fori_loop(1, n_tiles, blk, 0)
                return c
            lax.fori_loop(0, dil // unroll, classes, 0)

    def combine(t, carry):
        r = pl.ds(pl.multiple_of(t * Q_TILE, Q_TILE), Q_TILE)
        ms = [m_ref[p, r, :] for p in range(len(DILATION_PATTERNS))]
        mx = functools.reduce(jnp.maximum, ms)
        ws = [jnp.exp2(m - mx) for m in ms]
        num = sum(w * op_ref[p, r, :] for p, w in enumerate(ws))
        den = sum(w * l_ref[p, r, :] for p, w in enumerate(ws))
        o_ref[r, :] = num / den
        return carry

    lax.fori_loop(0, seq // Q_TILE, combine, 0)


def _prompt_attn(proj, bias, batch, seq, attn_width):
    n_pairs = attn_width // LANES
    n_pat = len(DILATION_PATTERNS)
    assert seq % (Q_TILE * max(d for _, d in DILATION_PATTERNS)) == 0
    return pl.pallas_call(
        _prompt_attn_body,
        grid=(batch, n_pairs),
        in_specs=[pl.BlockSpec((seq, LANES), lambda b, h: (b, h)),
                  pl.BlockSpec((seq, LANES), lambda b, h: (b, n_pairs + h)),
                  pl.BlockSpec((seq, LANES), lambda b, h: (b, 2 * n_pairs + h)),
                  pl.BlockSpec((n_pat, 2, Q_TILE, 2 * Q_TILE), lambda b, h: (0, h, 0, 0))],
        out_specs=pl.BlockSpec((seq, LANES), lambda b, h: (b, h)),
        out_shape=jax.ShapeDtypeStruct((batch * seq, attn_width), F32),
        scratch_shapes=[pltpu.VMEM((n_pat, seq, LANES), F32)] * 3,
        compiler_params=_compiler_params(("parallel", "parallel")),
        name="prompt_attn",
    )(proj, proj, proj, bias)


def _sample_tables(rel_bias, win, n_new):
    n_heads = rel_bias.shape[1]
    vec_t = rel_bias[_t5_causal_bucket(np.arange(win + n_new))].astype(F32).T
    flipped = vec_t[:, ::-1]
    t = np.arange(n_new)
    d_cache = win + t[:, None] - np.arange(win)[None, :]
    d_new = t[:, None] - t[None, :]
    b_cache = jnp.stack([flipped[:, n_new - 1 - i:n_new - 1 - i + win] for i in range(n_new)], 0)
    b_new = jnp.transpose(vec_t[:, np.maximum(d_new, 0)], (1, 0, 2))
    out = []
    for b, d in ((b_cache, d_cache), (b_new, d_new)):
        mult = _multiplicity(d)
        b = jnp.where((mult > 0)[:, None, :], b, NEG_BIG).reshape(n_new * n_heads, d.shape[1])
        m = np.broadcast_to(np.maximum(mult, 1)[:, None, :], (n_new, n_heads, d.shape[1]))
        out += [b, jnp.asarray(m.reshape(n_new * n_heads, d.shape[1]), F32)]
    return out


def _sample_attn_body(q_ref, kc_ref, kn_ref, vc_ref, vn_ref, bc_ref, mc_ref, bn_ref, mn_ref, hm_ref, o_ref):
    n_new = o_ref.shape[0]
    win = kc_ref.shape[-1]
    qb = (q_ref[...] * HEAD_DIM ** -0.5).astype(BF16)
    s_c = jnp.dot(qb, kc_ref[...].reshape(-1, win).astype(BF16), preferred_element_type=F32) + bc_ref[...]
    s_n = jnp.dot(qb, kn_ref[...].astype(BF16), preferred_element_type=F32) + bn_ref[...]
    m = jnp.maximum(jnp.max(s_c, axis=-1, keepdims=True), jnp.max(s_n, axis=-1, keepdims=True))
    e_c = jnp.exp(s_c - m) * mc_ref[...]
    e_n = jnp.exp(s_n - m) * mn_ref[...]
    den = jnp.sum(e_c, axis=-1, keepdims=True) + jnp.sum(e_n, axis=-1, keepdims=True)
    nt = (((1,), (1,)), ((), ()))
    o = (lax.dot_general(e_c.astype(BF16), vc_ref[...].reshape(-1, win).astype(BF16), nt,
                         preferred_element_type=F32)
         + lax.dot_general(e_n.astype(BF16), vn_ref[...].astype(BF16), nt, preferred_element_type=F32))
    o = o / den * hm_ref[...]
    o_ref[...] = jnp.sum(o.reshape(n_new, -1, o.shape[-1]), axis=1)


def _sample_attn(q, k_new, v_new, cache_k, cache_v, layer, tables):
    depth, batch, n_heads, hd, win = cache_k.shape
    n_new = q.shape[1]
    width = n_heads * hd
    rows = n_new * n_heads
    eye = jnp.eye(n_heads, dtype=F32)
    q_bd = jnp.einsum('bthd,hg->btghd', q, eye).reshape(batch, rows, width)
    kn = jnp.transpose(k_new, (0, 2, 3, 1)).reshape(batch, width, n_new)
    vn = jnp.transpose(v_new, (0, 2, 3, 1)).reshape(batch, width, n_new)
    head_mask = jnp.asarray(np.kron(np.tile(np.eye(n_heads), (n_new, 1)), np.ones((1, hd))), F32)
    cache_spec = pl.BlockSpec((None, None, n_heads, hd, win), lambda b: (layer, b, 0, 0, 0))
    new_spec = pl.BlockSpec((None, width, n_new), lambda b: (b, 0, 0))
    const = lambda a: pl.BlockSpec(a.shape, lambda b: (0, 0))
    return pl.pallas_call(
        _sample_attn_body,
        grid=(batch,),
        in_specs=[pl.BlockSpec((None, rows, width), lambda b: (b, 0, 0)),
                  cache_spec, new_spec, cache_spec, new_spec] + [const(t) for t in tables] + [const(head_mask)],
        out_specs=pl.BlockSpec((None, n_new, width), lambda b: (b, 0, 0)),
        out_shape=jax.ShapeDtypeStruct((batch, n_new, width), F32),
        compiler_params=_compiler_params(("parallel",)),
        name="sample_attn",
    )(q_bd, cache_k, kn, cache_v, vn, *tables, head_mask)


def _s5_param_body(are_ref, aim_ref, ldt_ref, bre_ref, bim_ref, ar_ref, ai_ref, bbre_ref, bbim_ref):
    a_re, a_im = are_ref[...], aim_ref[...]
    dt = jnp.exp(ldt_ref[...])
    mag = jnp.exp(dt * a_re)
    ar, ai = mag * jnp.cos(dt * a_im), mag * jnp.sin(dt * a_im)
    nr, ni = ar - 1.0, ai
    den = a_re * a_re + a_im * a_im
    fr = (nr * a_re + ni * a_im) / den
    fi = (ni * a_re - nr * a_im) / den
    br, bi = bre_ref[...], bim_ref[...]
    ar_ref[...] = ar
    ai_ref[...] = ai
    bbre_ref[...] = fr[:, None, :] * br - fi[:, None, :] * bi
    bbim_ref[...] = fr[:, None, :] * bi + fi[:, None, :] * br


def _s5_params(a_re, a_im, log_dt, b_re, b_im):
    depth, g, p = a_re.shape
    c = b_re.shape[-1]
    n = depth * g
    full = lambda shape: pl.BlockSpec(shape, lambda: (0,) * len(shape))
    ar, ai, bbre, bbim = pl.pallas_call(
        _s5_param_body,
        in_specs=[full((n, p)), full((n, p)), full((n, 1)), full((n, c, p)), full((n, c, p))],
        out_specs=[full((n, p)), full((n, p)), full((n, c, p)), full((n, c, p))],
        out_shape=[jax.ShapeDtypeStruct((n, p), F32), jax.ShapeDtypeStruct((n, p), F32),
                   jax.ShapeDtypeStruct((n, c, p), F32), jax.ShapeDtypeStruct((n, c, p), F32)],
        name="s5_params",
    )(a_re.reshape(n, p), a_im.reshape(n, p), log_dt.reshape(n, 1),
      jnp.swapaxes(b_re, -1, -2).reshape(n, c, p), jnp.swapaxes(b_im, -1, -2).reshape(n, c, p))
    return (ar.reshape(depth, g, p), ai.reshape(depth, g, p),
            bbre.reshape(depth, g, c, p), bbim.reshape(depth, g, c, p))


def _pack_s5_weights(bb_re, bb_im, c_re, c_im):
    g, c, p = bb_re.shape
    tiles = g // GROUPS_PER_TILE
    eye = jnp.eye(GROUPS_PER_TILE, dtype=F32)

    def inp(w):
        w = w.reshape(tiles, GROUPS_PER_TILE, c, p)
        return jnp.einsum('jgcp,gh->jgchp', w, eye).reshape(tiles, LANES, STATE_TILE)

    def outp(w):
        w = w.reshape(tiles, GROUPS_PER_TILE, c, p)
        return jnp.einsum('jgcp,gh->jgphc', w, eye).reshape(tiles, STATE_TILE, LANES)

    wx = jnp.concatenate([inp(bb_re), inp(bb_im)], axis=-1).astype(BF16)
    wc = jnp.concatenate([outp(c_re), -outp(c_im)], axis=1).astype(BF16)
    return wx, wc


def _pack_state(h_re, h_im):
    b, g, p = h_re.shape
    tiles = g // GROUPS_PER_TILE
    st = jnp.stack([h_re.reshape(b, tiles, STATE_TILE), h_im.reshape(b, tiles, STATE_TILE)], axis=2)
    return st.reshape(b, tiles * 2 * STATE_TILE).astype(F32)


def _unpack_state(h, g, p):
    b = h.shape[0]
    st = h.reshape(b, g // GROUPS_PER_TILE, 2, STATE_TILE)
    return st[:, :, 0].reshape(b, g, p), st[:, :, 1].reshape(b, g, p)


def _s5_scan_body(u_ref, wx_ref, wc_ref, ar_ref, ai_ref, d_ref, h0_ref, y_ref, hout_ref, us_ref, ys_ref, xs_ref,
                  h_ref, *, bsz, steps, batch_on_lanes):
    tiles = wx_ref.shape[0]
    ch = tiles * LANES

    @pl.when(pl.program_id(0) == 0)
    def _():
        h_ref[...] = h0_ref[...]

    for j in range(tiles):
        if batch_on_lanes:
            for b in range(bsz):
                us_ref[j, pl.ds(b, steps, stride=bsz), :] = u_ref[:, ch * b + LANES * j:ch * b + LANES * (j + 1)]
        else:
            us_ref[j] = u_ref[:, LANES * j:LANES * (j + 1)]

    for j in range(tiles):
        xs_ref[:, 2 * STATE_TILE * j:2 * STATE_TILE * (j + 1)] = jnp.dot(
            us_ref[j].astype(BF16), wx_ref[j], preferred_element_type=F32)

    for j in range(tiles):
        re = slice(2 * STATE_TILE * j, 2 * STATE_TILE * j + STATE_TILE)
        im = slice(2 * STATE_TILE * j + STATE_TILE, 2 * STATE_TILE * (j + 1))
        a_r = jnp.broadcast_to(ar_ref[:, STATE_TILE * j:STATE_TILE * (j + 1)], (bsz, STATE_TILE))
        a_i = jnp.broadcast_to(ai_ref[:, STATE_TILE * j:STATE_TILE * (j + 1)], (bsz, STATE_TILE))

        def step(t, carry, re=re, im=im, a_r=a_r, a_i=a_i):
            hr, hi = carry
            r = pl.ds(pl.multiple_of(t * bsz, bsz), bsz)
            nhr = a_r * hr - a_i * hi + xs_ref[r, re]
            nhi = a_r * hi + a_i * hr + xs_ref[r, im]
            xs_ref[r, re] = nhr
            xs_ref[r, im] = nhi
            return nhr, nhi

        hr, hi = lax.fori_loop(0, steps, step, (h_ref[:, re], h_ref[:, im]))
        h_ref[:, re] = hr
        h_ref[:, im] = hi

    for j in range(tiles):
        cols = slice(LANES * j, LANES * (j + 1))
        hb = xs_ref[:, 2 * STATE_TILE * j:2 * STATE_TILE * (j + 1)].astype(BF16)
        y = jnp.dot(hb, wc_ref[j], preferred_element_type=F32) + d_ref[:, cols] * us_ref[j]
        if batch_on_lanes:
            ys_ref[j] = y
            for b in range(bsz):
                y_ref[:, ch * b + LANES * j:ch * b + LANES * (j + 1)] = ys_ref[j, pl.ds(b, steps, stride=bsz), :]
        else:
            y_ref[:, cols] = y
    hout_ref[...] = h_ref[...]


def _s5_scan(u, wx, wc, ar, ai, d_skip, h0, bsz, steps, batch_on_lanes):
    tiles = wx.shape[0]
    ch = tiles * LANES
    width = tiles * 2 * STATE_TILE
    rows = steps * bsz
    if batch_on_lanes:
        n_chunks = u.shape[0] // steps
        io_spec = pl.BlockSpec((steps, bsz * ch), lambda i: (i, 0))
    else:
        n_chunks = 1
        assert u.shape == (rows, ch)
        io_spec = pl.BlockSpec((rows, ch), lambda i: (0, 0))
    const = lambda shape: pl.BlockSpec(shape, lambda i: (0,) * len(shape))
    return pl.pallas_call(
        functools.partial(_s5_scan_body, bsz=bsz, steps=steps, batch_on_lanes=batch_on_lanes),
        grid=(n_chunks,),
        in_specs=[io_spec, const(wx.shape), const(wc.shape), const((1, tiles * STATE_TILE)),
                  const((1, tiles * STATE_TILE)), const((1, ch)), const((bsz, width))],
        out_specs=[io_spec, const((bsz, width))],
        out_shape=[jax.ShapeDtypeStruct(u.shape, F32), jax.ShapeDtypeStruct((bsz, width), F32)],
        scratch_shapes=[pltpu.VMEM((tiles, rows, LANES), F32), pltpu.VMEM((tiles, rows, LANES), F32),
                        pltpu.VMEM((rows, width), F32), pltpu.VMEM((bsz, width), F32)],
        compiler_params=_compiler_params(("arbitrary",)),
        name="s5_scan",
    )(u, wx, wc, ar.reshape(1, -1), ai.reshape(1, -1), d_skip.reshape(1, ch), h0)


def _post_mix_body(attn_ref, y_ref, x_ref, ga_ref, gs_ref, wglu_ref, bglu_ref, wo_ref, o_ref):
    aw = attn_ref.shape[1]
    an = _rms(attn_ref[...], ga_ref[...])
    z = jax.nn.gelu(y_ref[...])
    gate = jax.nn.sigmoid(jnp.dot(z.astype(BF16), wglu_ref[...], preferred_element_type=F32) + bglu_ref[...])
    sn = _rms(z * gate, gs_ref[...])
    o_ref[...] = (x_ref[...]
                  + jnp.dot(an.astype(BF16), wo_ref[:aw, :], preferred_element_type=F32)
                  + jnp.dot(sn.astype(BF16), wo_ref[aw:, :], preferred_element_type=F32))


def _post_mix(attn, y, x, g_attn, g_ssm, w_glu, b_glu, w_out, layer, tm, y_tiles_per_seq):
    n, d = x.shape
    aw, sw = attn.shape[1], w_glu.shape[1]
    row = lambda w: pl.BlockSpec((tm, w), lambda i: (i, 0))
    vec = lambda w: pl.BlockSpec((None, 1, w), lambda i: (layer, 0, 0))
    mat = lambda r, c: pl.BlockSpec((None, r, c), lambda i: (layer, 0, 0))
    if y_tiles_per_seq is None:
        y_spec = row(sw)
    else:
        y_spec = pl.BlockSpec((tm, sw), lambda i: (i % y_tiles_per_seq, i // y_tiles_per_seq))
    return pl.pallas_call(
        _post_mix_body,
        grid=(n // tm,),
        in_specs=[row(aw), y_spec, row(d), vec(aw), vec(sw), mat(sw, sw), vec(sw), mat(aw + sw, d)],
        out_specs=row(d),
        out_shape=jax.ShapeDtypeStruct((n, d), F32),
        compiler_params=_compiler_params(("parallel",)),
        name="post_mix",
    )(attn, y, x, g_attn, g_ssm, w_glu, b_glu, w_out)


def _finish(x_ref, acc_ref, gf_ref, o_ref):
    y = x_ref[...] + acc_ref[...]
    o_ref[...] = y if gf_ref is None else _rms(y, gf_ref[...])


def _ffn_body(x_ref, g_ref, wg_ref, wu_ref, wd_ref, *rest, final):
    gf_ref = rest[0] if final else None
    o_ref, hn_ref, acc_ref = rest[-3:]
    f = pl.program_id(1)

    @pl.when(f == 0)
    def _():
        hn_ref[...] = _rms(x_ref[...], g_ref[...]).astype(BF16)
        acc_ref[...] = jnp.zeros_like(acc_ref)

    hn = hn_ref[...]
    gate = jnp.dot(hn, wg_ref[...], preferred_element_type=F32)
    up = jnp.dot(hn, wu_ref[...], preferred_element_type=F32)
    act = (jax.nn.silu(gate) * up).astype(BF16)
    acc_ref[...] += jnp.dot(act, wd_ref[...], preferred_element_type=F32)

    @pl.when(f == pl.num_programs(1) - 1)
    def _():
        _finish(x_ref, acc_ref, gf_ref, o_ref)


def _ffn(x, g, wg, wu, wd, layer, g_final, tm, tf):
    n, d = x.shape
    ff = wg.shape[2]
    final = g_final is not None
    blk = layer // 2
    in_specs = [pl.BlockSpec((tm, d), lambda i, f: (i, 0)),
                pl.BlockSpec((None, 1, d), lambda i, f: (layer, 0, 0)),
                pl.BlockSpec((None, d, tf), lambda i, f: (blk, 0, f)),
                pl.BlockSpec((None, d, tf), lambda i, f: (blk, 0, f)),
                pl.BlockSpec((None, tf, d), lambda i, f: (blk, f, 0))]
    args = [x, g, wg, wu, wd]
    if final:
        in_specs.append(pl.BlockSpec((1, d), lambda i, f: (0, 0)))
        args.append(g_final.reshape(1, d))
    return pl.pallas_call(
        functools.partial(_ffn_body, final=final),
        grid=(n // tm, ff // tf),
        in_specs=in_specs,
        out_specs=pl.BlockSpec((tm, d), lambda i, f: (i, 0)),
        out_shape=jax.ShapeDtypeStruct((n, d), F32),
        scratch_shapes=[pltpu.VMEM((tm, d), BF16), pltpu.VMEM((tm, d), F32)],
        compiler_params=_compiler_params(("parallel", "arbitrary")),
        name="ffn_dense",
    )(*args)


def _route(hn, wr_ref, br_ref, n_experts, lane):
    logits = jnp.dot(hn, wr_ref[...], preferred_element_type=F32, precision=lax.Precision.HIGHEST) + br_ref[...]
    logits = jnp.where(lane < n_experts, logits, -jnp.inf)
    m1 = jnp.max(logits, axis=-1, keepdims=True)
    i1 = jnp.min(jnp.where(logits == m1, lane, float(LANES)), axis=-1, keepdims=True)
    rest_logits = jnp.where(lane == i1, -jnp.inf, logits)
    m2 = jnp.max(rest_logits, axis=-1, keepdims=True)
    i2 = jnp.min(jnp.where(rest_logits == m2, lane, float(LANES)), axis=-1, keepdims=True)
    e2 = jnp.exp(m2 - m1)
    return i1, i2, 1.0 / (1.0 + e2), e2 / (1.0 + e2)


def _moe_body(x_ref, g_ref, wr_ref, br_ref, wg_ref, wu_ref, wd_ref, *rest, n_experts, final):
    gf_ref = rest[0] if final else None
    o_ref, hn_ref, comb_ref, acc_ref = rest[-4:]
    e, f = pl.program_id(1), pl.program_id(2)
    lane = lax.broadcasted_iota(jnp.int32, comb_ref.shape, 1).astype(F32)

    @pl.when((e == 0) & (f == 0))
    def _():
        hn = _rms(x_ref[...], g_ref[...])
        hn_ref[...] = hn.astype(BF16)
        acc_ref[...] = jnp.zeros_like(acc_ref)
        i1, i2, g1, g2 = _route(hn, wr_ref, br_ref, n_experts, lane)
        comb_ref[...] = jnp.where(lane == i1, g1, 0.0) + jnp.where(lane == i2, g2, 0.0)

    hn = hn_ref[...]
    gate = jnp.dot(hn, wg_ref[...], preferred_element_type=F32)
    up = jnp.dot(hn, wu_ref[...], preferred_element_type=F32)
    act = (jax.nn.silu(gate) * up).astype(BF16)
    weight = jnp.sum(jnp.where(lane == e.astype(F32), comb_ref[...], 0.0), axis=-1, keepdims=True)
    acc_ref[...] += weight * jnp.dot(act, wd_ref[...], preferred_element_type=F32)

    @pl.when((e == n_experts - 1) & (f == pl.num_programs(2) - 1))
    def _():
        _finish(x_ref, acc_ref, gf_ref, o_ref)


def _moe(x, g, w_router, b_router, wg, wu, wd, layer, g_final, tm, tf):
    n, d = x.shape
    _, n_experts, _, ff = wg.shape
    final = g_final is not None
    blk = layer // 2
    wr = jnp.zeros((d, LANES), F32).at[:, :n_experts].set(w_router.astype(F32))
    br = jnp.zeros((1, LANES), F32).at[0, :n_experts].set(b_router.astype(F32))
    in_specs = [pl.BlockSpec((tm, d), lambda i, e, f: (i, 0)),
                pl.BlockSpec((None, 1, d), lambda i, e, f: (layer, 0, 0)),
                pl.BlockSpec((d, LANES), lambda i, e, f: (0, 0)),
                pl.BlockSpec((1, LANES), lambda i, e, f: (0, 0)),
                pl.BlockSpec((None, None, d, tf), lambda i, e, f: (blk, e, 0, f)),
                pl.BlockSpec((None, None, d, tf), lambda i, e, f: (blk, e, 0, f)),
                pl.BlockSpec((None, None, tf, d), lambda i, e, f: (blk, e, f, 0))]
    args = [x, g, wr, br, wg, wu, wd]
    if final:
        in_specs.append(pl.BlockSpec((1, d), lambda i, e, f: (0, 0)))
        args.append(g_final.reshape(1, d))
    return pl.pallas_call(
        functools.partial(_moe_body, n_experts=n_experts, final=final),
        grid=(n // tm, n_experts, ff // tf),
        in_specs=in_specs,
        out_specs=pl.BlockSpec((tm, d), lambda i, e, f: (i, 0)),
        out_shape=jax.ShapeDtypeStruct((n, d), F32),
        scratch_shapes=[pltpu.VMEM((tm, d), BF16), pltpu.VMEM((tm, LANES), F32), pltpu.VMEM((tm, d), F32)],
        compiler_params=_compiler_params(("parallel", "arbitrary", "arbitrary")),
        name="moe",
    )(*args)


EXPERT_ROW_TILE = 512
MOE_DMA_ROWS = 256


def _router_body(x_ref, g_ref, wr_ref, br_ref, hn_ref, route_ref, *, n_experts):
    lane = lax.broadcasted_iota(jnp.int32, route_ref.shape, 1).astype(F32)
    hn = _rms(x_ref[...], g_ref[...])
    hn_ref[...] = hn
    i1, i2, g1, g2 = _route(hn, wr_ref, br_ref, n_experts, lane)
    route_ref[...] = jnp.where(lane == 0.0, i1, jnp.where(lane == 1.0, i2, jnp.where(
        lane == 2.0, g1, jnp.where(lane == 3.0, g2, 0.0))))


def _router(x, g, wr, br, layer, n_experts, tm):
    n, d = x.shape
    return pl.pallas_call(
        functools.partial(_router_body, n_experts=n_experts),
        grid=(n // tm,),
        in_specs=[pl.BlockSpec((tm, d), lambda i: (i, 0)),
                  pl.BlockSpec((None, 1, d), lambda i: (layer, 0, 0)),
                  pl.BlockSpec((d, LANES), lambda i: (0, 0)),
                  pl.BlockSpec((1, LANES), lambda i: (0, 0))],
        out_specs=[pl.BlockSpec((tm, d), lambda i: (i, 0)), pl.BlockSpec((tm, LANES), lambda i: (i, 0))],
        out_shape=[jax.ShapeDtypeStruct((n, d), F32), jax.ShapeDtypeStruct((n, LANES), F32)],
        compiler_params=_compiler_params(("parallel",)),
        name="moe_router",
    )(x, g, wr, br)


def _dispatch_plan(experts, n_experts, n_tiles):
    flat = experts.reshape(-1)
    onehot = (flat[:, None] == jnp.arange(n_experts, dtype=jnp.int32)[None, :]).astype(jnp.int32)
    running = jnp.cumsum(onehot, axis=0)
    rank = jnp.sum(onehot * running, axis=1) - 1
    tiles_per_expert = (running[-1] + EXPERT_ROW_TILE - 1) // EXPERT_ROW_TILE
    tile_end = jnp.cumsum(tiles_per_expert)
    group_start = (tile_end - tiles_per_expert) * EXPERT_ROW_TILE
    slot = jnp.sum(onehot * group_start[None, :], axis=1) + rank
    tile = jnp.arange(n_tiles, dtype=jnp.int32)
    tile_expert = jnp.minimum(jnp.sum((tile[:, None] >= tile_end[None, :]).astype(jnp.int32), axis=1),
                              n_experts - 1)
    tile_valid = (tile < tile_end[-1]).astype(jnp.int32)
    return slot.astype(jnp.int32), tile_expert.astype(jnp.int32), tile_valid


def _row_copy(src_ref, src_row, dst_ref, dst_row, sem):
    return pltpu.make_async_copy(src_ref.at[pl.ds(src_row, 1)], dst_ref.at[pl.ds(dst_row, 1)], sem)


def _slot_tokens_body(slot_ref, tok_ref):
    def clear(i, c):
        for k in range(8):
            tok_ref[8 * i + k] = 0
        return c

    def fill(i, c):
        for k in range(8):
            tok_ref[slot_ref[8 * i + k]] = 4 * i + k // 2
        return c

    lax.fori_loop(0, tok_ref.shape[0] // 8, clear, 0)
    lax.fori_loop(0, slot_ref.shape[0] // 8, fill, 0)


def _slot_tokens(slot, n_slots):
    smem = pl.BlockSpec(memory_space=pltpu.SMEM)
    return pl.pallas_call(
        _slot_tokens_body,
        in_specs=[smem],
        out_specs=smem,
        out_shape=jax.ShapeDtypeStruct((n_slots,), jnp.int32),
        compiler_params=pltpu.CompilerParams(disable_bounds_checks=True),
        name="moe_slot_tokens",
    )(slot)


def _experts_body(tok_ref, te_ref, tv_ref, hn_ref, wg_ref, wu_ref, wd_ref, y_ref, xbuf_ref, sem, *, tf):
    t = pl.program_id(0)
    last = pl.num_programs(0) - 1
    rows = y_ref.shape[0]
    n_f = wg_ref.shape[1] // tf
    cur = t % 2

    def copy(tile, buf, j):
        return _row_copy(hn_ref, tok_ref[tile * rows + j], xbuf_ref.at[buf], j, sem.at[buf])

    def start_all(tile, buf):
        for j in range(rows):
            copy(tile, buf, j).start()

    def wait_all(buf):
        pltpu.make_async_copy(hn_ref.at[pl.ds(0, rows)], xbuf_ref.at[buf], sem.at[buf]).wait()

    @pl.when(t == 0)
    def _():
        start_all(0, 0)

    @pl.when((t == 0) | (tv_ref[jnp.maximum(t - 1, 0)] != 0))
    def _():
        wait_all(cur)

    live = tv_ref[t] != 0

    @pl.when(live)
    def _():
        start_all(jnp.minimum(t + 1, last), 1 - cur)

    @pl.when(live)
    def _():
        xb = xbuf_ref[cur].astype(BF16)
        for f in range(n_f):
            cols = slice(f * tf, (f + 1) * tf)
            gate = jnp.dot(xb, wg_ref[:, cols], preferred_element_type=F32)
            up = jnp.dot(xb, wu_ref[:, cols], preferred_element_type=F32)
            act = (jax.nn.silu(gate) * up).astype(BF16)
            part = jnp.dot(act, wd_ref[cols, :], preferred_element_type=F32)
            if f == 0:
                y_ref[...] = part
            else:
                y_ref[...] += part

    @pl.when(jnp.logical_not(live))
    def _():
        y_ref[...] = jnp.zeros_like(y_ref)

    @pl.when((t == last) & live)
    def _():
        wait_all(1 - cur)


def _experts(tok, tile_expert, tile_valid, hn, wg, wu, wd, layer, tf):
    d = hn.shape[1]
    ff = wg.shape[3]
    blk = layer // 2
    n_tiles = tile_expert.shape[0]
    weights = lambda r, c: pl.BlockSpec((None, None, r, c), lambda t, tok, te, tv: (blk, te[t], 0, 0))
    return pl.pallas_call(
        functools.partial(_experts_body, tf=tf),
        grid_spec=pltpu.PrefetchScalarGridSpec(
            num_scalar_prefetch=3,
            grid=(n_tiles,),
            in_specs=[pl.BlockSpec(memory_space=pl.ANY), weights(d, ff), weights(d, ff), weights(ff, d)],
            out_specs=pl.BlockSpec((EXPERT_ROW_TILE, d), lambda t, tok, te, tv: (t, 0)),
            scratch_shapes=[pltpu.VMEM((2, EXPERT_ROW_TILE, d), F32), pltpu.SemaphoreType.DMA((2,))]),
        out_shape=jax.ShapeDtypeStruct((n_tiles * EXPERT_ROW_TILE, d), F32),
        compiler_params=pltpu.CompilerParams(dimension_semantics=("arbitrary",), disable_bounds_checks=True,
                                             vmem_limit_bytes=VMEM_LIMIT_BYTES),
        name="moe_experts",
    )(tok, tile_expert, tile_valid, hn, wg, wu, wd)


def _combine_body(slot_ref, x_ref, route_ref, y_ref, *rest, final):
    gf_ref = rest[0] if final else None
    o_ref, ya_ref, yb_ref, sem = rest[-4:]
    i = pl.program_id(0)
    cur = i % 2

    def start_step(step, buf):
        base = step * MOE_DMA_ROWS

        for j in range(MOE_DMA_ROWS):
            t = base + j
            _row_copy(y_ref, slot_ref[2 * t], ya_ref.at[buf], j, sem.at[0, buf]).start()
            _row_copy(y_ref, slot_ref[2 * t + 1], yb_ref.at[buf], j, sem.at[1, buf]).start()

    @pl.when(i == 0)
    def _():
        start_step(0, 0)

    @pl.when(i + 1 < pl.num_programs(0))
    def _():
        start_step(i + 1, 1 - cur)

    for k, buf_ref in enumerate((ya_ref, yb_ref)):
        pltpu.make_async_copy(y_ref.at[pl.ds(0, MOE_DMA_ROWS)], buf_ref.at[cur], sem.at[k, cur]).wait()
    y = x_ref[...] + route_ref[:, 2:3] * ya_ref[cur] + route_ref[:, 3:4] * yb_ref[cur]
    o_ref[...] = y if gf_ref is None else _rms(y, gf_ref[...])


def _combine(slot, x, route, y, g_final):
    n, d = x.shape
    final = g_final is not None
    in_specs = [pl.BlockSpec((MOE_DMA_ROWS, d), lambda i, s: (i, 0)),
                pl.BlockSpec((MOE_DMA_ROWS, LANES), lambda i, s: (i, 0)),
                pl.BlockSpec(memory_space=pl.ANY)]
    args = [slot, x, route, y]
    if final:
        in_specs.append(pl.BlockSpec((1, d), lambda i, s: (0, 0)))
        args.append(g_final.reshape(1, d))
    return pl.pallas_call(
        functools.partial(_combine_body, final=final),
        grid_spec=pltpu.PrefetchScalarGridSpec(
            num_scalar_prefetch=1,
            grid=(n // MOE_DMA_ROWS,),
            in_specs=in_specs,
            out_specs=pl.BlockSpec((MOE_DMA_ROWS, d), lambda i, s: (i, 0)),
            scratch_shapes=[pltpu.VMEM((2, MOE_DMA_ROWS, d), F32), pltpu.VMEM((2, MOE_DMA_ROWS, d), F32),
                            pltpu.SemaphoreType.DMA((2, 2))]),
        out_shape=jax.ShapeDtypeStruct((n, d), F32),
        compiler_params=pltpu.CompilerParams(dimension_semantics=("arbitrary",), disable_bounds_checks=True,
                                             vmem_limit_bytes=VMEM_LIMIT_BYTES),
        name="moe_combine",
    )(*args)


def _moe_sorted(x, g, w_router, b_router, wg, wu, wd, layer, g_final, tm, tf):
    n, d = x.shape
    n_experts = wg.shape[1]
    wr = jnp.zeros((d, LANES), F32).at[:, :n_experts].set(w_router.astype(F32))
    br = jnp.zeros((1, LANES), F32).at[0, :n_experts].set(b_router.astype(F32))
    hn, route = _router(x, g, wr, br, layer, n_experts, tm)
    n_tiles = 2 * n // EXPERT_ROW_TILE + n_experts
    slot, tile_expert, tile_valid = _dispatch_plan(route[:, :2].astype(jnp.int32), n_experts, n_tiles)
    tok = _slot_tokens(slot, n_tiles * EXPERT_ROW_TILE)
    y = _experts(tok, tile_expert, tile_valid, hn, wg, wu, wd, layer, tf)
    return _combine(slot, x, route, y, g_final)


PROMPT_ROW_TILE = 512
PROMPT_SCAN_STEPS = 128


def kernel(x_prompt, x_sample, cache_win_k, cache_win_v, state_ssm_re, state_ssm_im, rel_bias, norm_mix, w_in, ssm_a_re, ssm_a_im, ssm_log_dt, ssm_b_re, ssm_b_im, ssm_c_re, ssm_c_im, ssm_d, ssm_w_glu, ssm_b_glu, norm_attn_out, norm_ssm_out, w_out, norm_ffn, ffn_w_gate, ffn_w_up, ffn_w_down, moe_w_router, moe_b_router, moe_w_gate, moe_w_up, moe_w_down, norm_final):
    batch, seq, d_model = x_prompt.shape
    dec_batch, dec_seq, _ = x_sample.shape
    depth = w_in.shape[0]
    win, n_heads = cache_win_k.shape[2:4]
    attn_width = n_heads * HEAD_DIM
    n_groups, n_state = ssm_a_re.shape[1:]
    ssm_width = n_groups * SSM_GROUP
    assert w_in.shape[2] == 3 * attn_width + ssm_width and seq == MAX_WINDOW
    dec_rows = dec_batch * dec_seq

    band_bias = _band_bias(rel_bias)
    sample_tabs = _sample_tables(rel_bias, win, dec_seq)
    ar, ai, bb_re, bb_im = _s5_params(ssm_a_re, ssm_a_im, ssm_log_dt, ssm_b_re, ssm_b_im)
    cache_k = jnp.transpose(cache_win_k, (0, 1, 3, 4, 2))
    cache_v = jnp.transpose(cache_win_v, (0, 1, 3, 4, 2))

    rows3 = lambda a: a.reshape(a.shape[0], 1, a.shape[1])
    norm_mix3, norm_ffn3, b_glu3 = rows3(norm_mix), rows3(norm_ffn), rows3(ssm_b_glu)
    g_attn3, g_ssm3 = rows3(norm_attn_out), rows3(norm_ssm_out)
    w_in_b, w_out_b, w_glu_b = w_in.astype(BF16), w_out.astype(BF16), ssm_w_glu.astype(BF16)
    ffn_b = [w.astype(BF16) for w in (ffn_w_gate, ffn_w_up, ffn_w_down)]
    moe_b = [w.astype(BF16) for w in (moe_w_gate, moe_w_up, moe_w_down)]
    tf = ffn_w_gate.shape[2] // 2

    def mix_and_ffn(x, l, attn, y, tm, y_tiles_per_seq):
        g_final = norm_final if l == depth - 1 else None
        x = _post_mix(attn, y, x, g_attn3, g_ssm3, w_glu_b, b_glu3, w_out_b, l, tm, y_tiles_per_seq)
        if l % 2 == 0:
            return _ffn(x, norm_ffn3, *ffn_b, l, g_final, tm, tf)
        n_experts = moe_w_gate.shape[1]
        moe = _moe_sorted if 2 * x.shape[0] >= n_experts * EXPERT_ROW_TILE else _moe
        return moe(x, norm_ffn3, moe_w_router[l // 2], moe_b_router[l // 2], *moe_b, l, g_final, tm, tf)

    xp = x_prompt.reshape(batch * seq, d_model)
    xs = x_sample.reshape(dec_rows, d_model)
    zero_state = jnp.zeros((batch, 2 * n_groups * n_state), F32)
    k_all = jnp.zeros((depth, batch, attn_width, seq), F32)
    v_all = jnp.zeros((depth, batch, attn_width, seq), F32)
    outs = [[] for _ in range(8)]
    for l in range(depth):
        wx, wc = _pack_s5_weights(bb_re[l], bb_im[l], ssm_c_re[l].astype(F32), ssm_c_im[l].astype(F32))

        qkv, k_all, v_all, u = _norm_proj_prompt(xp, norm_mix3, w_in_b, l, k_all, v_all, PROMPT_ROW_TILE)
        attn = _prompt_attn(qkv, band_bias, batch, seq, attn_width)
        y, h_fin = _s5_scan(u, wx, wc, ar[l], ai[l], ssm_d[l], zero_state, batch, PROMPT_SCAN_STEPS, True)
        xp = mix_and_ffn(xp, l, attn, y, PROMPT_ROW_TILE, seq // PROMPT_ROW_TILE)
        h_re, h_im = _unpack_state(h_fin, n_groups, n_state)
        outs[2].append(h_re)
        outs[3].append(h_im)

        proj = _norm_proj(xs, norm_mix3, w_in_b, l, dec_rows)
        q, k_new, v_new = (proj[:, i * attn_width:(i + 1) * attn_width].reshape(dec_batch, dec_seq, n_heads, HEAD_DIM)
                           for i in range(3))
        attn = _sample_attn(q, k_new, v_new, cache_k, cache_v, l, sample_tabs).reshape(dec_rows, attn_width)
        u = proj[:, 3 * attn_width:].reshape(dec_batch, dec_seq, ssm_width).transpose(1, 0, 2)
        y, h_fin = _s5_scan(u.reshape(dec_rows, ssm_width), wx, wc, ar[l], ai[l], ssm_d[l],
                            _pack_state(state_ssm_re[l], state_ssm_im[l]), dec_batch, dec_seq, False)
        y = y.reshape(dec_seq, dec_batch, ssm_width).transpose(1, 0, 2).reshape(dec_rows, ssm_width)
        xs = mix_and_ffn(xs, l, attn, y, dec_rows, None)
        outs[4].append(k_new)
        outs[5].append(v_new)
        h_re, h_im = _unpack_state(h_fin, n_groups, n_state)
        outs[6].append(h_re)
        outs[7].append(h_im)

    def window_out(a):
        return jnp.transpose(a.reshape(depth, batch, n_heads, HEAD_DIM, seq), (0, 1, 4, 2, 3))

    return (xp.reshape(batch, seq, d_model), xs.reshape(dec_batch, dec_seq, d_model),
            window_out(k_all), window_out(v_all), *[jnp.stack(o, 0) for o in outs[2:]])
```

```python
import functools

import numpy as np
import jax
import jax.numpy as jnp
from jax import lax
from jax.experimental import pallas as pl
from jax.experimental.pallas import tpu as pltpu

F32 = jnp.float32
BF16 = jnp.bfloat16

HEAD_DIM = 64
DILATION_PATTERNS = ((128, 1), (512, 4), (2048, 16))
MAX_WINDOW = 2048
N_BUCKETS = 32
SSM_GROUP = 16
SSM_STATE = 64
RMS_EPS = 1e-6
NEG_BIG = -1e30
LOG2_E = 1.4426950408889634

LANES = 128
Q_TILE = 128
ATTN_UNROLL = 4
GROUPS_PER_TILE = LANES // SSM_GROUP
STATE_TILE = GROUPS_PER_TILE * SSM_STATE
VMEM_LIMIT_BYTES = 56 * 1024 * 1024


def _compiler_params(semantics):
    return pltpu.CompilerParams(dimension_semantics=semantics, vmem_limit_bytes=VMEM_LIMIT_BYTES)


def _rms(x, g):
    return x * lax.rsqrt(jnp.mean(x * x, axis=-1, keepdims=True) + RMS_EPS) * g


def _t5_causal_bucket(dist):
    max_exact = N_BUCKETS // 2
    d = np.asarray(dist, dtype=np.int64)
    large = max_exact + (np.log(np.maximum(d, max_exact) / max_exact)
                         / np.log(MAX_WINDOW / max_exact)
                         * (N_BUCKETS - max_exact)).astype(np.int64)
    large = np.minimum(large, N_BUCKETS - 1)
    return np.where(d < max_exact, d, large).astype(np.int32)


def _multiplicity(d):
    m = np.zeros(d.shape, np.int32)
    for window, dil in DILATION_PATTERNS:
        m += ((d >= 0) & (d <= window) & (d % dil == 0)).astype(np.int32)
    return m


def _norm_proj_body(x_ref, g_ref, w_ref, o_ref):
    hn = _rms(x_ref[...], g_ref[...])
    o_ref[...] = jnp.dot(hn.astype(BF16), w_ref[...], preferred_element_type=F32)


def _norm_proj(x, g, w, layer, tm):
    n, d = x.shape
    c = w.shape[2]
    return pl.pallas_call(
        _norm_proj_body,
        grid=(n // tm,),
        in_specs=[pl.BlockSpec((tm, d), lambda i: (i, 0)),
                  pl.BlockSpec((None, 1, d), lambda i: (layer, 0, 0)),
                  pl.BlockSpec((None, d, c), lambda i: (layer, 0, 0))],
        out_specs=pl.BlockSpec((tm, c), lambda i: (i, 0)),
        out_shape=jax.ShapeDtypeStruct((n, c), F32),
        compiler_params=_compiler_params(("parallel",)),
        name="norm_proj",
    )(x, g, w)


def _norm_proj_prompt_body(x_ref, g_ref, w_ref, _k_all, _v_all, qkv_ref, kt_ref, vt_ref, u_ref, *, attn_width):
    hn = _rms(x_ref[...], g_ref[...])
    proj = jnp.dot(hn.astype(BF16), w_ref[...], preferred_element_type=F32)
    qkv_ref[...] = proj[:, :3 * attn_width]
    kt_ref[...] = proj[:, attn_width:2 * attn_width].T
    vt_ref[...] = proj[:, 2 * attn_width:3 * attn_width].T
    u_ref[...] = proj[:, 3 * attn_width:]


def _norm_proj_prompt(x, g, w, layer, k_all, v_all, tm):
    n, d = x.shape
    c = w.shape[2]
    _, batch, attn_width, seq = k_all.shape
    ch = c - 3 * attn_width
    per_seq = seq // tm
    window = pl.BlockSpec((None, None, attn_width, tm), lambda i: (layer, i // per_seq, 0, i % per_seq))
    untouched = pl.BlockSpec(memory_space=pl.ANY)
    return pl.pallas_call(
        functools.partial(_norm_proj_prompt_body, attn_width=attn_width),
        grid=(n // tm,),
        in_specs=[pl.BlockSpec((tm, d), lambda i: (i, 0)),
                  pl.BlockSpec((None, 1, d), lambda i: (layer, 0, 0)),
                  pl.BlockSpec((None, d, c), lambda i: (layer, 0, 0)),
                  untouched, untouched],
        out_specs=[pl.BlockSpec((tm, 3 * attn_width), lambda i: (i, 0)), window, window,
                   pl.BlockSpec((tm, ch), lambda i: (i % per_seq, i // per_seq))],
        out_shape=[jax.ShapeDtypeStruct((n, 3 * attn_width), F32),
                   jax.ShapeDtypeStruct(k_all.shape, F32),
                   jax.ShapeDtypeStruct(v_all.shape, F32),
                   jax.ShapeDtypeStruct((seq, batch * ch), F32)],
        input_output_aliases={3: 1, 4: 2},
        compiler_params=_compiler_params(("parallel",)),
        name="norm_proj_prompt",
    )(x, g, w, k_all, v_all)


def _band_bias(rel_bias):
    period = 3 * Q_TILE
    n_heads = rel_bias.shape[1]
    rows0 = []
    for window, dil in DILATION_PATTERNS:
        assert window // dil == Q_TILE
        vec = rel_bias[_t5_causal_bucket((Q_TILE - np.arange(Q_TILE + 1)) * dil)].astype(F32)
        rows0.append(jnp.concatenate([vec.T, jnp.full((n_heads, period - Q_TILE - 1), NEG_BIG, F32)], axis=1))
    row0 = jnp.stack(rows0, 0)
    flat = jnp.tile(row0, (1, 1, Q_TILE))[:, :, :Q_TILE * (period - 1)]
    band = flat.reshape(len(DILATION_PATTERNS), n_heads, Q_TILE, period - 1)[:, :, :, :2 * Q_TILE]
    return band * LOG2_E


def _prompt_attn_body(q_ref, k_ref, v_ref, bias_ref, o_ref, op_ref, m_ref, l_ref):
    seq = q_ref.shape[0]
    lane = lax.broadcasted_iota(jnp.int32, (Q_TILE, LANES), 1)
    head0 = lane < HEAD_DIM
    scale = HEAD_DIM ** -0.5 * LOG2_E

    def rows(start, size, dil):
        return pl.ds(start, size) if dil == 1 else pl.ds(start, size, stride=dil)

    def tiles(p, dil, starts, first):
        nk = Q_TILE if first else 2 * Q_TILE
        bias = (bias_ref[p, :, :, Q_TILE:] if first else bias_ref[p]).reshape(2 * Q_TILE, nk)
        work = []
        for q_start, k_start in starts:
            qsl, ksl = rows(q_start, Q_TILE, dil), rows(k_start, nk, dil)
            qt = q_ref[qsl, :] * scale
            q2 = jnp.concatenate([jnp.where(head0, qt, 0.0), jnp.where(head0, 0.0, qt)], axis=0).astype(BF16)
            v_ones = jnp.concatenate([v_ref[ksl, :].astype(BF16), jnp.ones((nk, LANES), BF16)], axis=1)
            work.append([qsl, v_ones, q2, k_ref[ksl, :].astype(BF16)])
        for w in work:
            w[2] = lax.dot_general(w[2], w[3], (((1,), (1,)), ((), ())), preferred_element_type=F32) + bias
        for w in work:
            w[3] = jnp.max(w[2], axis=-1, keepdims=True)
        for w in work:
            w[2] = jnp.exp2((w[2] - w[3]).astype(BF16))
        for w in work:
            w[2] = jnp.dot(w[2], w[1], preferred_element_type=F32)
        for qsl, _, ol, m in work:
            op_ref[p, qsl, :] = jnp.where(head0, ol[:Q_TILE, :LANES], ol[Q_TILE:, :LANES])
            m_ref[p, qsl, :] = jnp.where(head0, m[:Q_TILE], m[Q_TILE:])
            l_ref[p, qsl, :] = jnp.where(head0, ol[:Q_TILE, LANES:], ol[Q_TILE:, LANES:])

    for p, (_, dil) in enumerate(DILATION_PATTERNS):
        n_tiles = seq // dil // Q_TILE
        stride = dil * Q_TILE
        unroll = min(ATTN_UNROLL, dil)
        if dil == 1:
            unroll = max(u for u in range(1, 2 * ATTN_UNROLL + 1) if (n_tiles - 1) % u == 0)
            for r in range(dil):
                tiles(p, dil, [(r, r)], True)

                def blk(t, c, p=p, dil=dil, r=r, stride=stride, unroll=unroll):
                    first_t = 1 + t * unroll
                    tiles(p, dil, [(r + stride * (first_t + j), r + stride * (first_t + j - 1))
                                   for j in range(unroll)], False)
                    return c
                lax.fori_loop(0, (n_tiles - 1) // unroll, blk, 0)
        else:
            assert dil % unroll == 0

            def classes(g, c, p=p, dil=dil, n_tiles=n_tiles, stride=stride, unroll=unroll):
                rs = [g * unroll + j for j in range(unroll)]
                tiles(p, dil, [(r, r) for r in rs], True)
                if n_tiles > 1:
                    def blk(t, c2):
                        tiles(p, dil, [(r + stride * t, r + stride * (t - 1)) for r in rs], False)
                        return c2
                    lax.fori_loop(1, n_tiles, blk, 0)
                return c
            lax.fori_loop(0, dil // unroll, classes, 0)

    def combine(t, carry):
        r = pl.ds(pl.multiple_of(t * Q_TILE, Q_TILE), Q_TILE)
        ms = [m_ref[p, r, :] for p in range(len(DILATION_PATTERNS))]
        mx = functools.reduce(jnp.maximum, ms)
        ws = [jnp.exp2(m - mx) for m in ms]
        num = sum(w * op_ref[p, r, :] for p, w in enumerate(ws))
        den = sum(w * l_ref[p, r, :] for p, w in enumerate(ws))
        o_ref[r, :] = num / den
        return carry

    lax.fori_loop(0, seq // Q_TILE, combine, 0)


def _prompt_attn(proj, bias, batch, seq, attn_width):
    n_pairs = attn_width // LANES
    n_pat = len(DILATION_PATTERNS)
    assert seq % (Q_TILE * max(d for _, d in DILATION_PATTERNS)) == 0
    return pl.pallas_call(
        _prompt_attn_body,
        grid=(batch, n_pairs),
        in_specs=[pl.BlockSpec((seq, LANES), lambda b, h: (b, h)),
                  pl.BlockSpec((seq, LANES), lambda b, h: (b, n_pairs + h)),
                  pl.BlockSpec((seq, LANES), lambda b, h: (b, 2 * n_pairs + h)),
                  pl.BlockSpec((n_pat, 2, Q_TILE, 2 * Q_TILE), lambda b, h: (0, h, 0, 0))],
        out_specs=pl.BlockSpec((seq, LANES), lambda b, h: (b, h)),
        out_shape=jax.ShapeDtypeStruct((batch * seq, attn_width), F32),
        scratch_shapes=[pltpu.VMEM((n_pat, seq, LANES), F32)] * 3,
        compiler_params=_compiler_params(("parallel", "parallel")),
        name="prompt_attn",
    )(proj, proj, proj, bias)


def _sample_tables(rel_bias, win, n_new):
    n_heads = rel_bias.shape[1]
    vec_t = rel_bias[_t5_causal_bucket(np.arange(win + n_new))].astype(F32).T
    flipped = vec_t[:, ::-1]
    t = np.arange(n_new)
    d_cache = win + t[:, None] - np.arange(win)[None, :]
    d_new = t[:, None] - t[None, :]
    b_cache = jnp.stack([flipped[:, n_new - 1 - i:n_new - 1 - i + win] for i in range(n_new)], 0)
    b_new = jnp.transpose(vec_t[:, np.maximum(d_new, 0)], (1, 0, 2))
    out = []
    for b, d in ((b_cache, d_cache), (b_new, d_new)):
        mult = _multiplicity(d)
        b = jnp.where((mult > 0)[:, None, :], b, NEG_BIG).reshape(n_new * n_heads, d.shape[1])
        m = np.broadcast_to(np.maximum(mult, 1)[:, None, :], (n_new, n_heads, d.shape[1]))
        out += [b, jnp.asarray(m.reshape(n_new * n_heads, d.shape[1]), F32)]
    return out


def _sample_attn_body(q_ref, kc_ref, kn_ref, vc_ref, vn_ref, bc_ref, mc_ref, bn_ref, mn_ref, hm_ref, o_ref):
    n_new = o_ref.shape[0]
    win = kc_ref.shape[-1]
    qb = (q_ref[...] * HEAD_DIM ** -0.5).astype(BF16)
    s_c = jnp.dot(qb, kc_ref[...].reshape(-1, win).astype(BF16), preferred_element_type=F32) + bc_ref[...]
    s_n = jnp.dot(qb, kn_ref[...].astype(BF16), preferred_element_type=F32) + bn_ref[...]
    m = jnp.maximum(jnp.max(s_c, axis=-1, keepdims=True), jnp.max(s_n, axis=-1, keepdims=True))
    e_c = jnp.exp(s_c - m) * mc_ref[...]
    e_n = jnp.exp(s_n - m) * mn_ref[...]
    den = jnp.sum(e_c, axis=-1, keepdims=True) + jnp.sum(e_n, axis=-1, keepdims=True)
    nt = (((1,), (1,)), ((), ()))
    o = (lax.dot_general(e_c.astype(BF16), vc_ref[...].reshape(-1, win).astype(BF16), nt,
                         preferred_element_type=F32)
         + lax.dot_general(e_n.astype(BF16), vn_ref[...].astype(BF16), nt, preferred_element_type=F32))
    o = o / den * hm_ref[...]
    o_ref[...] = jnp.sum(o.reshape(n_new, -1, o.shape[-1]), axis=1)


def _sample_attn(q, k_new, v_new, cache_k, cache_v, layer, tables):
    depth, batch, n_heads, hd, win = cache_k.shape
    n_new = q.shape[1]
    width = n_heads * hd
    rows = n_new * n_heads
    eye = jnp.eye(n_heads, dtype=F32)
    q_bd = jnp.einsum('bthd,hg->btghd', q, eye).reshape(batch, rows, width)
    kn = jnp.transpose(k_new, (0, 2, 3, 1)).reshape(batch, width, n_new)
    vn = jnp.transpose(v_new, (0, 2, 3, 1)).reshape(batch, width, n_new)
    head_mask = jnp.asarray(np.kron(np.tile(np.eye(n_heads), (n_new, 1)), np.ones((1, hd))), F32)
    cache_spec = pl.BlockSpec((None, None, n_heads, hd, win), lambda b: (layer, b, 0, 0, 0))
    new_spec = pl.BlockSpec((None, width, n_new), lambda b: (b, 0, 0))
    const = lambda a: pl.BlockSpec(a.shape, lambda b: (0, 0))
    return pl.pallas_call(
        _sample_attn_body,
        grid=(batch,),
        in_specs=[pl.BlockSpec((None, rows, width), lambda b: (b, 0, 0)),
                  cache_spec, new_spec, cache_spec, new_spec] + [const(t) for t in tables] + [const(head_mask)],
        out_specs=pl.BlockSpec((None, n_new, width), lambda b: (b, 0, 0)),
        out_shape=jax.ShapeDtypeStruct((batch, n_new, width), F32),
        compiler_params=_compiler_params(("parallel",)),
        name="sample_attn",
    )(q_bd, cache_k, kn, cache_v, vn, *tables, head_mask)


def _s5_param_body(are_ref, aim_ref, ldt_ref, bre_ref, bim_ref, ar_ref, ai_ref, bbre_ref, bbim_ref):
    a_re, a_im = are_ref[...], aim_ref[...]
    dt = jnp.exp(ldt_ref[...])
    mag = jnp.exp(dt * a_re)
    ar, ai = mag * jnp.cos(dt * a_im), mag * jnp.sin(dt * a_im)
    nr, ni = ar - 1.0, ai
    den = a_re * a_re + a_im * a_im
    fr = (nr * a_re + ni * a_im) / den
    fi = (ni * a_re - nr * a_im) / den
    br, bi = bre_ref[...], bim_ref[...]
    ar_ref[...] = ar
    ai_ref[...] = ai
    bbre_ref[...] = fr[:, None, :] * br - fi[:, None, :] * bi
    bbim_ref[...] = fr[:, None, :] * bi + fi[:, None, :] * br


def _s5_params(a_re, a_im, log_dt, b_re, b_im):
    depth, g, p = a_re.shape
    c = b_re.shape[-1]
    n = depth * g
    full = lambda shape: pl.BlockSpec(shape, lambda: (0,) * len(shape))
    ar, ai, bbre, bbim = pl.pallas_call(
        _s5_param_body,
        in_specs=[full((n, p)), full((n, p)), full((n, 1)), full((n, c, p)), full((n, c, p))],
        out_specs=[full((n, p)), full((n, p)), full((n, c, p)), full((n, c, p))],
        out_shape=[jax.ShapeDtypeStruct((n, p), F32), jax.ShapeDtypeStruct((n, p), F32),
                   jax.ShapeDtypeStruct((n, c, p), F32), jax.ShapeDtypeStruct((n, c, p), F32)],
        name="s5_params",
    )(a_re.reshape(n, p), a_im.reshape(n, p), log_dt.reshape(n, 1),
      jnp.swapaxes(b_re, -1, -2).reshape(n, c, p), jnp.swapaxes(b_im, -1, -2).reshape(n, c, p))
    return (ar.reshape(depth, g, p), ai.reshape(depth, g, p),
            bbre.reshape(depth, g, c, p), bbim.reshape(depth, g, c, p))


def _pack_s5_weights(bb_re, bb_im, c_re, c_im):
    g, c, p = bb_re.shape
    tiles = g // GROUPS_PER_TILE
    eye = jnp.eye(GROUPS_PER_TILE, dtype=F32)

    def inp(w):
        w = w.reshape(tiles, GROUPS_PER_TILE, c, p)
        return jnp.einsum('jgcp,gh->jgchp', w, eye).reshape(tiles, LANES, STATE_TILE)

    def outp(w):
        w = w.reshape(tiles, GROUPS_PER_TILE, c, p)
        return jnp.einsum('jgcp,gh->jgphc', w, eye).reshape(tiles, STATE_TILE, LANES)

    wx = jnp.concatenate([inp(bb_re), inp(bb_im)], axis=-1).astype(BF16)
    wc = jnp.concatenate([outp(c_re), -outp(c_im)], axis=1).astype(BF16)
    return wx, wc


def _pack_state(h_re, h_im):
    b, g, p = h_re.shape
    tiles = g // GROUPS_PER_TILE
    st = jnp.stack([h_re.reshape(b, tiles, STATE_TILE), h_im.reshape(b, tiles, STATE_TILE)], axis=2)
    return st.reshape(b, tiles * 2 * STATE_TILE).astype(F32)


def _unpack_state(h, g, p):
    b = h.shape[0]
    st = h.reshape(b, g // GROUPS_PER_TILE, 2, STATE_TILE)
    return st[:, :, 0].reshape(b, g, p), st[:, :, 1].reshape(b, g, p)


def _s5_scan_body(u_ref, wx_ref, wc_ref, ar_ref, ai_ref, d_ref, h0_ref, y_ref, hout_ref, us_ref, ys_ref, xs_ref,
                  h_ref, *, bsz, steps, batch_on_lanes):
    tiles = wx_ref.shape[0]
    ch = tiles * LANES

    @pl.when(pl.program_id(0) == 0)
    def _():
        h_ref[...] = h0_ref[...]

    for j in range(tiles):
        if batch_on_lanes:
            for b in range(bsz):
                us_ref[j, pl.ds(b, steps, stride=bsz), :] = u_ref[:, ch * b + LANES * j:ch * b + LANES * (j + 1)]
        else:
            us_ref[j] = u_ref[:, LANES * j:LANES * (j + 1)]

    for j in range(tiles):
        xs_ref[:, 2 * STATE_TILE * j:2 * STATE_TILE * (j + 1)] = jnp.dot(
            us_ref[j].astype(BF16), wx_ref[j], preferred_element_type=F32)

    for j in range(tiles):
        re = slice(2 * STATE_TILE * j, 2 * STATE_TILE * j + STATE_TILE)
        im = slice(2 * STATE_TILE * j + STATE_TILE, 2 * STATE_TILE * (j + 1))
        a_r = jnp.broadcast_to(ar_ref[:, STATE_TILE * j:STATE_TILE * (j + 1)], (bsz, STATE_TILE))
        a_i = jnp.broadcast_to(ai_ref[:, STATE_TILE * j:STATE_TILE * (j + 1)], (bsz, STATE_TILE))

        def step(t, carry, re=re, im=im, a_r=a_r, a_i=a_i):
            hr, hi = carry
            r = pl.ds(pl.multiple_of(t * bsz, bsz), bsz)
            nhr = a_r * hr - a_i * hi + xs_ref[r, re]
            nhi = a_r * hi + a_i * hr + xs_ref[r, im]
            xs_ref[r, re] = nhr
            xs_ref[r, im] = nhi
            return nhr, nhi

        hr, hi = lax.fori_loop(0, steps, step, (h_ref[:, re], h_ref[:, im]))
        h_ref[:, re] = hr
        h_ref[:, im] = hi

    for j in range(tiles):
        cols = slice(LANES * j, LANES * (j + 1))
        hb = xs_ref[:, 2 * STATE_TILE * j:2 * STATE_TILE * (j + 1)].astype(BF16)
        y = jnp.dot(hb, wc_ref[j], preferred_element_type=F32) + d_ref[:, cols] * us_ref[j]
        if batch_on_lanes:
            ys_ref[j] = y
            for b in range(bsz):
                y_ref[:, ch * b + LANES * j:ch * b + LANES * (j + 1)] = ys_ref[j, pl.ds(b, steps, stride=bsz), :]
        else:
            y_ref[:, cols] = y
    hout_ref[...] = h_ref[...]


def _s5_scan(u, wx, wc, ar, ai, d_skip, h0, bsz, steps, batch_on_lanes):
    tiles = wx.shape[0]
    ch = tiles * LANES
    width = tiles * 2 * STATE_TILE
    rows = steps * bsz
    if batch_on_lanes:
        n_chunks = u.shape[0] // steps
        io_spec = pl.BlockSpec((steps, bsz * ch), lambda i: (i, 0))
    else:
        n_chunks = 1
        assert u.shape == (rows, ch)
        io_spec = pl.BlockSpec((rows, ch), lambda i: (0, 0))
    const = lambda shape: pl.BlockSpec(shape, lambda i: (0,) * len(shape))
    return pl.pallas_call(
        functools.partial(_s5_scan_body, bsz=bsz, steps=steps, batch_on_lanes=batch_on_lanes),
        grid=(n_chunks,),
        in_specs=[io_spec, const(wx.shape), const(wc.shape), const((1, tiles * STATE_TILE)),
                  const((1, tiles * STATE_TILE)), const((1, ch)), const((bsz, width))],
        out_specs=[io_spec, const((bsz, width))],
        out_shape=[jax.ShapeDtypeStruct(u.shape, F32), jax.ShapeDtypeStruct((bsz, width), F32)],
        scratch_shapes=[pltpu.VMEM((tiles, rows, LANES), F32), pltpu.VMEM((tiles, rows, LANES), F32),
                        pltpu.VMEM((rows, width), F32), pltpu.VMEM((bsz, width), F32)],
        compiler_params=_compiler_params(("arbitrary",)),
        name="s5_scan",
    )(u, wx, wc, ar.reshape(1, -1), ai.reshape(1, -1), d_skip.reshape(1, ch), h0)


def _post_mix_body(attn_ref, y_ref, x_ref, ga_ref, gs_ref, wglu_ref, bglu_ref, wo_ref, *rest, n_experts):
    aw = attn_ref.shape[1]
    an = _rms(attn_ref[...], ga_ref[...])
    z = jax.nn.gelu(y_ref[...])
    gate = jax.nn.sigmoid(jnp.dot(z.astype(BF16), wglu_ref[...], preferred_element_type=F32) + bglu_ref[...])
    sn = _rms(z * gate, gs_ref[...])
    x = (x_ref[...]
         + jnp.dot(an.astype(BF16), wo_ref[:aw, :], preferred_element_type=F32)
         + jnp.dot(sn.astype(BF16), wo_ref[aw:, :], preferred_element_type=F32))
    if n_experts:
        gf_ref, wr_ref, br_ref, o_ref, hn_ref, route_ref = rest
        _norm_and_route(x, gf_ref, wr_ref, br_ref, hn_ref, route_ref, n_experts)
    else:
        o_ref, = rest
    o_ref[...] = x


def _post_mix(attn, y, x, g_attn, g_ssm, w_glu, b_glu, w_out, layer, tm, y_tiles_per_seq, router=None):
    n, d = x.shape
    aw, sw = attn.shape[1], w_glu.shape[1]
    row = lambda w: pl.BlockSpec((tm, w), lambda i: (i, 0))
    vec = lambda w: pl.BlockSpec((None, 1, w), lambda i: (layer, 0, 0))
    mat = lambda r, c: pl.BlockSpec((None, r, c), lambda i: (layer, 0, 0))
    if y_tiles_per_seq is None:
        y_spec = row(sw)
    else:
        y_spec = pl.BlockSpec((tm, sw), lambda i: (i % y_tiles_per_seq, i // y_tiles_per_seq))
    in_specs = [row(aw), y_spec, row(d), vec(aw), vec(sw), mat(sw, sw), vec(sw), mat(aw + sw, d)]
    args = [attn, y, x, g_attn, g_ssm, w_glu, b_glu, w_out]
    out_specs, out_shape = row(d), jax.ShapeDtypeStruct((n, d), F32)
    n_experts = 0
    if router is not None:
        g_ffn, wr, br, n_experts = router
        in_specs += [vec(d), pl.BlockSpec((d, LANES), lambda i: (0, 0)), pl.BlockSpec((1, LANES), lambda i: (0, 0))]
        args += [g_ffn, wr, br]
        out_specs = [row(d), row(d), row(LANES)]
        out_shape = [out_shape, jax.ShapeDtypeStruct((n, d), F32), jax.ShapeDtypeStruct((n, LANES), F32)]
    return pl.pallas_call(
        functools.partial(_post_mix_body, n_experts=n_experts),
        grid=(n // tm,),
        in_specs=in_specs,
        out_specs=out_specs,
        out_shape=out_shape,
        compiler_params=_compiler_params(("parallel",)),
        name="post_mix",
    )(*args)


def _finish(x_ref, acc_ref, gf_ref, o_ref):
    y = x_ref[...] + acc_ref[...]
    o_ref[...] = y if gf_ref is None else _rms(y, gf_ref[...])


def _ffn_body(x_ref, g_ref, wg_ref, wu_ref, wd_ref, *rest, final, tf):
    gf_ref = rest[0] if final else None
    o_ref, acc_ref = rest[-2:]
    hn = _rms(x_ref[...], g_ref[...]).astype(BF16)
    for f in range(wg_ref.shape[1] // tf):
        cols = slice(f * tf, (f + 1) * tf)
        gate = jnp.dot(hn, wg_ref[:, cols], preferred_element_type=F32)
        up = jnp.dot(hn, wu_ref[:, cols], preferred_element_type=F32)
        act = (jax.nn.silu(gate) * up).astype(BF16)
        part = jnp.dot(act, wd_ref[cols, :], preferred_element_type=F32)
        if f == 0:
            acc_ref[...] = part
        else:
            acc_ref[...] += part
    _finish(x_ref, acc_ref, gf_ref, o_ref)


def _ffn(x, g, wg, wu, wd, layer, g_final, tm, tf):
    n, d = x.shape
    ff = wg.shape[2]
    final = g_final is not None
    blk = layer // 2
    in_specs = [pl.BlockSpec((tm, d), lambda i: (i, 0)),
                pl.BlockSpec((None, 1, d), lambda i: (layer, 0, 0)),
                pl.BlockSpec((None, d, ff), lambda i: (blk, 0, 0)),
                pl.BlockSpec((None, d, ff), lambda i: (blk, 0, 0)),
                pl.BlockSpec((None, ff, d), lambda i: (blk, 0, 0))]
    args = [x, g, wg, wu, wd]
    if final:
        in_specs.append(pl.BlockSpec((1, d), lambda i: (0, 0)))
        args.append(g_final.reshape(1, d))
    return pl.pallas_call(
        functools.partial(_ffn_body, final=final, tf=tf),
        grid=(n // tm,),
        in_specs=in_specs,
        out_specs=pl.BlockSpec((tm, d), lambda i: (i, 0)),
        out_shape=jax.ShapeDtypeStruct((n, d), F32),
        scratch_shapes=[pltpu.VMEM((tm, d), F32)],
        compiler_params=_compiler_params(("parallel",)),
        name="ffn_dense",
    )(*args)


def _route(hn, wr_ref, br_ref, n_experts, lane):
    logits = jnp.dot(hn, wr_ref[...], preferred_element_type=F32, precision=lax.Precision.HIGHEST) + br_ref[...]
    logits = jnp.where(lane < n_experts, logits, -jnp.inf)
    m1 = jnp.max(logits, axis=-1, keepdims=True)
    i1 = jnp.min(jnp.where(logits == m1, lane, float(LANES)), axis=-1, keepdims=True)
    rest_logits = jnp.where(lane == i1, -jnp.inf, logits)
    m2 = jnp.max(rest_logits, axis=-1, keepdims=True)
    i2 = jnp.min(jnp.where(rest_logits == m2, lane, float(LANES)), axis=-1, keepdims=True)
    e2 = jnp.exp(m2 - m1)
    return i1, i2, 1.0 / (1.0 + e2), e2 / (1.0 + e2)


def _moe_body(x_ref, g_ref, wr_ref, br_ref, wg_ref, wu_ref, wd_ref, *rest, n_experts, final):
    gf_ref = rest[0] if final else None
    o_ref, hn_ref, comb_ref, acc_ref = rest[-4:]
    e, f = pl.program_id(1), pl.program_id(2)
    lane = lax.broadcasted_iota(jnp.int32, comb_ref.shape, 1).astype(F32)

    @pl.when((e == 0) & (f == 0))
    def _():
        hn = _rms(x_ref[...], g_ref[...])
        hn_ref[...] = hn.astype(BF16)
        acc_ref[...] = jnp.zeros_like(acc_ref)
        i1, i2, g1, g2 = _route(hn, wr_ref, br_ref, n_experts, lane)
        comb_ref[...] = jnp.where(lane == i1, g1, 0.0) + jnp.where(lane == i2, g2, 0.0)

    hn = hn_ref[...]
    gate = jnp.dot(hn, wg_ref[...], preferred_element_type=F32)
    up = jnp.dot(hn, wu_ref[...], preferred_element_type=F32)
    act = (jax.nn.silu(gate) * up).astype(BF16)
    weight = jnp.sum(jnp.where(lane == e.astype(F32), comb_ref[...], 0.0), axis=-1, keepdims=True)
    acc_ref[...] += weight * jnp.dot(act, wd_ref[...], preferred_element_type=F32)

    @pl.when((e == n_experts - 1) & (f == pl.num_programs(2) - 1))
    def _():
        _finish(x_ref, acc_ref, gf_ref, o_ref)


def _moe(x, g, wr, br, wg, wu, wd, layer, g_final, tm, tf):
    n, d = x.shape
    _, n_experts, _, ff = wg.shape
    final = g_final is not None
    blk = layer // 2
    in_specs = [pl.BlockSpec((tm, d), lambda i, e, f: (i, 0)),
                pl.BlockSpec((None, 1, d), lambda i, e, f: (layer, 0, 0)),
                pl.BlockSpec((d, LANES), lambda i, e, f: (0, 0)),
                pl.BlockSpec((1, LANES), lambda i, e, f: (0, 0)),
                pl.BlockSpec((None, None, d, tf), lambda i, e, f: (blk, e, 0, f)),
                pl.BlockSpec((None, None, d, tf), lambda i, e, f: (blk, e, 0, f)),
                pl.BlockSpec((None, None, tf, d), lambda i, e, f: (blk, e, f, 0))]
    args = [x, g, wr, br, wg, wu, wd]
    if final:
        in_specs.append(pl.BlockSpec((1, d), lambda i, e, f: (0, 0)))
        args.append(g_final.reshape(1, d))
    return pl.pallas_call(
        functools.partial(_moe_body, n_experts=n_experts, final=final),
        grid=(n // tm, n_experts, ff // tf),
        in_specs=in_specs,
        out_specs=pl.BlockSpec((tm, d), lambda i, e, f: (i, 0)),
        out_shape=jax.ShapeDtypeStruct((n, d), F32),
        scratch_shapes=[pltpu.VMEM((tm, d), BF16), pltpu.VMEM((tm, LANES), F32), pltpu.VMEM((tm, d), F32)],
        compiler_params=_compiler_params(("parallel", "arbitrary", "arbitrary")),
        name="moe",
    )(*args)


EXPERT_ROW_TILE = 512
MOE_DMA_ROWS = 256


def _norm_and_route(x, g_ref, wr_ref, br_ref, hn_ref, route_ref, n_experts):
    lane = lax.broadcasted_iota(jnp.int32, route_ref.shape, 1).astype(F32)
    hn = _rms(x, g_ref[...])
    hn_ref[...] = hn
    i1, i2, g1, g2 = _route(hn, wr_ref, br_ref, n_experts, lane)
    route_ref[...] = jnp.where(lane == 0.0, i1, jnp.where(lane == 1.0, i2, jnp.where(
        lane == 2.0, g1, jnp.where(lane == 3.0, g2, 0.0))))


def _padded_router(w_router, b_router):
    d, n_experts = w_router.shape
    wr = jnp.zeros((d, LANES), F32).at[:, :n_experts].set(w_router.astype(F32))
    br = jnp.zeros((1, LANES), F32).at[0, :n_experts].set(b_router.astype(F32))
    return wr, br


def _dispatch_plan(experts, n_experts, n_tiles):
    flat = experts.reshape(-1)
    onehot = (flat[:, None] == jnp.arange(n_experts, dtype=jnp.int32)[None, :]).astype(jnp.int32)
    running = jnp.cumsum(onehot, axis=0)
    rank = jnp.sum(onehot * running, axis=1) - 1
    tiles_per_expert = (running[-1] + EXPERT_ROW_TILE - 1) // EXPERT_ROW_TILE
    tile_end = jnp.cumsum(tiles_per_expert)
    group_start = (tile_end - tiles_per_expert) * EXPERT_ROW_TILE
    slot = jnp.sum(onehot * group_start[None, :], axis=1) + rank
    tile = jnp.arange(n_tiles, dtype=jnp.int32)
    tile_expert = jnp.minimum(jnp.sum((tile[:, None] >= tile_end[None, :]).astype(jnp.int32), axis=1),
                              n_experts - 1)
    tile_valid = (tile < tile_end[-1]).astype(jnp.int32)
    return slot.astype(jnp.int32), tile_expert.astype(jnp.int32), tile_valid


def _row_copy(src_ref, src_row, dst_ref, dst_row, sem):
    return pltpu.make_async_copy(src_ref.at[pl.ds(src_row, 1)], dst_ref.at[pl.ds(dst_row, 1)], sem)


def _slot_tokens_body(slot_ref, tok_ref):
    def clear(i, c):
        for k in range(8):
            tok_ref[8 * i + k] = 0
        return c

    def fill(i, c):
        for k in range(8):
            tok_ref[slot_ref[8 * i + k]] = 4 * i + k // 2
        return c

    lax.fori_loop(0, tok_ref.shape[0] // 8, clear, 0)
    lax.fori_loop(0, slot_ref.shape[0] // 8, fill, 0)


def _slot_tokens(slot, n_slots):
    smem = pl.BlockSpec(memory_space=pltpu.SMEM)
    return pl.pallas_call(
        _slot_tokens_body,
        in_specs=[smem],
        out_specs=smem,
        out_shape=jax.ShapeDtypeStruct((n_slots,), jnp.int32),
        compiler_params=pltpu.CompilerParams(disable_bounds_checks=True),
        name="moe_slot_tokens",
    )(slot)


def _experts_body(tok_ref, te_ref, tv_ref, hn_ref, wg_ref, wu_ref, wd_ref, y_ref, xbuf_ref, sem, *, tf):
    t = pl.program_id(0)
    last = pl.num_programs(0) - 1
    rows = y_ref.shape[0]
    n_f = wg_ref.shape[1] // tf
    cur = t % 2

    def copy(tile, buf, j):
        return _row_copy(hn_ref, tok_ref[tile * rows + j], xbuf_ref.at[buf], j, sem.at[buf])

    def start_all(tile, buf):
        for j in range(rows):
            copy(tile, buf, j).start()

    def wait_all(buf):
        pltpu.make_async_copy(hn_ref.at[pl.ds(0, rows)], xbuf_ref.at[buf], sem.at[buf]).wait()

    @pl.when(t == 0)
    def _():
        start_all(0, 0)

    @pl.when((t == 0) | (tv_ref[jnp.maximum(t - 1, 0)] != 0))
    def _():
        wait_all(cur)

    live = tv_ref[t] != 0

    @pl.when(live)
    def _():
        start_all(jnp.minimum(t + 1, last), 1 - cur)

    @pl.when(live)
    def _():
        xb = xbuf_ref[cur].astype(BF16)
        for f in range(n_f):
            cols = slice(f * tf, (f + 1) * tf)
            gate = jnp.dot(xb, wg_ref[:, cols], preferred_element_type=F32)
            up = jnp.dot(xb, wu_ref[:, cols], preferred_element_type=F32)
            act = (jax.nn.silu(gate) * up).astype(BF16)
            part = jnp.dot(act, wd_ref[cols, :], preferred_element_type=F32)
            if f == 0:
                y_ref[...] = part
            else:
                y_ref[...] += part

    @pl.when(jnp.logical_not(live))
    def _():
        y_ref[...] = jnp.zeros_like(y_ref)

    @pl.when((t == last) & live)
    def _():
        wait_all(1 - cur)


def _experts(tok, tile_expert, tile_valid, hn, wg, wu, wd, layer, tf):
    d = hn.shape[1]
    ff = wg.shape[3]
    blk = layer // 2
    n_tiles = tile_expert.shape[0]
    weights = lambda r, c: pl.BlockSpec((None, None, r, c), lambda t, tok, te, tv: (blk, te[t], 0, 0))
    return pl.pallas_call(
        functools.partial(_experts_body, tf=tf),
        grid_spec=pltpu.PrefetchScalarGridSpec(
            num_scalar_prefetch=3,
            grid=(n_tiles,),
            in_specs=[pl.BlockSpec(memory_space=pl.ANY), weights(d, ff), weights(d, ff), weights(ff, d)],
            out_specs=pl.BlockSpec((EXPERT_ROW_TILE, d), lambda t, tok, te, tv: (t, 0)),
            scratch_shapes=[pltpu.VMEM((2, EXPERT_ROW_TILE, d), F32), pltpu.SemaphoreType.DMA((2,))]),
        out_shape=jax.ShapeDtypeStruct((n_tiles * EXPERT_ROW_TILE, d), F32),
        compiler_params=pltpu.CompilerParams(dimension_semantics=("arbitrary",), disable_bounds_checks=True,
                                             vmem_limit_bytes=VMEM_LIMIT_BYTES),
        name="moe_experts",
    )(tok, tile_expert, tile_valid, hn, wg, wu, wd)


def _combine_body(slot_ref, x_ref, route_ref, y_ref, *rest, final):
    gf_ref = rest[0] if final else None
    o_ref, ya_ref, yb_ref, sem = rest[-4:]
    i = pl.program_id(0)
    cur = i % 2

    def start_step(step, buf):
        base = step * MOE_DMA_ROWS

        for j in range(MOE_DMA_ROWS):
            t = base + j
            _row_copy(y_ref, slot_ref[2 * t], ya_ref.at[buf], j, sem.at[0, buf]).start()
            _row_copy(y_ref, slot_ref[2 * t + 1], yb_ref.at[buf], j, sem.at[1, buf]).start()

    @pl.when(i == 0)
    def _():
        start_step(0, 0)

    @pl.when(i + 1 < pl.num_programs(0))
    def _():
        start_step(i + 1, 1 - cur)

    for k, buf_ref in enumerate((ya_ref, yb_ref)):
        pltpu.make_async_copy(y_ref.at[pl.ds(0, MOE_DMA_ROWS)], buf_ref.at[cur], sem.at[k, cur]).wait()
    y = x_ref[...] + route_ref[:, 2:3] * ya_ref[cur] + route_ref[:, 3:4] * yb_ref[cur]
    o_ref[...] = y if gf_ref is None else _rms(y, gf_ref[...])


def _combine(slot, x, route, y, g_final):
    n, d = x.shape
    final = g_final is not None
    in_specs = [pl.BlockSpec((MOE_DMA_ROWS, d), lambda i, s: (i, 0)),
                pl.BlockSpec((MOE_DMA_ROWS, LANES), lambda i, s: (i, 0)),
                pl.BlockSpec(memory_space=pl.ANY)]
    args = [slot, x, route, y]
    if final:
        in_specs.append(pl.BlockSpec((1, d), lambda i, s: (0, 0)))
        args.append(g_final.reshape(1, d))
    return pl.pallas_call(
        functools.partial(_combine_body, final=final),
        grid_spec=pltpu.PrefetchScalarGridSpec(
            num_scalar_prefetch=1,
            grid=(n // MOE_DMA_ROWS,),
            in_specs=in_specs,
            out_specs=pl.BlockSpec((MOE_DMA_ROWS, d), lambda i, s: (i, 0)),
            scratch_shapes=[pltpu.VMEM((2, MOE_DMA_ROWS, d), F32), pltpu.VMEM((2, MOE_DMA_ROWS, d), F32),
                            pltpu.SemaphoreType.DMA((2, 2))]),
        out_shape=jax.ShapeDtypeStruct((n, d), F32),
        compiler_params=pltpu.CompilerParams(dimension_semantics=("arbitrary",), disable_bounds_checks=True,
                                             vmem_limit_bytes=VMEM_LIMIT_BYTES),
        name="moe_combine",
    )(*args)


def _moe_sorted(x, hn, route, wg, wu, wd, layer, g_final, tf):
    n = x.shape[0]
    n_experts = wg.shape[1]
    n_tiles = 2 * n // EXPERT_ROW_TILE + n_experts
    slot, tile_expert, tile_valid = _dispatch_plan(route[:, :2].astype(jnp.int32), n_experts, n_tiles)
    tok = _slot_tokens(slot, n_tiles * EXPERT_ROW_TILE)
    y = _experts(tok, tile_expert, tile_valid, hn, wg, wu, wd, layer, tf)
    return _combine(slot, x, route, y, g_final)


PROMPT_ROW_TILE = 512
PROMPT_SCAN_STEPS = 128


def kernel(x_prompt, x_sample, cache_win_k, cache_win_v, state_ssm_re, state_ssm_im, rel_bias, norm_mix, w_in, ssm_a_re, ssm_a_im, ssm_log_dt, ssm_b_re, ssm_b_im, ssm_c_re, ssm_c_im, ssm_d, ssm_w_glu, ssm_b_glu, norm_attn_out, norm_ssm_out, w_out, norm_ffn, ffn_w_gate, ffn_w_up, ffn_w_down, moe_w_router, moe_b_router, moe_w_gate, moe_w_up, moe_w_down, norm_final):
    batch, seq, d_model = x_prompt.shape
    dec_batch, dec_seq, _ = x_sample.shape
    depth = w_in.shape[0]
    win, n_heads = cache_win_k.shape[2:4]
    attn_width = n_heads * HEAD_DIM
    n_groups, n_state = ssm_a_re.shape[1:]
    ssm_width = n_groups * SSM_GROUP
    assert w_in.shape[2] == 3 * attn_width + ssm_width and seq == MAX_WINDOW
    dec_rows = dec_batch * dec_seq

    band_bias = _band_bias(rel_bias)
    sample_tabs = _sample_tables(rel_bias, win, dec_seq)
    ar, ai, bb_re, bb_im = _s5_params(ssm_a_re, ssm_a_im, ssm_log_dt, ssm_b_re, ssm_b_im)
    cache_k = jnp.transpose(cache_win_k, (0, 1, 3, 4, 2))
    cache_v = jnp.transpose(cache_win_v, (0, 1, 3, 4, 2))

    rows3 = lambda a: a.reshape(a.shape[0], 1, a.shape[1])
    norm_mix3, norm_ffn3, b_glu3 = rows3(norm_mix), rows3(norm_ffn), rows3(ssm_b_glu)
    g_attn3, g_ssm3 = rows3(norm_attn_out), rows3(norm_ssm_out)
    w_in_b, w_out_b, w_glu_b = w_in.astype(BF16), w_out.astype(BF16), ssm_w_glu.astype(BF16)
    ffn_b = [w.astype(BF16) for w in (ffn_w_gate, ffn_w_up, ffn_w_down)]
    moe_b = [w.astype(BF16) for w in (moe_w_gate, moe_w_up, moe_w_down)]
    tf = ffn_w_gate.shape[2] // 2

    def mix_and_ffn(x, l, attn, y, tm, y_tiles_per_seq):
        g_final = norm_final if l == depth - 1 else None
        mix = functools.partial(_post_mix, attn, y, x, g_attn3, g_ssm3, w_glu_b, b_glu3, w_out_b, l, tm,
                                y_tiles_per_seq)
        if l % 2 == 0:
            return _ffn(mix(), norm_ffn3, *ffn_b, l, g_final, tm, tf)
        n_experts = moe_w_gate.shape[1]
        wr, br = _padded_router(moe_w_router[l // 2], moe_b_router[l // 2])
        if 2 * x.shape[0] >= n_experts * EXPERT_ROW_TILE:
            x, hn, route = mix(router=(norm_ffn3, wr, br, n_experts))
            return _moe_sorted(x, hn, route, *moe_b, l, g_final, tf)
        return _moe(mix(), norm_ffn3, wr, br, *moe_b, l, g_final, tm, tf)

    xp = x_prompt.reshape(batch * seq, d_model)
    xs = x_sample.reshape(dec_rows, d_model)
    zero_state = jnp.zeros((batch, 2 * n_groups * n_state), F32)
    k_all = jnp.zeros((depth, batch, attn_width, seq), F32)
    v_all = jnp.zeros((depth, batch, attn_width, seq), F32)
    outs = [[] for _ in range(8)]
    for l in range(depth):
        wx, wc = _pack_s5_weights(bb_re[l], bb_im[l], ssm_c_re[l].astype(F32), ssm_c_im[l].astype(F32))

        qkv, k_all, v_all, u = _norm_proj_prompt(xp, norm_mix3, w_in_b, l, k_all, v_all, PROMPT_ROW_TILE)
        attn = _prompt_attn(qkv, band_bias, batch, seq, attn_width)
        y, h_fin = _s5_scan(u, wx, wc, ar[l], ai[l], ssm_d[l], zero_state, batch, PROMPT_SCAN_STEPS, True)
        xp = mix_and_ffn(xp, l, attn, y, PROMPT_ROW_TILE, seq // PROMPT_ROW_TILE)
        h_re, h_im = _unpack_state(h_fin, n_groups, n_state)
        outs[2].append(h_re)
        outs[3].append(h_im)

        proj = _norm_proj(xs, norm_mix3, w_in_b, l, dec_rows)
        q, k_new, v_new = (proj[:, i * attn_width:(i + 1) * attn_width].reshape(dec_batch, dec_seq, n_heads, HEAD_DIM)
                           for i in range(3))
        attn = _sample_attn(q, k_new, v_new, cache_k, cache_v, l, sample_tabs).reshape(dec_rows, attn_width)
        u = proj[:, 3 * attn_width:].reshape(dec_batch, dec_seq, ssm_width).transpose(1, 0, 2)
        y, h_fin = _s5_scan(u.reshape(dec_rows, ssm_width), wx, wc, ar[l], ai[l], ssm_d[l],
                            _pack_state(state_ssm_re[l], state_ssm_im[l]), dec_batch, dec_seq, False)
        y = y.reshape(dec_seq, dec_batch, ssm_width).transpose(1, 0, 2).reshape(dec_rows, ssm_width)
        xs = mix_and_ffn(xs, l, attn, y, dec_rows, None)
        outs[4].append(k_new)
        outs[5].append(v_new)
        h_re, h_im = _unpack_state(h_fin, n_groups, n_state)
        outs[6].append(h_re)
        outs[7].append(h_im)

    def window_out(a):
        return jnp.transpose(a.reshape(depth, batch, n_heads, HEAD_DIM, seq), (0, 1, 4, 2, 3))

    return (xp.reshape(batch, seq, d_model), xs.reshape(dec_batch, dec_seq, d_model),
            window_out(k_all), window_out(v_all), *[jnp.stack(o, 0) for o in outs[2:]])
```

```python
import functools

import numpy as np
import jax
import jax.numpy as jnp
from jax import lax
from jax.experimental import pallas as pl
from jax.experimental.pallas import tpu as pltpu

F32 = jnp.float32
BF16 = jnp.bfloat16

HEAD_DIM = 64
DILATION_PATTERNS = ((128, 1), (512, 4), (2048, 16))
MAX_WINDOW = 2048
N_BUCKETS = 32
SSM_GROUP = 16
SSM_STATE = 64
RMS_EPS = 1e-6
NEG_BIG = -1e30
LOG2_E = 1.4426950408889634

LANES = 128
Q_TILE = 128
ATTN_UNROLL = 4
GROUPS_PER_TILE = LANES // SSM_GROUP
STATE_TILE = GROUPS_PER_TILE * SSM_STATE
VMEM_LIMIT_BYTES = 56 * 1024 * 1024


def _compiler_params(semantics):
    return pltpu.CompilerParams(dimension_semantics=semantics, vmem_limit_bytes=VMEM_LIMIT_BYTES)


def _rms(x, g):
    return x * lax.rsqrt(jnp.mean(x * x, axis=-1, keepdims=True) + RMS_EPS) * g


def _t5_causal_bucket(dist):
    max_exact = N_BUCKETS // 2
    d = np.asarray(dist, dtype=np.int64)
    large = max_exact + (np.log(np.maximum(d, max_exact) / max_exact)
                         / np.log(MAX_WINDOW / max_exact)
                         * (N_BUCKETS - max_exact)).astype(np.int64)
    large = np.minimum(large, N_BUCKETS - 1)
    return np.where(d < max_exact, d, large).astype(np.int32)


def _multiplicity(d):
    m = np.zeros(d.shape, np.int32)
    for window, dil in DILATION_PATTERNS:
        m += ((d >= 0) & (d <= window) & (d % dil == 0)).astype(np.int32)
    return m


def _norm_proj_body(x_ref, g_ref, w_ref, o_ref):
    hn = _rms(x_ref[...], g_ref[...])
    o_ref[...] = jnp.dot(hn.astype(BF16), w_ref[...], preferred_element_type=F32)


def _norm_proj(x, g, w, layer, tm):
    n, d = x.shape
    c = w.shape[2]
    return pl.pallas_call(
        _norm_proj_body,
        grid=(n // tm,),
        in_specs=[pl.BlockSpec((tm, d), lambda i: (i, 0)),
                  pl.BlockSpec((None, 1, d), lambda i: (layer, 0, 0)),
                  pl.BlockSpec((None, d, c), lambda i: (layer, 0, 0))],
        out_specs=pl.BlockSpec((tm, c), lambda i: (i, 0)),
        out_shape=jax.ShapeDtypeStruct((n, c), F32),
        compiler_params=_compiler_params(("parallel",)),
        name="norm_proj",
    )(x, g, w)


def _norm_proj_prompt_body(x_ref, g_ref, w_ref, _k_all, _v_all, qkv_ref, kt_ref, vt_ref, u_ref, *, attn_width):
    hn = _rms(x_ref[...], g_ref[...])
    proj = jnp.dot(hn.astype(BF16), w_ref[...], preferred_element_type=F32)
    qkv_ref[...] = proj[:, :3 * attn_width]
    kt_ref[...] = proj[:, attn_width:2 * attn_width].T
    vt_ref[...] = proj[:, 2 * attn_width:3 * attn_width].T
    u_ref[...] = proj[:, 3 * attn_width:]


def _norm_proj_prompt(x, g, w, layer, k_all, v_all, tm):
    n, d = x.shape
    c = w.shape[2]
    _, batch, attn_width, seq = k_all.shape
    ch = c - 3 * attn_width
    per_seq = seq // tm
    window = pl.BlockSpec((None, None, attn_width, tm), lambda i: (layer, i // per_seq, 0, i % per_seq))
    untouched = pl.BlockSpec(memory_space=pl.ANY)
    return pl.pallas_call(
        functools.partial(_norm_proj_prompt_body, attn_width=attn_width),
        grid=(n // tm,),
        in_specs=[pl.BlockSpec((tm, d), lambda i: (i, 0)),
                  pl.BlockSpec((None, 1, d), lambda i: (layer, 0, 0)),
                  pl.BlockSpec((None, d, c), lambda i: (layer, 0, 0)),
                  untouched, untouched],
        out_specs=[pl.BlockSpec((tm, 3 * attn_width), lambda i: (i, 0)), window, window,
                   pl.BlockSpec((tm, ch), lambda i: (i % per_seq, i // per_seq))],
        out_shape=[jax.ShapeDtypeStruct((n, 3 * attn_width), F32),
                   jax.ShapeDtypeStruct(k_all.shape, F32),
                   jax.ShapeDtypeStruct(v_all.shape, F32),
                   jax.ShapeDtypeStruct((seq, batch * ch), F32)],
        input_output_aliases={3: 1, 4: 2},
        compiler_params=_compiler_params(("parallel",)),
        name="norm_proj_prompt",
    )(x, g, w, k_all, v_all)


def _band_bias(rel_bias):
    period = 3 * Q_TILE
    n_heads = rel_bias.shape[1]
    rows0 = []
    for window, dil in DILATION_PATTERNS:
        assert window // dil == Q_TILE
        vec = rel_bias[_t5_causal_bucket((Q_TILE - np.arange(Q_TILE + 1)) * dil)].astype(F32)
        rows0.append(jnp.concatenate([vec.T, jnp.full((n_heads, period - Q_TILE - 1), NEG_BIG, F32)], axis=1))
    row0 = jnp.stack(rows0, 0)
    flat = jnp.tile(row0, (1, 1, Q_TILE))[:, :, :Q_TILE * (period - 1)]
    band = flat.reshape(len(DILATION_PATTERNS), n_heads, Q_TILE, period - 1)[:, :, :, :2 * Q_TILE]
    return band * LOG2_E


def _prompt_attn_body(q_ref, k_ref, v_ref, bias_ref, o_ref, op_ref, m_ref, l_ref):
    seq = q_ref.shape[0]
    lane = lax.broadcasted_iota(jnp.int32, (Q_TILE, LANES), 1)
    head0 = lane < HEAD_DIM
    scale = HEAD_DIM ** -0.5 * LOG2_E

    def rows(start, size, dil):
        return pl.ds(start, size) if dil == 1 else pl.ds(start, size, stride=dil)

    def tiles(p, dil, starts, first):
        nk = Q_TILE if first else 2 * Q_TILE
        bias = (bias_ref[p, :, :, Q_TILE:] if first else bias_ref[p]).reshape(2 * Q_TILE, nk)
        work = []
        for q_start, k_start in starts:
            qsl, ksl = rows(q_start, Q_TILE, dil), rows(k_start, nk, dil)
            qt = q_ref[qsl, :] * scale
            q2 = jnp.concatenate([jnp.where(head0, qt, 0.0), jnp.where(head0, 0.0, qt)], axis=0).astype(BF16)
            v_ones = jnp.concatenate([v_ref[ksl, :].astype(BF16), jnp.ones((nk, LANES), BF16)], axis=1)
            work.append([qsl, v_ones, q2, k_ref[ksl, :].astype(BF16)])
        for w in work:
            w[2] = lax.dot_general(w[2], w[3], (((1,), (1,)), ((), ())), preferred_element_type=F32) + bias
        for w in work:
            w[3] = jnp.max(w[2], axis=-1, keepdims=True)
        for w in work:
            w[2] = jnp.exp2((w[2] - w[3]).astype(BF16))
        for w in work:
            w[2] = jnp.dot(w[2], w[1], preferred_element_type=F32)
        for qsl, _, ol, m in work:
            op_ref[p, qsl, :] = jnp.where(head0, ol[:Q_TILE, :LANES], ol[Q_TILE:, :LANES])
            m_ref[p, qsl, :] = jnp.where(head0, m[:Q_TILE], m[Q_TILE:])
            l_ref[p, qsl, :] = jnp.where(head0, ol[:Q_TILE, LANES:], ol[Q_TILE:, LANES:])

    for p, (_, dil) in enumerate(DILATION_PATTERNS):
        n_tiles = seq // dil // Q_TILE
        stride = dil * Q_TILE
        unroll = min(ATTN_UNROLL, dil)
        if dil == 1:
            unroll = max(u for u in range(1, 2 * ATTN_UNROLL + 1) if (n_tiles - 1) % u == 0)
            for r in range(dil):
                tiles(p, dil, [(r, r)], True)

                def blk(t, c, p=p, dil=dil, r=r, stride=stride, unroll=unroll):
                    first_t = 1 + t * unroll
                    tiles(p, dil, [(r + stride * (first_t + j), r + stride * (first_t + j - 1))
                                   for j in range(unroll)], False)
                    return c
                lax.fori_loop(0, (n_tiles - 1) // unroll, blk, 0)
        else:
            assert dil % unroll == 0

            def classes(g, c, p=p, dil=dil, n_tiles=n_tiles, stride=stride, unroll=unroll):
                rs = [g * unroll + j for j in range(unroll)]
                tiles(p, dil, [(r, r) for r in rs], True)
                if n_tiles > 1:
                    def blk(t, c2):
                        tiles(p, dil, [(r + stride * t, r + stride * (t - 1)) for r in rs], False)
                        return c2
                    lax.fori_loop(1, n_tiles, blk, 0)
                return c
            lax.fori_loop(0, dil // unroll, classes, 0)

    def combine(t, carry):
        r = pl.ds(pl.multiple_of(t * Q_TILE, Q_TILE), Q_TILE)
        ms = [m_ref[p, r, :] for p in range(len(DILATION_PATTERNS))]
        mx = functools.reduce(jnp.maximum, ms)
        ws = [jnp.exp2(m - mx) for m in ms]
        num = sum(w * op_ref[p, r, :] for p, w in enumerate(ws))
        den = sum(w * l_ref[p, r, :] for p, w in enumerate(ws))
        o_ref[r, :] = num / den
        return carry

    lax.fori_loop(0, seq // Q_TILE, combine, 0)


def _prompt_attn(proj, bias, batch, seq, attn_width):
    n_pairs = attn_width // LANES
    n_pat = len(DILATION_PATTERNS)
    assert seq % (Q_TILE * max(d for _, d in DILATION_PATTERNS)) == 0
    return pl.pallas_call(
        _prompt_attn_body,
        grid=(batch, n_pairs),
        in_specs=[pl.BlockSpec((seq, LANES), lambda b, h: (b, h)),
                  pl.BlockSpec((seq, LANES), lambda b, h: (b, n_pairs + h)),
                  pl.BlockSpec((seq, LANES), lambda b, h: (b, 2 * n_pairs + h)),
                  pl.BlockSpec((n_pat, 2, Q_TILE, 2 * Q_TILE), lambda b, h: (0, h, 0, 0))],
        out_specs=pl.BlockSpec((seq, LANES), lambda b, h: (b, h)),
        out_shape=jax.ShapeDtypeStruct((batch * seq, attn_width), F32),
        scratch_shapes=[pltpu.VMEM((n_pat, seq, LANES), F32)] * 3,
        compiler_params=_compiler_params(("parallel", "parallel")),
        name="prompt_attn",
    )(proj, proj, proj, bias)


def _sample_tables(rel_bias, win, n_new):
    n_heads = rel_bias.shape[1]
    vec_t = rel_bias[_t5_causal_bucket(np.arange(win + n_new))].astype(F32).T
    flipped = vec_t[:, ::-1]
    t = np.arange(n_new)
    d_cache = win + t[:, None] - np.arange(win)[None, :]
    d_new = t[:, None] - t[None, :]
    b_cache = jnp.stack([flipped[:, n_new - 1 - i:n_new - 1 - i + win] for i in range(n_new)], 0)
    b_new = jnp.transpose(vec_t[:, np.maximum(d_new, 0)], (1, 0, 2))
    out = []
    for b, d in ((b_cache, d_cache), (b_new, d_new)):
        mult = _multiplicity(d)
        b = jnp.where((mult > 0)[:, None, :], b, NEG_BIG).reshape(n_new * n_heads, d.shape[1])
        m = np.broadcast_to(np.maximum(mult, 1)[:, None, :], (n_new, n_heads, d.shape[1]))
        out += [b, jnp.asarray(m.reshape(n_new * n_heads, d.shape[1]), F32)]
    return out


def _sample_attn_body(q_ref, kc_ref, kn_ref, vc_ref, vn_ref, bc_ref, mc_ref, bn_ref, mn_ref, hm_ref, o_ref):
    n_new = o_ref.shape[0]
    win = kc_ref.shape[-1]
    qb = (q_ref[...] * HEAD_DIM ** -0.5).astype(BF16)
    s_c = jnp.dot(qb, kc_ref[...].reshape(-1, win).astype(BF16), preferred_element_type=F32) + bc_ref[...]
    s_n = jnp.dot(qb, kn_ref[...].astype(BF16), preferred_element_type=F32) + bn_ref[...]
    m = jnp.maximum(jnp.max(s_c, axis=-1, keepdims=True), jnp.max(s_n, axis=-1, keepdims=True))
    e_c = jnp.exp(s_c - m) * mc_ref[...]
    e_n = jnp.exp(s_n - m) * mn_ref[...]
    den = jnp.sum(e_c, axis=-1, keepdims=True) + jnp.sum(e_n, axis=-1, keepdims=True)
    nt = (((1,), (1,)), ((), ()))
    o = (lax.dot_general(e_c.astype(BF16), vc_ref[...].reshape(-1, win).astype(BF16), nt,
                         preferred_element_type=F32)
         + lax.dot_general(e_n.astype(BF16), vn_ref[...].astype(BF16), nt, preferred_element_type=F32))
    o = o / den * hm_ref[...]
    o_ref[...] = jnp.sum(o.reshape(n_new, -1, o.shape[-1]), axis=1)


def _sample_attn(q, k_new, v_new, cache_k, cache_v, layer, tables):
    depth, batch, n_heads, hd, win = cache_k.shape
    n_new = q.shape[1]
    width = n_heads * hd
    rows = n_new * n_heads
    eye = jnp.eye(n_heads, dtype=F32)
    q_bd = jnp.einsum('bthd,hg->btghd', q, eye).reshape(batch, rows, width)
    kn = jnp.transpose(k_new, (0, 2, 3, 1)).reshape(batch, width, n_new)
    vn = jnp.transpose(v_new, (0, 2, 3, 1)).reshape(batch, width, n_new)
    head_mask = jnp.asarray(np.kron(np.tile(np.eye(n_heads), (n_new, 1)), np.ones((1, hd))), F32)
    cache_spec = pl.BlockSpec((None, None, n_heads, hd, win), lambda b: (layer, b, 0, 0, 0))
    new_spec = pl.BlockSpec((None, width, n_new), lambda b: (b, 0, 0))
    const = lambda a: pl.BlockSpec(a.shape, lambda b: (0, 0))
    return pl.pallas_call(
        _sample_attn_body,
        grid=(batch,),
        in_specs=[pl.BlockSpec((None, rows, width), lambda b: (b, 0, 0)),
                  cache_spec, new_spec, cache_spec, new_spec] + [const(t) for t in tables] + [const(head_mask)],
        out_specs=pl.BlockSpec((None, n_new, width), lambda b: (b, 0, 0)),
        out_shape=jax.ShapeDtypeStruct((batch, n_new, width), F32),
        compiler_params=_compiler_params(("parallel",)),
        name="sample_attn",
    )(q_bd, cache_k, kn, cache_v, vn, *tables, head_mask)


def _s5_param_body(are_ref, aim_ref, ldt_ref, bre_ref, bim_ref, ar_ref, ai_ref, bbre_ref, bbim_ref):
    a_re, a_im = are_ref[...], aim_ref[...]
    dt = jnp.exp(ldt_ref[...])
    mag = jnp.exp(dt * a_re)
    ar, ai = mag * jnp.cos(dt * a_im), mag * jnp.sin(dt * a_im)
    nr, ni = ar - 1.0, ai
    den = a_re * a_re + a_im * a_im
    fr = (nr * a_re + ni * a_im) / den
    fi = (ni * a_re - nr * a_im) / den
    br, bi = bre_ref[...], bim_ref[...]
    ar_ref[...] = ar
    ai_ref[...] = ai
    bbre_ref[...] = fr[:, None, :] * br - fi[:, None, :] * bi
    bbim_ref[...] = fr[:, None, :] * bi + fi[:, None, :] * br


def _s5_params(a_re, a_im, log_dt, b_re, b_im):
    depth, g, p = a_re.shape
    c = b_re.shape[-1]
    n = depth * g
    full = lambda shape: pl.BlockSpec(shape, lambda: (0,) * len(shape))
    ar, ai, bbre, bbim = pl.pallas_call(
        _s5_param_body,
        in_specs=[full((n, p)), full((n, p)), full((n, 1)), full((n, c, p)), full((n, c, p))],
        out_specs=[full((n, p)), full((n, p)), full((n, c, p)), full((n, c, p))],
        out_shape=[jax.ShapeDtypeStruct((n, p), F32), jax.ShapeDtypeStruct((n, p), F32),
                   jax.ShapeDtypeStruct((n, c, p), F32), jax.ShapeDtypeStruct((n, c, p), F32)],
        name="s5_params",
    )(a_re.reshape(n, p), a_im.reshape(n, p), log_dt.reshape(n, 1),
      jnp.swapaxes(b_re, -1, -2).reshape(n, c, p), jnp.swapaxes(b_im, -1, -2).reshape(n, c, p))
    return (ar.reshape(depth, g, p), ai.reshape(depth, g, p),
            bbre.reshape(depth, g, c, p), bbim.reshape(depth, g, c, p))


def _pack_s5_weights(bb_re, bb_im, c_re, c_im):
    depth, g, c, p = bb_re.shape
    tiles = g // GROUPS_PER_TILE
    eye = jnp.eye(GROUPS_PER_TILE, dtype=F32)

    def inp(w):
        w = w.astype(F32).reshape(depth, tiles, GROUPS_PER_TILE, c, p)
        return jnp.einsum('ljgcp,gh->ljgchp', w, eye).reshape(depth, tiles, LANES, STATE_TILE)

    def outp(w):
        w = w.astype(F32).reshape(depth, tiles, GROUPS_PER_TILE, c, p)
        return jnp.einsum('ljgcp,gh->ljgphc', w, eye).reshape(depth, tiles, STATE_TILE, LANES)

    wx = jnp.concatenate([inp(bb_re), inp(bb_im)], axis=-1).astype(BF16)
    wc = jnp.concatenate([outp(c_re), -outp(c_im)], axis=2).astype(BF16)
    return wx, wc


def _pack_state(h_re, h_im):
    lead, (g, p) = h_re.shape[:-2], h_re.shape[-2:]
    tiles = g // GROUPS_PER_TILE
    st = jnp.stack([h_re.reshape(*lead, tiles, STATE_TILE), h_im.reshape(*lead, tiles, STATE_TILE)], axis=-2)
    return st.reshape(*lead, tiles * 2 * STATE_TILE).astype(F32)


def _unpack_state(h, g, p):
    lead = h.shape[:-1]
    st = h.reshape(*lead, g // GROUPS_PER_TILE, 2, STATE_TILE)
    return st[..., 0, :].reshape(*lead, g, p), st[..., 1, :].reshape(*lead, g, p)


def _s5_scan_body(u_ref, wx_ref, wc_ref, ar_ref, ai_ref, d_ref, h0_ref, y_ref, hout_ref, us_ref, ys_ref, xs_ref,
                  h_ref, *, bsz, steps, batch_on_lanes):
    tiles = wx_ref.shape[0]
    ch = tiles * LANES

    @pl.when(pl.program_id(0) == 0)
    def _():
        h_ref[...] = h0_ref[...]

    for j in range(tiles):
        if batch_on_lanes:
            for b in range(bsz):
                us_ref[j, pl.ds(b, steps, stride=bsz), :] = u_ref[:, ch * b + LANES * j:ch * b + LANES * (j + 1)]
        else:
            us_ref[j] = u_ref[:, LANES * j:LANES * (j + 1)]

    for j in range(tiles):
        xs_ref[:, 2 * STATE_TILE * j:2 * STATE_TILE * (j + 1)] = jnp.dot(
            us_ref[j].astype(BF16), wx_ref[j], preferred_element_type=F32)

    for j in range(tiles):
        re = slice(2 * STATE_TILE * j, 2 * STATE_TILE * j + STATE_TILE)
        im = slice(2 * STATE_TILE * j + STATE_TILE, 2 * STATE_TILE * (j + 1))
        a_r = jnp.broadcast_to(ar_ref[:, STATE_TILE * j:STATE_TILE * (j + 1)], (bsz, STATE_TILE))
        a_i = jnp.broadcast_to(ai_ref[:, STATE_TILE * j:STATE_TILE * (j + 1)], (bsz, STATE_TILE))

        def step(t, carry, re=re, im=im, a_r=a_r, a_i=a_i):
            hr, hi = carry
            r = pl.ds(pl.multiple_of(t * bsz, bsz), bsz)
            nhr = a_r * hr - a_i * hi + xs_ref[r, re]
            nhi = a_r * hi + a_i * hr + xs_ref[r, im]
            xs_ref[r, re] = nhr
            xs_ref[r, im] = nhi
            return nhr, nhi

        hr, hi = lax.fori_loop(0, steps, step, (h_ref[:, re], h_ref[:, im]))
        h_ref[:, re] = hr
        h_ref[:, im] = hi

    for j in range(tiles):
        cols = slice(LANES * j, LANES * (j + 1))
        hb = xs_ref[:, 2 * STATE_TILE * j:2 * STATE_TILE * (j + 1)].astype(BF16)
        y = jnp.dot(hb, wc_ref[j], preferred_element_type=F32) + d_ref[:, cols] * us_ref[j]
        if batch_on_lanes:
            ys_ref[j] = y
            for b in range(bsz):
                y_ref[:, ch * b + LANES * j:ch * b + LANES * (j + 1)] = ys_ref[j, pl.ds(b, steps, stride=bsz), :]
        else:
            y_ref[:, cols] = y
    hout_ref[...] = h_ref[...]


def _s5_scan(u, wx, wc, ar, ai, d_skip, h0, layer, h0_layer, bsz, steps, batch_on_lanes):
    tiles = wx.shape[1]
    ch = tiles * LANES
    width = tiles * 2 * STATE_TILE
    rows = steps * bsz
    if batch_on_lanes:
        n_chunks = u.shape[0] // steps
        io_spec = pl.BlockSpec((steps, bsz * ch), lambda i: (i, 0))
    else:
        n_chunks = 1
        assert u.shape == (rows, ch)
        io_spec = pl.BlockSpec((rows, ch), lambda i: (0, 0))
    of_layer = lambda a: pl.BlockSpec((None,) + a.shape[1:], lambda i: (layer,) + (0,) * (a.ndim - 1))
    return pl.pallas_call(
        functools.partial(_s5_scan_body, bsz=bsz, steps=steps, batch_on_lanes=batch_on_lanes),
        grid=(n_chunks,),
        in_specs=[io_spec, of_layer(wx), of_layer(wc), of_layer(ar), of_layer(ai), of_layer(d_skip),
                  pl.BlockSpec((None, bsz, width), lambda i: (h0_layer, 0, 0))],
        out_specs=[io_spec, pl.BlockSpec((bsz, width), lambda i: (0, 0))],
        out_shape=[jax.ShapeDtypeStruct(u.shape, F32), jax.ShapeDtypeStruct((bsz, width), F32)],
        scratch_shapes=[pltpu.VMEM((tiles, rows, LANES), F32), pltpu.VMEM((tiles, rows, LANES), F32),
                        pltpu.VMEM((rows, width), F32), pltpu.VMEM((bsz, width), F32)],
        compiler_params=_compiler_params(("arbitrary",)),
        name="s5_scan",
    )(u, wx, wc, ar, ai, d_skip, h0)


def _post_mix_body(attn_ref, y_ref, x_ref, ga_ref, gs_ref, wglu_ref, bglu_ref, wo_ref, o_ref):
    aw = attn_ref.shape[1]
    an = _rms(attn_ref[...], ga_ref[...])
    z = jax.nn.gelu(y_ref[...])
    gate = jax.nn.sigmoid(jnp.dot(z.astype(BF16), wglu_ref[...], preferred_element_type=F32) + bglu_ref[...])
    sn = _rms(z * gate, gs_ref[...])
    o_ref[...] = (x_ref[...]
                  + jnp.dot(an.astype(BF16), wo_ref[:aw, :], preferred_element_type=F32)
                  + jnp.dot(sn.astype(BF16), wo_ref[aw:, :], preferred_element_type=F32))


def _post_mix(attn, y, x, g_attn, g_ssm, w_glu, b_glu, w_out, layer, tm, y_tiles_per_seq):
    n, d = x.shape
    aw, sw = attn.shape[1], w_glu.shape[1]
    row = lambda w: pl.BlockSpec((tm, w), lambda i: (i, 0))
    vec = lambda w: pl.BlockSpec((None, 1, w), lambda i: (layer, 0, 0))
    mat = lambda r, c: pl.BlockSpec((None, r, c), lambda i: (layer, 0, 0))
    if y_tiles_per_seq is None:
        y_spec = row(sw)
    else:
        y_spec = pl.BlockSpec((tm, sw), lambda i: (i % y_tiles_per_seq, i // y_tiles_per_seq))
    return pl.pallas_call(
        _post_mix_body,
        grid=(n // tm,),
        in_specs=[row(aw), y_spec, row(d), vec(aw), vec(sw), mat(sw, sw), vec(sw), mat(aw + sw, d)],
        out_specs=row(d),
        out_shape=jax.ShapeDtypeStruct((n, d), F32),
        compiler_params=_compiler_params(("parallel",)),
        name="post_mix",
    )(attn, y, x, g_attn, g_ssm, w_glu, b_glu, w_out)


def _finish(x_ref, acc_ref, gf_ref, o_ref):
    y = x_ref[...] + acc_ref[...]
    o_ref[...] = y if gf_ref is None else _rms(y, gf_ref[...])


def _ffn_body(x_ref, g_ref, wg_ref, wu_ref, wd_ref, *rest, final, tf):
    gf_ref = rest[0] if final else None
    o_ref, acc_ref = rest[-2:]
    hn = _rms(x_ref[...], g_ref[...]).astype(BF16)
    for f in range(wg_ref.shape[1] // tf):
        cols = slice(f * tf, (f + 1) * tf)
        gate = jnp.dot(hn, wg_ref[:, cols], preferred_element_type=F32)
        up = jnp.dot(hn, wu_ref[:, cols], preferred_element_type=F32)
        act = (jax.nn.silu(gate) * up).astype(BF16)
        part = jnp.dot(act, wd_ref[cols, :], preferred_element_type=F32)
        if f == 0:
            acc_ref[...] = part
        else:
            acc_ref[...] += part
    _finish(x_ref, acc_ref, gf_ref, o_ref)


def _ffn(x, g, wg, wu, wd, layer, g_final, tm, tf):
    n, d = x.shape
    ff = wg.shape[2]
    final = g_final is not None
    blk = layer // 2
    in_specs = [pl.BlockSpec((tm, d), lambda i: (i, 0)),
                pl.BlockSpec((None, 1, d), lambda i: (layer, 0, 0)),
                pl.BlockSpec((None, d, ff), lambda i: (blk, 0, 0)),
                pl.BlockSpec((None, d, ff), lambda i: (blk, 0, 0)),
                pl.BlockSpec((None, ff, d), lambda i: (blk, 0, 0))]
    args = [x, g, wg, wu, wd]
    if final:
        in_specs.append(pl.BlockSpec((1, d), lambda i: (0, 0)))
        args.append(g_final.reshape(1, d))
    return pl.pallas_call(
        functools.partial(_ffn_body, final=final, tf=tf),
        grid=(n // tm,),
        in_specs=in_specs,
        out_specs=pl.BlockSpec((tm, d), lambda i: (i, 0)),
        out_shape=jax.ShapeDtypeStruct((n, d), F32),
        scratch_shapes=[pltpu.VMEM((tm, d), F32)],
        compiler_params=_compiler_params(("parallel",)),
        name="ffn_dense",
    )(*args)


def _route(hn, wr_ref, br_ref, n_experts, lane):
    w = wr_ref[...]
    hn_hi, w_hi = hn.astype(BF16), w.astype(BF16)
    hn_lo, w_lo = (hn - hn_hi.astype(F32)).astype(BF16), (w - w_hi.astype(F32)).astype(BF16)
    dot = functools.partial(jnp.dot, preferred_element_type=F32)
    logits = dot(hn_hi, w_hi) + dot(hn_hi, w_lo) + dot(hn_lo, w_hi) + br_ref[...]
    logits = jnp.where(lane < n_experts, logits, -jnp.inf)
    m1 = jnp.max(logits, axis=-1, keepdims=True)
    i1 = jnp.min(jnp.where(logits == m1, lane, float(LANES)), axis=-1, keepdims=True)
    rest_logits = jnp.where(lane == i1, -jnp.inf, logits)
    m2 = jnp.max(rest_logits, axis=-1, keepdims=True)
    i2 = jnp.min(jnp.where(rest_logits == m2, lane, float(LANES)), axis=-1, keepdims=True)
    e2 = jnp.exp(m2 - m1)
    return i1, i2, 1.0 / (1.0 + e2), e2 / (1.0 + e2)


def _moe_body(x_ref, g_ref, wr_ref, br_ref, wg_ref, wu_ref, wd_ref, *rest, n_experts, final):
    gf_ref = rest[0] if final else None
    o_ref, hn_ref, comb_ref, acc_ref = rest[-4:]
    e, f = pl.program_id(1), pl.program_id(2)
    lane = lax.broadcasted_iota(jnp.int32, comb_ref.shape, 1).astype(F32)

    @pl.when((e == 0) & (f == 0))
    def _():
        hn = _rms(x_ref[...], g_ref[...])
        hn_ref[...] = hn.astype(BF16)
        acc_ref[...] = jnp.zeros_like(acc_ref)
        i1, i2, g1, g2 = _route(hn, wr_ref, br_ref, n_experts, lane)
        comb_ref[...] = jnp.where(lane == i1, g1, 0.0) + jnp.where(lane == i2, g2, 0.0)

    hn = hn_ref[...]
    gate = jnp.dot(hn, wg_ref[...], preferred_element_type=F32)
    up = jnp.dot(hn, wu_ref[...], preferred_element_type=F32)
    act = (jax.nn.silu(gate) * up).astype(BF16)
    weight = jnp.sum(jnp.where(lane == e.astype(F32), comb_ref[...], 0.0), axis=-1, keepdims=True)
    acc_ref[...] += weight * jnp.dot(act, wd_ref[...], preferred_element_type=F32)

    @pl.when((e == n_experts - 1) & (f == pl.num_programs(2) - 1))
    def _():
        _finish(x_ref, acc_ref, gf_ref, o_ref)


def _moe(x, g, wr, br, wg, wu, wd, layer, g_final, tm, tf):
    n, d = x.shape
    _, n_experts, _, ff = wg.shape
    final = g_final is not None
    blk = layer // 2
    in_specs = [pl.BlockSpec((tm, d), lambda i, e, f: (i, 0)),
                pl.BlockSpec((None, 1, d), lambda i, e, f: (layer, 0, 0)),
                pl.BlockSpec((d, LANES), lambda i, e, f: (0, 0)),
                pl.BlockSpec((1, LANES), lambda i, e, f: (0, 0)),
                pl.BlockSpec((None, None, d, tf), lambda i, e, f: (blk, e, 0, f)),
                pl.BlockSpec((None, None, d, tf), lambda i, e, f: (blk, e, 0, f)),
                pl.BlockSpec((None, None, tf, d), lambda i, e, f: (blk, e, f, 0))]
    args = [x, g, wr, br, wg, wu, wd]
    if final:
        in_specs.append(pl.BlockSpec((1, d), lambda i, e, f: (0, 0)))
        args.append(g_final.reshape(1, d))
    return pl.pallas_call(
        functools.partial(_moe_body, n_experts=n_experts, final=final),
        grid=(n // tm, n_experts, ff // tf),
        in_specs=in_specs,
        out_specs=pl.BlockSpec((tm, d), lambda i, e, f: (i, 0)),
        out_shape=jax.ShapeDtypeStruct((n, d), F32),
        scratch_shapes=[pltpu.VMEM((tm, d), BF16), pltpu.VMEM((tm, LANES), F32), pltpu.VMEM((tm, d), F32)],
        compiler_params=_compiler_params(("parallel", "arbitrary", "arbitrary")),
        name="moe",
    )(*args)


EXPERT_ROW_TILE = 512
MOE_DMA_ROWS = 256


def _router_body(x_ref, g_ref, wr_ref, br_ref, hn_ref, route_ref, *, n_experts):
    lane = lax.broadcasted_iota(jnp.int32, route_ref.shape, 1).astype(F32)
    hn = _rms(x_ref[...], g_ref[...])
    hn_ref[...] = hn
    i1, i2, g1, g2 = _route(hn, wr_ref, br_ref, n_experts, lane)
    route_ref[...] = jnp.where(lane == 0.0, i1, jnp.where(lane == 1.0, i2, jnp.where(
        lane == 2.0, g1, jnp.where(lane == 3.0, g2, 0.0))))


def _router(x, g, wr, br, layer, n_experts, tm):
    n, d = x.shape
    return pl.pallas_call(
        functools.partial(_router_body, n_experts=n_experts),
        grid=(n // tm,),
        in_specs=[pl.BlockSpec((tm, d), lambda i: (i, 0)),
                  pl.BlockSpec((None, 1, d), lambda i: (layer, 0, 0)),
                  pl.BlockSpec((d, LANES), lambda i: (0, 0)),
                  pl.BlockSpec((1, LANES), lambda i: (0, 0))],
        out_specs=[pl.BlockSpec((tm, d), lambda i: (i, 0)), pl.BlockSpec((tm, LANES), lambda i: (i, 0))],
        out_shape=[jax.ShapeDtypeStruct((n, d), F32), jax.ShapeDtypeStruct((n, LANES), F32)],
        compiler_params=_compiler_params(("parallel",)),
        name="moe_router",
    )(x, g, wr, br)


def _padded_router(w_router, b_router):
    d, n_experts = w_router.shape
    wr = jnp.zeros((d, LANES), F32).at[:, :n_experts].set(w_router.astype(F32))
    br = jnp.zeros((1, LANES), F32).at[0, :n_experts].set(b_router.astype(F32))
    return wr, br


def _dispatch_plan(experts, n_experts, n_tiles):
    flat = experts.reshape(-1)
    onehot = (flat[:, None] == jnp.arange(n_experts, dtype=jnp.int32)[None, :]).astype(jnp.int32)
    running = jnp.cumsum(onehot, axis=0)
    rank = jnp.sum(onehot * running, axis=1) - 1
    tiles_per_expert = (running[-1] + EXPERT_ROW_TILE - 1) // EXPERT_ROW_TILE
    tile_end = jnp.cumsum(tiles_per_expert)
    group_start = (tile_end - tiles_per_expert) * EXPERT_ROW_TILE
    slot = jnp.sum(onehot * group_start[None, :], axis=1) + rank
    tile = jnp.arange(n_tiles, dtype=jnp.int32)
    tile_expert = jnp.minimum(jnp.sum((tile[:, None] >= tile_end[None, :]).astype(jnp.int32), axis=1),
                              n_experts - 1)
    tile_valid = (tile < tile_end[-1]).astype(jnp.int32)
    return slot.astype(jnp.int32), tile_expert.astype(jnp.int32), tile_valid


def _row_copy(src_ref, src_row, dst_ref, dst_row, sem):
    return pltpu.make_async_copy(src_ref.at[pl.ds(src_row, 1)], dst_ref.at[pl.ds(dst_row, 1)], sem)


def _slot_tokens_body(slot_ref, tok_ref):
    def clear(i, c):
        for k in range(8):
            tok_ref[8 * i + k] = 0
        return c

    def fill(i, c):
        for k in range(8):
            tok_ref[slot_ref[8 * i + k]] = 4 * i + k // 2
        return c

    lax.fori_loop(0, tok_ref.shape[0] // 8, clear, 0)
    lax.fori_loop(0, slot_ref.shape[0] // 8, fill, 0)


def _slot_tokens(slot, n_slots):
    smem = pl.BlockSpec(memory_space=pltpu.SMEM)
    return pl.pallas_call(
        _slot_tokens_body,
        in_specs=[smem],
        out_specs=smem,
        out_shape=jax.ShapeDtypeStruct((n_slots,), jnp.int32),
        compiler_params=pltpu.CompilerParams(disable_bounds_checks=True),
        name="moe_slot_tokens",
    )(slot)


def _experts_body(tok_ref, te_ref, tv_ref, hn_ref, wg_ref, wu_ref, wd_ref, y_ref, xbuf_ref, sem, *, tf):
    t = pl.program_id(0)
    last = pl.num_programs(0) - 1
    rows = y_ref.shape[0]
    n_f = wg_ref.shape[1] // tf
    cur = t % 2

    def copy(tile, buf, j):
        return _row_copy(hn_ref, tok_ref[tile * rows + j], xbuf_ref.at[buf], j, sem.at[buf])

    def start_all(tile, buf):
        for j in range(rows):
            copy(tile, buf, j).start()

    def wait_all(buf):
        pltpu.make_async_copy(hn_ref.at[pl.ds(0, rows)], xbuf_ref.at[buf], sem.at[buf]).wait()

    @pl.when(t == 0)
    def _():
        start_all(0, 0)

    @pl.when((t == 0) | (tv_ref[jnp.maximum(t - 1, 0)] != 0))
    def _():
        wait_all(cur)

    live = tv_ref[t] != 0

    @pl.when(live)
    def _():
        start_all(jnp.minimum(t + 1, last), 1 - cur)

    @pl.when(live)
    def _():
        xb = xbuf_ref[cur].astype(BF16)
        for f in range(n_f):
            cols = slice(f * tf, (f + 1) * tf)
            gate = jnp.dot(xb, wg_ref[:, cols], preferred_element_type=F32)
            up = jnp.dot(xb, wu_ref[:, cols], preferred_element_type=F32)
            act = (jax.nn.silu(gate) * up).astype(BF16)
            part = jnp.dot(act, wd_ref[cols, :], preferred_element_type=F32)
            if f == 0:
                y_ref[...] = part
            else:
                y_ref[...] += part

    @pl.when(jnp.logical_not(live))
    def _():
        y_ref[...] = jnp.zeros_like(y_ref)

    @pl.when((t == last) & live)
    def _():
        wait_all(1 - cur)


def _experts(tok, tile_expert, tile_valid, hn, wg, wu, wd, layer, tf):
    d = hn.shape[1]
    ff = wg.shape[3]
    blk = layer // 2
    n_tiles = tile_expert.shape[0]
    weights = lambda r, c: pl.BlockSpec((None, None, r, c), lambda t, tok, te, tv: (blk, te[t], 0, 0))
    return pl.pallas_call(
        functools.partial(_experts_body, tf=tf),
        grid_spec=pltpu.PrefetchScalarGridSpec(
            num_scalar_prefetch=3,
            grid=(n_tiles,),
            in_specs=[pl.BlockSpec(memory_space=pl.ANY), weights(d, ff), weights(d, ff), weights(ff, d)],
            out_specs=pl.BlockSpec((EXPERT_ROW_TILE, d), lambda t, tok, te, tv: (t, 0)),
            scratch_shapes=[pltpu.VMEM((2, EXPERT_ROW_TILE, d), F32), pltpu.SemaphoreType.DMA((2,))]),
        out_shape=jax.ShapeDtypeStruct((n_tiles * EXPERT_ROW_TILE, d), F32),
        compiler_params=pltpu.CompilerParams(dimension_semantics=("arbitrary",), disable_bounds_checks=True,
                                             vmem_limit_bytes=VMEM_LIMIT_BYTES),
        name="moe_experts",
    )(tok, tile_expert, tile_valid, hn, wg, wu, wd)


def _combine_body(slot_ref, x_ref, route_ref, y_ref, *rest, final):
    gf_ref = rest[0] if final else None
    o_ref, ya_ref, yb_ref, sem = rest[-4:]
    i = pl.program_id(0)
    cur = i % 2

    def start_step(step, buf):
        base = step * MOE_DMA_ROWS

        for j in range(MOE_DMA_ROWS):
            t = base + j
            _row_copy(y_ref, slot_ref[2 * t], ya_ref.at[buf], j, sem.at[0, buf]).start()
            _row_copy(y_ref, slot_ref[2 * t + 1], yb_ref.at[buf], j, sem.at[1, buf]).start()

    @pl.when(i == 0)
    def _():
        start_step(0, 0)

    @pl.when(i + 1 < pl.num_programs(0))
    def _():
        start_step(i + 1, 1 - cur)

    for k, buf_ref in enumerate((ya_ref, yb_ref)):
        pltpu.make_async_copy(y_ref.at[pl.ds(0, MOE_DMA_ROWS)], buf_ref.at[cur], sem.at[k, cur]).wait()
    y = x_ref[...] + route_ref[:, 2:3] * ya_ref[cur] + route_ref[:, 3:4] * yb_ref[cur]
    o_ref[...] = y if gf_ref is None else _rms(y, gf_ref[...])


def _combine(slot, x, route, y, g_final):
    n, d = x.shape
    final = g_final is not None
    in_specs = [pl.BlockSpec((MOE_DMA_ROWS, d), lambda i, s: (i, 0)),
                pl.BlockSpec((MOE_DMA_ROWS, LANES), lambda i, s: (i, 0)),
                pl.BlockSpec(memory_space=pl.ANY)]
    args = [slot, x, route, y]
    if final:
        in_specs.append(pl.BlockSpec((1, d), lambda i, s: (0, 0)))
        args.append(g_final.reshape(1, d))
    return pl.pallas_call(
        functools.partial(_combine_body, final=final),
        grid_spec=pltpu.PrefetchScalarGridSpec(
            num_scalar_prefetch=1,
            grid=(n // MOE_DMA_ROWS,),
            in_specs=in_specs,
            out_specs=pl.BlockSpec((MOE_DMA_ROWS, d), lambda i, s: (i, 0)),
            scratch_shapes=[pltpu.VMEM((2, MOE_DMA_ROWS, d), F32), pltpu.VMEM((2, MOE_DMA_ROWS, d), F32),
                            pltpu.SemaphoreType.DMA((2, 2))]),
        out_shape=jax.ShapeDtypeStruct((n, d), F32),
        compiler_params=pltpu.CompilerParams(dimension_semantics=("arbitrary",), disable_bounds_checks=True,
                                             vmem_limit_bytes=VMEM_LIMIT_BYTES),
        name="moe_combine",
    )(*args)


def _moe_sorted(x, hn, route, wg, wu, wd, layer, g_final, tf):
    n = x.shape[0]
    n_experts = wg.shape[1]
    n_tiles = 2 * n // EXPERT_ROW_TILE + n_experts
    slot, tile_expert, tile_valid = _dispatch_plan(route[:, :2].astype(jnp.int32), n_experts, n_tiles)
    tok = _slot_tokens(slot, n_tiles * EXPERT_ROW_TILE)
    y = _experts(tok, tile_expert, tile_valid, hn, wg, wu, wd, layer, tf)
    return _combine(slot, x, route, y, g_final)


PROMPT_ROW_TILE = 512
PROMPT_SCAN_STEPS = 128


def kernel(x_prompt, x_sample, cache_win_k, cache_win_v, state_ssm_re, state_ssm_im, rel_bias, norm_mix, w_in, ssm_a_re, ssm_a_im, ssm_log_dt, ssm_b_re, ssm_b_im, ssm_c_re, ssm_c_im, ssm_d, ssm_w_glu, ssm_b_glu, norm_attn_out, norm_ssm_out, w_out, norm_ffn, ffn_w_gate, ffn_w_up, ffn_w_down, moe_w_router, moe_b_router, moe_w_gate, moe_w_up, moe_w_down, norm_final):
    batch, seq, d_model = x_prompt.shape
    dec_batch, dec_seq, _ = x_sample.shape
    depth = w_in.shape[0]
    win, n_heads = cache_win_k.shape[2:4]
    attn_width = n_heads * HEAD_DIM
    n_groups, n_state = ssm_a_re.shape[1:]
    ssm_width = n_groups * SSM_GROUP
    assert w_in.shape[2] == 3 * attn_width + ssm_width and seq == MAX_WINDOW
    dec_rows = dec_batch * dec_seq

    band_bias = _band_bias(rel_bias)
    sample_tabs = _sample_tables(rel_bias, win, dec_seq)
    ar, ai, bb_re, bb_im = _s5_params(ssm_a_re, ssm_a_im, ssm_log_dt, ssm_b_re, ssm_b_im)
    cache_k = jnp.transpose(cache_win_k, (0, 1, 3, 4, 2))
    cache_v = jnp.transpose(cache_win_v, (0, 1, 3, 4, 2))

    rows3 = lambda a: a.reshape(a.shape[0], 1, a.shape[1])
    norm_mix3, norm_ffn3, b_glu3 = rows3(norm_mix), rows3(norm_ffn), rows3(ssm_b_glu)
    g_attn3, g_ssm3 = rows3(norm_attn_out), rows3(norm_ssm_out)
    w_in_b, w_out_b, w_glu_b = w_in.astype(BF16), w_out.astype(BF16), ssm_w_glu.astype(BF16)
    ffn_b = [w.astype(BF16) for w in (ffn_w_gate, ffn_w_up, ffn_w_down)]
    moe_b = [w.astype(BF16) for w in (moe_w_gate, moe_w_up, moe_w_down)]
    tf = ffn_w_gate.shape[2] // 2

    def mix_and_ffn(x, l, attn, y, tm, y_tiles_per_seq):
        g_final = norm_final if l == depth - 1 else None
        mix = functools.partial(_post_mix, attn, y, x, g_attn3, g_ssm3, w_glu_b, b_glu3, w_out_b, l, tm,
                                y_tiles_per_seq)
        if l % 2 == 0:
            return _ffn(mix(), norm_ffn3, *ffn_b, l, g_final, tm, tf)
        n_experts = moe_w_gate.shape[1]
        wr, br = _padded_router(moe_w_router[l // 2], moe_b_router[l // 2])
        if 2 * x.shape[0] >= n_experts * EXPERT_ROW_TILE:
            x = mix()
            hn, route = _router(x, norm_ffn3, wr, br, l, n_experts, tm)
            return _moe_sorted(x, hn, route, *moe_b, l, g_final, tf)
        return _moe(mix(), norm_ffn3, wr, br, *moe_b, l, g_final, tm, tf)

    xp = x_prompt.reshape(batch * seq, d_model)
    xs = x_sample.reshape(dec_rows, d_model)
    zero_state = jnp.zeros((1, batch, 2 * n_groups * n_state), F32)
    k_all = jnp.zeros((depth, batch, attn_width, seq), F32)
    v_all = jnp.zeros((depth, batch, attn_width, seq), F32)
    wx, wc = _pack_s5_weights(bb_re, bb_im, ssm_c_re, ssm_c_im)
    s5_consts = (wx, wc, ar.reshape(depth, 1, -1), ai.reshape(depth, 1, -1), ssm_d.reshape(depth, 1, -1))
    sample_h0 = _pack_state(state_ssm_re, state_ssm_im)
    prompt_h, sample_h, sample_k, sample_v = [], [], [], []
    for l in range(depth):
        qkv, k_all, v_all, u = _norm_proj_prompt(xp, norm_mix3, w_in_b, l, k_all, v_all, PROMPT_ROW_TILE)
        attn = _prompt_attn(qkv, band_bias, batch, seq, attn_width)
        y, h_fin = _s5_scan(u, *s5_consts, zero_state, l, 0, batch, PROMPT_SCAN_STEPS, True)
        xp = mix_and_ffn(xp, l, attn, y, PROMPT_ROW_TILE, seq // PROMPT_ROW_TILE)
        prompt_h.append(h_fin)

        proj = _norm_proj(xs, norm_mix3, w_in_b, l, dec_rows)
        q, k_new, v_new = (proj[:, i * attn_width:(i + 1) * attn_width].reshape(dec_batch, dec_seq, n_heads, HEAD_DIM)
                           for i in range(3))
        attn = _sample_attn(q, k_new, v_new, cache_k, cache_v, l, sample_tabs).reshape(dec_rows, attn_width)
        u = proj[:, 3 * attn_width:].reshape(dec_batch, dec_seq, ssm_width).transpose(1, 0, 2)
        y, h_fin = _s5_scan(u.reshape(dec_rows, ssm_width), *s5_consts, sample_h0, l, l, dec_batch, dec_seq, False)
        y = y.reshape(dec_seq, dec_batch, ssm_width).transpose(1, 0, 2).reshape(dec_rows, ssm_width)
        xs = mix_and_ffn(xs, l, attn, y, dec_rows, None)
        sample_h.append(h_fin)
        sample_k.append(k_new)
        sample_v.append(v_new)

    def window_out(a):
        return jnp.transpose(a.reshape(depth, batch, n_heads, HEAD_DIM, seq), (0, 1, 4, 2, 3))

    return (xp.reshape(batch, seq, d_model), xs.reshape(dec_batch, dec_seq, d_model),
            window_out(k_all), window_out(v_all), *_unpack_state(jnp.stack(prompt_h, 0), n_groups, n_state),
            jnp.stack(sample_k, 0), jnp.stack(sample_v, 0),
            *_unpack_state(jnp.stack(sample_h, 0), n_groups, n_state))
```

```python
import functools

import numpy as np
import jax
import jax.numpy as jnp
from jax import lax
from jax.experimental import pallas as pl
from jax.experimental.pallas import tpu as pltpu

F32 = jnp.float32
BF16 = jnp.bfloat16

HEAD_DIM = 64
DILATION_PATTERNS = ((128, 1), (512, 4), (2048, 16))
MAX_WINDOW = 2048
N_BUCKETS = 32
SSM_GROUP = 16
SSM_STATE = 64
RMS_EPS = 1e-6
NEG_BIG = -1e30
LOG2_E = 1.4426950408889634

LANES = 128
Q_TILE = 128
ATTN_UNROLL = 4
GROUPS_PER_TILE = LANES // SSM_GROUP
STATE_TILE = GROUPS_PER_TILE * SSM_STATE
VMEM_LIMIT_BYTES = 56 * 1024 * 1024


def _compiler_params(semantics):
    return pltpu.CompilerParams(dimension_semantics=semantics, vmem_limit_bytes=VMEM_LIMIT_BYTES)


def _rms(x, g):
    return x * lax.rsqrt(jnp.mean(x * x, axis=-1, keepdims=True) + RMS_EPS) * g


def _t5_causal_bucket(dist):
    max_exact = N_BUCKETS // 2
    d = np.asarray(dist, dtype=np.int64)
    large = max_exact + (np.log(np.maximum(d, max_exact) / max_exact)
                         / np.log(MAX_WINDOW / max_exact)
                         * (N_BUCKETS - max_exact)).astype(np.int64)
    large = np.minimum(large, N_BUCKETS - 1)
    return np.where(d < max_exact, d, large).astype(np.int32)


def _multiplicity(d):
    m = np.zeros(d.shape, np.int32)
    for window, dil in DILATION_PATTERNS:
        m += ((d >= 0) & (d <= window) & (d % dil == 0)).astype(np.int32)
    return m


def _norm_proj_body(x_ref, g_ref, w_ref, o_ref):
    hn = _rms(x_ref[...], g_ref[...])
    o_ref[...] = jnp.dot(hn.astype(BF16), w_ref[...], preferred_element_type=F32)


def _norm_proj(x, g, w, layer, tm):
    n, d = x.shape
    c = w.shape[2]
    return pl.pallas_call(
        _norm_proj_body,
        grid=(n // tm,),
        in_specs=[pl.BlockSpec((tm, d), lambda i: (i, 0)),
                  pl.BlockSpec((None, 1, d), lambda i: (layer, 0, 0)),
                  pl.BlockSpec((None, d, c), lambda i: (layer, 0, 0))],
        out_specs=pl.BlockSpec((tm, c), lambda i: (i, 0)),
        out_shape=jax.ShapeDtypeStruct((n, c), F32),
        compiler_params=_compiler_params(("parallel",)),
        name="norm_proj",
    )(x, g, w)


def _norm_proj_prompt_body(x_ref, g_ref, w_ref, _k_all, _v_all, qkv_ref, kt_ref, vt_ref, u_ref, *, attn_width):
    hn = _rms(x_ref[...], g_ref[...])
    proj = jnp.dot(hn.astype(BF16), w_ref[...], preferred_element_type=F32)
    qkv_ref[...] = proj[:, :3 * attn_width]
    kt_ref[...] = proj[:, attn_width:2 * attn_width].T
    vt_ref[...] = proj[:, 2 * attn_width:3 * attn_width].T
    u_ref[...] = proj[:, 3 * attn_width:]


def _norm_proj_prompt(x, g, w, layer, k_all, v_all, tm):
    n, d = x.shape
    c = w.shape[2]
    _, batch, attn_width, seq = k_all.shape
    ch = c - 3 * attn_width
    per_seq = seq // tm
    window = pl.BlockSpec((None, None, attn_width, tm), lambda i: (layer, i // per_seq, 0, i % per_seq))
    untouched = pl.BlockSpec(memory_space=pl.ANY)
    return pl.pallas_call(
        functools.partial(_norm_proj_prompt_body, attn_width=attn_width),
        grid=(n // tm,),
        in_specs=[pl.BlockSpec((tm, d), lambda i: (i, 0)),
                  pl.BlockSpec((None, 1, d), lambda i: (layer, 0, 0)),
                  pl.BlockSpec((None, d, c), lambda i: (layer, 0, 0)),
                  untouched, untouched],
        out_specs=[pl.BlockSpec((tm, 3 * attn_width), lambda i: (i, 0)), window, window,
                   pl.BlockSpec((tm, ch), lambda i: (i % per_seq, i // per_seq))],
        out_shape=[jax.ShapeDtypeStruct((n, 3 * attn_width), F32),
                   jax.ShapeDtypeStruct(k_all.shape, F32),
                   jax.ShapeDtypeStruct(v_all.shape, F32),
                   jax.ShapeDtypeStruct((seq, batch * ch), F32)],
        input_output_aliases={3: 1, 4: 2},
        compiler_params=_compiler_params(("parallel",)),
        name="norm_proj_prompt",
    )(x, g, w, k_all, v_all)


def _band_bias(rel_bias):
    period = 3 * Q_TILE
    n_heads = rel_bias.shape[1]
    rows0 = []
    for window, dil in DILATION_PATTERNS:
        assert window // dil == Q_TILE
        vec = rel_bias[_t5_causal_bucket((Q_TILE - np.arange(Q_TILE + 1)) * dil)].astype(F32)
        rows0.append(jnp.concatenate([vec.T, jnp.full((n_heads, period - Q_TILE - 1), NEG_BIG, F32)], axis=1))
    row0 = jnp.stack(rows0, 0)
    flat = jnp.tile(row0, (1, 1, Q_TILE))[:, :, :Q_TILE * (period - 1)]
    band = flat.reshape(len(DILATION_PATTERNS), n_heads, Q_TILE, period - 1)[:, :, :, :2 * Q_TILE]
    return band * LOG2_E


def _prompt_attn_body(q_ref, k_ref, v_ref, bias_ref, o_ref, op_ref, m_ref, l_ref):
    seq = q_ref.shape[0]
    lane = lax.broadcasted_iota(jnp.int32, (Q_TILE, LANES), 1)
    head0 = lane < HEAD_DIM
    scale = HEAD_DIM ** -0.5 * LOG2_E

    def rows(start, size, dil):
        return pl.ds(start, size) if dil == 1 else pl.ds(start, size, stride=dil)

    def tiles(p, dil, starts, first):
        nk = Q_TILE if first else 2 * Q_TILE
        bias = (bias_ref[p, :, :, Q_TILE:] if first else bias_ref[p]).reshape(2 * Q_TILE, nk)
        work = []
        for q_start, k_start in starts:
            qsl, ksl = rows(q_start, Q_TILE, dil), rows(k_start, nk, dil)
            qt = q_ref[qsl, :] * scale
            q2 = jnp.concatenate([jnp.where(head0, qt, 0.0), jnp.where(head0, 0.0, qt)], axis=0).astype(BF16)
            v_ones = jnp.concatenate([v_ref[ksl, :].astype(BF16), jnp.ones((nk, LANES), BF16)], axis=1)
            work.append([qsl, v_ones, q2, k_ref[ksl, :].astype(BF16)])
        for w in work:
            w[2] = lax.dot_general(w[2], w[3], (((1,), (1,)), ((), ())), preferred_element_type=F32) + bias
        for w in work:
            w[3] = jnp.max(w[2], axis=-1, keepdims=True)
        for w in work:
            w[2] = jnp.exp2((w[2] - w[3]).astype(BF16))
        for w in work:
            w[2] = jnp.dot(w[2], w[1], preferred_element_type=F32)
        for qsl, _, ol, m in work:
            op_ref[p, qsl, :] = jnp.where(head0, ol[:Q_TILE, :LANES], ol[Q_TILE:, :LANES])
            m_ref[p, qsl, :] = jnp.where(head0, m[:Q_TILE], m[Q_TILE:])
            l_ref[p, qsl, :] = jnp.where(head0, ol[:Q_TILE, LANES:], ol[Q_TILE:, LANES:])

    for p, (_, dil) in enumerate(DILATION_PATTERNS):
        n_tiles = seq // dil // Q_TILE
        stride = dil * Q_TILE
        unroll = min(ATTN_UNROLL, dil)
        if dil == 1:
            unroll = max(u for u in range(1, 2 * ATTN_UNROLL + 1) if (n_tiles - 1) % u == 0)
            for r in range(dil):
                tiles(p, dil, [(r, r)], True)

                def blk(t, c, p=p, dil=dil, r=r, stride=stride, unroll=unroll):
                    first_t = 1 + t * unroll
                    tiles(p, dil, [(r + stride * (first_t + j), r + stride * (first_t + j - 1))
                                   for j in range(unroll)], False)
                    return c
                lax.fori_loop(0, (n_tiles - 1) // unroll, blk, 0)
        else:
            assert dil % unroll == 0

            def classes(g, c, p=p, dil=dil, n_tiles=n_tiles, stride=stride, unroll=unroll):
                rs = [g * unroll + j for j in range(unroll)]
                tiles(p, dil, [(r, r) for r in rs], True)
                if n_tiles > 1:
                    def blk(t, c2):
                        tiles(p, dil, [(r + stride * t, r + stride * (t - 1)) for r in rs], False)
                        return c2
                    lax.fori_loop(1, n_tiles, blk, 0)
                return c
            lax.fori_loop(0, dil // unroll, classes, 0)

    def combine(t, carry):
        r = pl.ds(pl.multiple_of(t * Q_TILE, Q_TILE), Q_TILE)
        ms = [m_ref[p, r, :] for p in range(len(DILATION_PATTERNS))]
        mx = functools.reduce(jnp.maximum, ms)
        ws = [jnp.exp2(m - mx) for m in ms]
        num = sum(w * op_ref[p, r, :] for p, w in enumerate(ws))
        den = sum(w * l_ref[p, r, :] for p, w in enumerate(ws))
        o_ref[r, :] = num / den
        return carry

    lax.fori_loop(0, seq // Q_TILE, combine, 0)


def _prompt_attn(proj, bias, batch, seq, attn_width):
    n_pairs = attn_width // LANES
    n_pat = len(DILATION_PATTERNS)
    assert seq % (Q_TILE * max(d for _, d in DILATION_PATTERNS)) == 0
    return pl.pallas_call(
        _prompt_attn_body,
        grid=(batch, n_pairs),
        in_specs=[pl.BlockSpec((seq, LANES), lambda b, h: (b, h)),
                  pl.BlockSpec((seq, LANES), lambda b, h: (b, n_pairs + h)),
                  pl.BlockSpec((seq, LANES), lambda b, h: (b, 2 * n_pairs + h)),
                  pl.BlockSpec((n_pat, 2, Q_TILE, 2 * Q_TILE), lambda b, h: (0, h, 0, 0))],
        out_specs=pl.BlockSpec((seq, LANES), lambda b, h: (b, h)),
        out_shape=jax.ShapeDtypeStruct((batch * seq, attn_width), F32),
        scratch_shapes=[pltpu.VMEM((n_pat, seq, LANES), F32)] * 3,
        compiler_params=_compiler_params(("parallel", "parallel")),
        name="prompt_attn",
    )(proj, proj, proj, bias)


def _sample_tables(rel_bias, win, n_new):
    n_heads = rel_bias.shape[1]
    vec_t = rel_bias[_t5_causal_bucket(np.arange(win + n_new))].astype(F32).T
    flipped = vec_t[:, ::-1]
    t = np.arange(n_new)
    d_cache = win + t[:, None] - np.arange(win)[None, :]
    d_new = t[:, None] - t[None, :]
    b_cache = jnp.stack([flipped[:, n_new - 1 - i:n_new - 1 - i + win] for i in range(n_new)], 0)
    b_new = jnp.transpose(vec_t[:, np.maximum(d_new, 0)], (1, 0, 2))
    out = []
    for b, d in ((b_cache, d_cache), (b_new, d_new)):
        mult = _multiplicity(d)
        b = jnp.where((mult > 0)[:, None, :], b, NEG_BIG).reshape(n_new * n_heads, d.shape[1])
        m = np.broadcast_to(np.maximum(mult, 1)[:, None, :], (n_new, n_heads, d.shape[1]))
        out += [b, jnp.asarray(m.reshape(n_new * n_heads, d.shape[1]), F32)]
    return out


def _sample_attn_body(q_ref, kc_ref, kn_ref, vc_ref, vn_ref, bc_ref, mc_ref, bn_ref, mn_ref, hm_ref, o_ref):
    n_new = o_ref.shape[0]
    win = kc_ref.shape[-1]
    qb = (q_ref[...] * HEAD_DIM ** -0.5).astype(BF16)
    s_c = jnp.dot(qb, kc_ref[...].reshape(-1, win).astype(BF16), preferred_element_type=F32) + bc_ref[...]
    s_n = jnp.dot(qb, kn_ref[...].astype(BF16), preferred_element_type=F32) + bn_ref[...]
    m = jnp.maximum(jnp.max(s_c, axis=-1, keepdims=True), jnp.max(s_n, axis=-1, keepdims=True))
    e_c = jnp.exp(s_c - m) * mc_ref[...]
    e_n = jnp.exp(s_n - m) * mn_ref[...]
    den = jnp.sum(e_c, axis=-1, keepdims=True) + jnp.sum(e_n, axis=-1, keepdims=True)
    nt = (((1,), (1,)), ((), ()))
    o = (lax.dot_general(e_c.astype(BF16), vc_ref[...].reshape(-1, win).astype(BF16), nt,
                         preferred_element_type=F32)
         + lax.dot_general(e_n.astype(BF16), vn_ref[...].astype(BF16), nt, preferred_element_type=F32))
    o = o / den * hm_ref[...]
    o_ref[...] = jnp.sum(o.reshape(n_new, -1, o.shape[-1]), axis=1)


def _sample_attn(q, k_new, v_new, cache_k, cache_v, layer, tables):
    depth, batch, n_heads, hd, win = cache_k.shape
    n_new = q.shape[1]
    width = n_heads * hd
    rows = n_new * n_heads
    eye = jnp.eye(n_heads, dtype=F32)
    q_bd = jnp.einsum('bthd,hg->btghd', q, eye).reshape(batch, rows, width)
    kn = jnp.transpose(k_new, (0, 2, 3, 1)).reshape(batch, width, n_new)
    vn = jnp.transpose(v_new, (0, 2, 3, 1)).reshape(batch, width, n_new)
    head_mask = jnp.asarray(np.kron(np.tile(np.eye(n_heads), (n_new, 1)), np.ones((1, hd))), F32)
    cache_spec = pl.BlockSpec((None, None, n_heads, hd, win), lambda b: (layer, b, 0, 0, 0))
    new_spec = pl.BlockSpec((None, width, n_new), lambda b: (b, 0, 0))
    const = lambda a: pl.BlockSpec(a.shape, lambda b: (0, 0))
    return pl.pallas_call(
        _sample_attn_body,
        grid=(batch,),
        in_specs=[pl.BlockSpec((None, rows, width), lambda b: (b, 0, 0)),
                  cache_spec, new_spec, cache_spec, new_spec] + [const(t) for t in tables] + [const(head_mask)],
        out_specs=pl.BlockSpec((None, n_new, width), lambda b: (b, 0, 0)),
        out_shape=jax.ShapeDtypeStruct((batch, n_new, width), F32),
        compiler_params=_compiler_params(("parallel",)),
        name="sample_attn",
    )(q_bd, cache_k, kn, cache_v, vn, *tables, head_mask)


def _s5_param_body(are_ref, aim_ref, ldt_ref, bre_ref, bim_ref, ar_ref, ai_ref, bbre_ref, bbim_ref):
    a_re, a_im = are_ref[...], aim_ref[...]
    dt = jnp.exp(ldt_ref[...])
    mag = jnp.exp(dt * a_re)
    ar, ai = mag * jnp.cos(dt * a_im), mag * jnp.sin(dt * a_im)
    nr, ni = ar - 1.0, ai
    den = a_re * a_re + a_im * a_im
    fr = (nr * a_re + ni * a_im) / den
    fi = (ni * a_re - nr * a_im) / den
    br, bi = bre_ref[...], bim_ref[...]
    ar_ref[...] = ar
    ai_ref[...] = ai
    bbre_ref[...] = fr[:, None, :] * br - fi[:, None, :] * bi
    bbim_ref[...] = fr[:, None, :] * bi + fi[:, None, :] * br


def _s5_params(a_re, a_im, log_dt, b_re, b_im):
    depth, g, p = a_re.shape
    c = b_re.shape[-1]
    n = depth * g
    full = lambda shape: pl.BlockSpec(shape, lambda: (0,) * len(shape))
    ar, ai, bbre, bbim = pl.pallas_call(
        _s5_param_body,
        in_specs=[full((n, p)), full((n, p)), full((n, 1)), full((n, c, p)), full((n, c, p))],
        out_specs=[full((n, p)), full((n, p)), full((n, c, p)), full((n, c, p))],
        out_shape=[jax.ShapeDtypeStruct((n, p), F32), jax.ShapeDtypeStruct((n, p), F32),
                   jax.ShapeDtypeStruct((n, c, p), F32), jax.ShapeDtypeStruct((n, c, p), F32)],
        name="s5_params",
    )(a_re.reshape(n, p), a_im.reshape(n, p), log_dt.reshape(n, 1),
      jnp.swapaxes(b_re, -1, -2).reshape(n, c, p), jnp.swapaxes(b_im, -1, -2).reshape(n, c, p))
    return (ar.reshape(depth, g, p), ai.reshape(depth, g, p),
            bbre.reshape(depth, g, c, p), bbim.reshape(depth, g, c, p))


def _pack_s5_weights(bb_re, bb_im, c_re, c_im):
    depth, g, c, p = bb_re.shape
    tiles = g // GROUPS_PER_TILE
    eye = jnp.eye(GROUPS_PER_TILE, dtype=F32)

    def inp(w):
        w = w.astype(F32).reshape(depth, tiles, GROUPS_PER_TILE, c, p)
        return jnp.einsum('ljgcp,gh->ljgchp', w, eye).reshape(depth, tiles, LANES, STATE_TILE)

    def outp(w):
        w = w.astype(F32).reshape(depth, tiles, GROUPS_PER_TILE, c, p)
        return jnp.einsum('ljgcp,gh->ljgphc', w, eye).reshape(depth, tiles, STATE_TILE, LANES)

    wx = jnp.concatenate([inp(bb_re), inp(bb_im)], axis=-1).astype(BF16)
    wc = jnp.concatenate([outp(c_re), -outp(c_im)], axis=2).astype(BF16)
    return wx, wc


def _pack_state(h_re, h_im):
    lead, (g, p) = h_re.shape[:-2], h_re.shape[-2:]
    tiles = g // GROUPS_PER_TILE
    st = jnp.stack([h_re.reshape(*lead, tiles, STATE_TILE), h_im.reshape(*lead, tiles, STATE_TILE)], axis=-2)
    return st.reshape(*lead, tiles * 2 * STATE_TILE).astype(F32)


def _unpack_state(h, g, p):
    lead = h.shape[:-1]
    st = h.reshape(*lead, g // GROUPS_PER_TILE, 2, STATE_TILE)
    return st[..., 0, :].reshape(*lead, g, p), st[..., 1, :].reshape(*lead, g, p)


def _s5_scan_body(u_ref, wx_ref, wc_ref, ar_ref, ai_ref, d_ref, h0_ref, y_ref, hout_ref, us_ref, ys_ref, xs_ref,
                  h_ref, *, bsz, steps, batch_on_lanes):
    tiles = wx_ref.shape[0]
    ch = tiles * LANES

    @pl.when(pl.program_id(0) == 0)
    def _():
        h_ref[...] = h0_ref[...]

    for j in range(tiles):
        if batch_on_lanes:
            for b in range(bsz):
                us_ref[j, pl.ds(b, steps, stride=bsz), :] = u_ref[:, ch * b + LANES * j:ch * b + LANES * (j + 1)]
        else:
            us_ref[j] = u_ref[:, LANES * j:LANES * (j + 1)]

    for j in range(tiles):
        xs_ref[:, 2 * STATE_TILE * j:2 * STATE_TILE * (j + 1)] = jnp.dot(
            us_ref[j].astype(BF16), wx_ref[j], preferred_element_type=F32)

    for j in range(tiles):
        re = slice(2 * STATE_TILE * j, 2 * STATE_TILE * j + STATE_TILE)
        im = slice(2 * STATE_TILE * j + STATE_TILE, 2 * STATE_TILE * (j + 1))
        a_r = jnp.broadcast_to(ar_ref[:, STATE_TILE * j:STATE_TILE * (j + 1)], (bsz, STATE_TILE))
        a_i = jnp.broadcast_to(ai_ref[:, STATE_TILE * j:STATE_TILE * (j + 1)], (bsz, STATE_TILE))

        def step(t, carry, re=re, im=im, a_r=a_r, a_i=a_i):
            hr, hi = carry
            r = pl.ds(pl.multiple_of(t * bsz, bsz), bsz)
            nhr = a_r * hr - a_i * hi + xs_ref[r, re]
            nhi = a_r * hi + a_i * hr + xs_ref[r, im]
            xs_ref[r, re] = nhr
            xs_ref[r, im] = nhi
            return nhr, nhi

        hr, hi = lax.fori_loop(0, steps, step, (h_ref[:, re], h_ref[:, im]), unroll=True)
        h_ref[:, re] = hr
        h_ref[:, im] = hi

    for j in range(tiles):
        cols = slice(LANES * j, LANES * (j + 1))
        hb = xs_ref[:, 2 * STATE_TILE * j:2 * STATE_TILE * (j + 1)].astype(BF16)
        y = jnp.dot(hb, wc_ref[j], preferred_element_type=F32) + d_ref[:, cols] * us_ref[j]
        if batch_on_lanes:
            ys_ref[j] = y
            for b in range(bsz):
                y_ref[:, ch * b + LANES * j:ch * b + LANES * (j + 1)] = ys_ref[j, pl.ds(b, steps, stride=bsz), :]
        else:
            y_ref[:, cols] = y
    hout_ref[...] = h_ref[...]


def _s5_scan(u, wx, wc, ar, ai, d_skip, h0, layer, h0_layer, bsz, steps, batch_on_lanes):
    tiles = wx.shape[1]
    ch = tiles * LANES
    width = tiles * 2 * STATE_TILE
    rows = steps * bsz
    if batch_on_lanes:
        n_chunks = u.shape[0] // steps
        io_spec = pl.BlockSpec((steps, bsz * ch), lambda i: (i, 0))
    else:
        n_chunks = 1
        assert u.shape == (rows, ch)
        io_spec = pl.BlockSpec((rows, ch), lambda i: (0, 0))
    of_layer = lambda a: pl.BlockSpec((None,) + a.shape[1:], lambda i: (layer,) + (0,) * (a.ndim - 1))
    return pl.pallas_call(
        functools.partial(_s5_scan_body, bsz=bsz, steps=steps, batch_on_lanes=batch_on_lanes),
        grid=(n_chunks,),
        in_specs=[io_spec, of_layer(wx), of_layer(wc), of_layer(ar), of_layer(ai), of_layer(d_skip),
                  pl.BlockSpec((None, bsz, width), lambda i: (h0_layer, 0, 0))],
        out_specs=[io_spec, pl.BlockSpec((bsz, width), lambda i: (0, 0))],
        out_shape=[jax.ShapeDtypeStruct(u.shape, F32), jax.ShapeDtypeStruct((bsz, width), F32)],
        scratch_shapes=[pltpu.VMEM((tiles, rows, LANES), F32), pltpu.VMEM((tiles, rows, LANES), F32),
                        pltpu.VMEM((rows, width), F32), pltpu.VMEM((bsz, width), F32)],
        compiler_params=_compiler_params(("arbitrary",)),
        name="s5_scan",
    )(u, wx, wc, ar, ai, d_skip, h0)


def _post_mix_body(attn_ref, y_ref, x_ref, ga_ref, gs_ref, wglu_ref, bglu_ref, wo_ref, o_ref):
    aw = attn_ref.shape[1]
    an = _rms(attn_ref[...], ga_ref[...])
    z = jax.nn.gelu(y_ref[...])
    gate = jax.nn.sigmoid(jnp.dot(z.astype(BF16), wglu_ref[...], preferred_element_type=F32) + bglu_ref[...])
    sn = _rms(z * gate, gs_ref[...])
    o_ref[...] = (x_ref[...]
                  + jnp.dot(an.astype(BF16), wo_ref[:aw, :], preferred_element_type=F32)
                  + jnp.dot(sn.astype(BF16), wo_ref[aw:, :], preferred_element_type=F32))


def _post_mix(attn, y, x, g_attn, g_ssm, w_glu, b_glu, w_out, layer, tm, y_tiles_per_seq):
    n, d = x.shape
    aw, sw = attn.shape[1], w_glu.shape[1]
    row = lambda w: pl.BlockSpec((tm, w), lambda i: (i, 0))
    vec = lambda w: pl.BlockSpec((None, 1, w), lambda i: (layer, 0, 0))
    mat = lambda r, c: pl.BlockSpec((None, r, c), lambda i: (layer, 0, 0))
    if y_tiles_per_seq is None:
        y_spec = row(sw)
    else:
        y_spec = pl.BlockSpec((tm, sw), lambda i: (i % y_tiles_per_seq, i // y_tiles_per_seq))
    return pl.pallas_call(
        _post_mix_body,
        grid=(n // tm,),
        in_specs=[row(aw), y_spec, row(d), vec(aw), vec(sw), mat(sw, sw), vec(sw), mat(aw + sw, d)],
        out_specs=row(d),
        out_shape=jax.ShapeDtypeStruct((n, d), F32),
        compiler_params=_compiler_params(("parallel",)),
        name="post_mix",
    )(attn, y, x, g_attn, g_ssm, w_glu, b_glu, w_out)


def _finish(x_ref, acc_ref, gf_ref, o_ref):
    y = x_ref[...] + acc_ref[...]
    o_ref[...] = y if gf_ref is None else _rms(y, gf_ref[...])


def _ffn_body(x_ref, g_ref, wg_ref, wu_ref, wd_ref, *rest, final, tf):
    gf_ref = rest[0] if final else None
    o_ref, acc_ref = rest[-2:]
    hn = _rms(x_ref[...], g_ref[...]).astype(BF16)
    for f in range(wg_ref.shape[1] // tf):
        cols = slice(f * tf, (f + 1) * tf)
        gate = jnp.dot(hn, wg_ref[:, cols], preferred_element_type=F32)
        up = jnp.dot(hn, wu_ref[:, cols], preferred_element_type=F32)
        act = (jax.nn.silu(gate) * up).astype(BF16)
        part = jnp.dot(act, wd_ref[cols, :], preferred_element_type=F32)
        if f == 0:
            acc_ref[...] = part
        else:
            acc_ref[...] += part
    _finish(x_ref, acc_ref, gf_ref, o_ref)


def _ffn(x, g, wg, wu, wd, layer, g_final, tm, tf):
    n, d = x.shape
    ff = wg.shape[2]
    final = g_final is not None
    blk = layer // 2
    in_specs = [pl.BlockSpec((tm, d), lambda i: (i, 0)),
                pl.BlockSpec((None, 1, d), lambda i: (layer, 0, 0)),
                pl.BlockSpec((None, d, ff), lambda i: (blk, 0, 0)),
                pl.BlockSpec((None, d, ff), lambda i: (blk, 0, 0)),
                pl.BlockSpec((None, ff, d), lambda i: (blk, 0, 0))]
    args = [x, g, wg, wu, wd]
    if final:
        in_specs.append(pl.BlockSpec((1, d), lambda i: (0, 0)))
        args.append(g_final.reshape(1, d))
    return pl.pallas_call(
        functools.partial(_ffn_body, final=final, tf=tf),
        grid=(n // tm,),
        in_specs=in_specs,
        out_specs=pl.BlockSpec((tm, d), lambda i: (i, 0)),
        out_shape=jax.ShapeDtypeStruct((n, d), F32),
        scratch_shapes=[pltpu.VMEM((tm, d), F32)],
        compiler_params=_compiler_params(("parallel",)),
        name="ffn_dense",
    )(*args)


def _route(hn, wr_ref, br_ref, n_experts, lane):
    w = wr_ref[...]
    hn_hi, w_hi = hn.astype(BF16), w.astype(BF16)
    hn_lo, w_lo = (hn - hn_hi.astype(F32)).astype(BF16), (w - w_hi.astype(F32)).astype(BF16)
    dot = functools.partial(jnp.dot, preferred_element_type=F32)
    logits = dot(hn_hi, w_hi) + dot(hn_hi, w_lo) + dot(hn_lo, w_hi) + br_ref[...]
    logits = jnp.where(lane < n_experts, logits, -jnp.inf)
    m1 = jnp.max(logits, axis=-1, keepdims=True)
    i1 = jnp.min(jnp.where(logits == m1, lane, float(LANES)), axis=-1, keepdims=True)
    rest_logits = jnp.where(lane == i1, -jnp.inf, logits)
    m2 = jnp.max(rest_logits, axis=-1, keepdims=True)
    i2 = jnp.min(jnp.where(rest_logits == m2, lane, float(LANES)), axis=-1, keepdims=True)
    e2 = jnp.exp(m2 - m1)
    return i1, i2, 1.0 / (1.0 + e2), e2 / (1.0 + e2)


def _moe_body(x_ref, g_ref, wr_ref, br_ref, wg_ref, wu_ref, wd_ref, *rest, n_experts, final):
    gf_ref = rest[0] if final else None
    o_ref, hn_ref, comb_ref, acc_ref = rest[-4:]
    e, f = pl.program_id(1), pl.program_id(2)
    lane = lax.broadcasted_iota(jnp.int32, comb_ref.shape, 1).astype(F32)

    @pl.when((e == 0) & (f == 0))
    def _():
        hn = _rms(x_ref[...], g_ref[...])
        hn_ref[...] = hn.astype(BF16)
        acc_ref[...] = jnp.zeros_like(acc_ref)
        i1, i2, g1, g2 = _route(hn, wr_ref, br_ref, n_experts, lane)
        comb_ref[...] = jnp.where(lane == i1, g1, 0.0) + jnp.where(lane == i2, g2, 0.0)

    hn = hn_ref[...]
    gate = jnp.dot(hn, wg_ref[...], preferred_element_type=F32)
    up = jnp.dot(hn, wu_ref[...], preferred_element_type=F32)
    act = (jax.nn.silu(gate) * up).astype(BF16)
    weight = jnp.sum(jnp.where(lane == e.astype(F32), comb_ref[...], 0.0), axis=-1, keepdims=True)
    acc_ref[...] += weight * jnp.dot(act, wd_ref[...], preferred_element_type=F32)

    @pl.when((e == n_experts - 1) & (f == pl.num_programs(2) - 1))
    def _():
        _finish(x_ref, acc_ref, gf_ref, o_ref)


def _moe(x, g, wr, br, wg, wu, wd, layer, g_final, tm, tf):
    n, d = x.shape
    _, n_experts, _, ff = wg.shape
    final = g_final is not None
    blk = layer // 2
    in_specs = [pl.BlockSpec((tm, d), lambda i, e, f: (i, 0)),
                pl.BlockSpec((None, 1, d), lambda i, e, f: (layer, 0, 0)),
                pl.BlockSpec((d, LANES), lambda i, e, f: (0, 0)),
                pl.BlockSpec((1, LANES), lambda i, e, f: (0, 0)),
                pl.BlockSpec((None, None, d, tf), lambda i, e, f: (blk, e, 0, f)),
                pl.BlockSpec((None, None, d, tf), lambda i, e, f: (blk, e, 0, f)),
                pl.BlockSpec((None, None, tf, d), lambda i, e, f: (blk, e, f, 0))]
    args = [x, g, wr, br, wg, wu, wd]
    if final:
        in_specs.append(pl.BlockSpec((1, d), lambda i, e, f: (0, 0)))
        args.append(g_final.reshape(1, d))
    return pl.pallas_call(
        functools.partial(_moe_body, n_experts=n_experts, final=final),
        grid=(n // tm, n_experts, ff // tf),
        in_specs=in_specs,
        out_specs=pl.BlockSpec((tm, d), lambda i, e, f: (i, 0)),
        out_shape=jax.ShapeDtypeStruct((n, d), F32),
        scratch_shapes=[pltpu.VMEM((tm, d), BF16), pltpu.VMEM((tm, LANES), F32), pltpu.VMEM((tm, d), F32)],
        compiler_params=_compiler_params(("parallel", "arbitrary", "arbitrary")),
        name="moe",
    )(*args)


EXPERT_ROW_TILE = 512
MOE_DMA_ROWS = 256


def _router_body(x_ref, g_ref, wr_ref, br_ref, hn_ref, route_ref, *, n_experts):
    lane = lax.broadcasted_iota(jnp.int32, route_ref.shape, 1).astype(F32)
    hn = _rms(x_ref[...], g_ref[...])
    hn_ref[...] = hn
    i1, i2, g1, g2 = _route(hn, wr_ref, br_ref, n_experts, lane)
    route_ref[...] = jnp.where(lane == 0.0, i1, jnp.where(lane == 1.0, i2, jnp.where(
        lane == 2.0, g1, jnp.where(lane == 3.0, g2, 0.0))))


def _router(x, g, wr, br, layer, n_experts, tm):
    n, d = x.shape
    return pl.pallas_call(
        functools.partial(_router_body, n_experts=n_experts),
        grid=(n // tm,),
        in_specs=[pl.BlockSpec((tm, d), lambda i: (i, 0)),
                  pl.BlockSpec((None, 1, d), lambda i: (layer, 0, 0)),
                  pl.BlockSpec((d, LANES), lambda i: (0, 0)),
                  pl.BlockSpec((1, LANES), lambda i: (0, 0))],
        out_specs=[pl.BlockSpec((tm, d), lambda i: (i, 0)), pl.BlockSpec((tm, LANES), lambda i: (i, 0))],
        out_shape=[jax.ShapeDtypeStruct((n, d), F32), jax.ShapeDtypeStruct((n, LANES), F32)],
        compiler_params=_compiler_params(("parallel",)),
        name="moe_router",
    )(x, g, wr, br)


def _padded_router(w_router, b_router):
    d, n_experts = w_router.shape
    wr = jnp.zeros((d, LANES), F32).at[:, :n_experts].set(w_router.astype(F32))
    br = jnp.zeros((1, LANES), F32).at[0, :n_experts].set(b_router.astype(F32))
    return wr, br


def _dispatch_plan(experts, n_experts, n_tiles):
    flat = experts.reshape(-1)
    onehot = (flat[:, None] == jnp.arange(n_experts, dtype=jnp.int32)[None, :]).astype(jnp.int32)
    running = jnp.cumsum(onehot, axis=0)
    rank = jnp.sum(onehot * running, axis=1) - 1
    tiles_per_expert = (running[-1] + EXPERT_ROW_TILE - 1) // EXPERT_ROW_TILE
    tile_end = jnp.cumsum(tiles_per_expert)
    group_start = (tile_end - tiles_per_expert) * EXPERT_ROW_TILE
    slot = jnp.sum(onehot * group_start[None, :], axis=1) + rank
    tile = jnp.arange(n_tiles, dtype=jnp.int32)
    tile_expert = jnp.minimum(jnp.sum((tile[:, None] >= tile_end[None, :]).astype(jnp.int32), axis=1),
                              n_experts - 1)
    tile_valid = (tile < tile_end[-1]).astype(jnp.int32)
    return slot.astype(jnp.int32), tile_expert.astype(jnp.int32), tile_valid


def _row_copy(src_ref, src_row, dst_ref, dst_row, sem):
    return pltpu.make_async_copy(src_ref.at[pl.ds(src_row, 1)], dst_ref.at[pl.ds(dst_row, 1)], sem)


def _slot_tokens_body(slot_ref, zeros_ref, tok_ref, sem):
    clear = pltpu.make_async_copy(zeros_ref, tok_ref, sem)
    clear.start()
    clear.wait()

    def fill(i, c):
        for k in range(8):
            tok_ref[slot_ref[8 * i + k]] = 4 * i + k // 2
        return c

    lax.fori_loop(0, slot_ref.shape[0] // 8, fill, 0)


def _slot_tokens(slot, n_slots):
    smem = pl.BlockSpec(memory_space=pltpu.SMEM)
    return pl.pallas_call(
        _slot_tokens_body,
        in_specs=[smem, pl.BlockSpec(memory_space=pl.ANY)],
        out_specs=smem,
        out_shape=jax.ShapeDtypeStruct((n_slots,), jnp.int32),
        scratch_shapes=[pltpu.SemaphoreType.DMA(())],
        compiler_params=pltpu.CompilerParams(disable_bounds_checks=True),
        name="moe_slot_tokens",
    )(slot, jnp.zeros((n_slots,), jnp.int32))


def _experts_body(tok_ref, te_ref, tv_ref, hn_ref, wg_ref, wu_ref, wd_ref, y_ref, xbuf_ref, sem, *, tf):
    t = pl.program_id(0)
    last = pl.num_programs(0) - 1
    rows = y_ref.shape[0]
    n_f = wg_ref.shape[1] // tf
    cur = t % 2

    def copy(tile, buf, j):
        return _row_copy(hn_ref, tok_ref[tile * rows + j], xbuf_ref.at[buf], j, sem.at[buf])

    def start_all(tile, buf):
        for j in range(rows):
            copy(tile, buf, j).start()

    def wait_all(buf):
        pltpu.make_async_copy(hn_ref.at[pl.ds(0, rows)], xbuf_ref.at[buf], sem.at[buf]).wait()

    @pl.when(t == 0)
    def _():
        start_all(0, 0)

    @pl.when((t == 0) | (tv_ref[jnp.maximum(t - 1, 0)] != 0))
    def _():
        wait_all(cur)

    live = tv_ref[t] != 0

    @pl.when(live)
    def _():
        start_all(jnp.minimum(t + 1, last), 1 - cur)

    @pl.when(live)
    def _():
        xb = xbuf_ref[cur].astype(BF16)
        for f in range(n_f):
            cols = slice(f * tf, (f + 1) * tf)
            gate = jnp.dot(xb, wg_ref[:, cols], preferred_element_type=F32)
            up = jnp.dot(xb, wu_ref[:, cols], preferred_element_type=F32)
            act = (jax.nn.silu(gate) * up).astype(BF16)
            part = jnp.dot(act, wd_ref[cols, :], preferred_element_type=F32)
            if f == 0:
                y_ref[...] = part
            else:
                y_ref[...] += part

    @pl.when(jnp.logical_not(live))
    def _():
        y_ref[...] = jnp.zeros_like(y_ref)

    @pl.when((t == last) & live)
    def _():
        wait_all(1 - cur)


def _experts(tok, tile_expert, tile_valid, hn, wg, wu, wd, layer, tf):
    d = hn.shape[1]
    ff = wg.shape[3]
    blk = layer // 2
    n_tiles = tile_expert.shape[0]
    weights = lambda r, c: pl.BlockSpec((None, None, r, c), lambda t, tok, te, tv: (blk, te[t], 0, 0))
    return pl.pallas_call(
        functools.partial(_experts_body, tf=tf),
        grid_spec=pltpu.PrefetchScalarGridSpec(
            num_scalar_prefetch=3,
            grid=(n_tiles,),
            in_specs=[pl.BlockSpec(memory_space=pl.ANY), weights(d, ff), weights(d, ff), weights(ff, d)],
            out_specs=pl.BlockSpec((EXPERT_ROW_TILE, d), lambda t, tok, te, tv: (t, 0)),
            scratch_shapes=[pltpu.VMEM((2, EXPERT_ROW_TILE, d), F32), pltpu.SemaphoreType.DMA((2,))]),
        out_shape=jax.ShapeDtypeStruct((n_tiles * EXPERT_ROW_TILE, d), F32),
        compiler_params=pltpu.CompilerParams(dimension_semantics=("arbitrary",), disable_bounds_checks=True,
                                             vmem_limit_bytes=VMEM_LIMIT_BYTES),
        name="moe_experts",
    )(tok, tile_expert, tile_valid, hn, wg, wu, wd)


def _combine_body(slot_ref, x_ref, route_ref, y_ref, *rest, final):
    gf_ref = rest[0] if final else None
    o_ref, ya_ref, yb_ref, sem = rest[-4:]
    i = pl.program_id(0)
    cur = i % 2

    def start_step(step, buf):
        base = step * MOE_DMA_ROWS

        for j in range(MOE_DMA_ROWS):
            t = base + j
            _row_copy(y_ref, slot_ref[2 * t], ya_ref.at[buf], j, sem.at[0, buf]).start()
            _row_copy(y_ref, slot_ref[2 * t + 1], yb_ref.at[buf], j, sem.at[1, buf]).start()

    @pl.when(i == 0)
    def _():
        start_step(0, 0)

    @pl.when(i + 1 < pl.num_programs(0))
    def _():
        start_step(i + 1, 1 - cur)

    for k, buf_ref in enumerate((ya_ref, yb_ref)):
        pltpu.make_async_copy(y_ref.at[pl.ds(0, MOE_DMA_ROWS)], buf_ref.at[cur], sem.at[k, cur]).wait()
    y = x_ref[...] + route_ref[:, 2:3] * ya_ref[cur] + route_ref[:, 3:4] * yb_ref[cur]
    o_ref[...] = y if gf_ref is None else _rms(y, gf_ref[...])


def _combine(slot, x, route, y, g_final):
    n, d = x.shape
    final = g_final is not None
    in_specs = [pl.BlockSpec((MOE_DMA_ROWS, d), lambda i, s: (i, 0)),
                pl.BlockSpec((MOE_DMA_ROWS, LANES), lambda i, s: (i, 0)),
                pl.BlockSpec(memory_space=pl.ANY)]
    args = [slot, x, route, y]
    if final:
        in_specs.append(pl.BlockSpec((1, d), lambda i, s: (0, 0)))
        args.append(g_final.reshape(1, d))
    return pl.pallas_call(
        functools.partial(_combine_body, final=final),
        grid_spec=pltpu.PrefetchScalarGridSpec(
            num_scalar_prefetch=1,
            grid=(n // MOE_DMA_ROWS,),
            in_specs=in_specs,
            out_specs=pl.BlockSpec((MOE_DMA_ROWS, d), lambda i, s: (i, 0)),
            scratch_shapes=[pltpu.VMEM((2, MOE_DMA_ROWS, d), F32), pltpu.VMEM((2, MOE_DMA_ROWS, d), F32),
                            pltpu.SemaphoreType.DMA((2, 2))]),
        out_shape=jax.ShapeDtypeStruct((n, d), F32),
        compiler_params=pltpu.CompilerParams(dimension_semantics=("arbitrary",), disable_bounds_checks=True,
                                             vmem_limit_bytes=VMEM_LIMIT_BYTES),
        name="moe_combine",
    )(*args)


def _moe_sorted(x, hn, route, wg, wu, wd, layer, g_final, tf):
    n = x.shape[0]
    n_experts = wg.shape[1]
    n_tiles = 2 * n // EXPERT_ROW_TILE + n_experts
    slot, tile_expert, tile_valid = _dispatch_plan(route[:, :2].astype(jnp.int32), n_experts, n_tiles)
    tok = _slot_tokens(slot, n_tiles * EXPERT_ROW_TILE)
    y = _experts(tok, tile_expert, tile_valid, hn, wg, wu, wd, layer, tf)
    return _combine(slot, x, route, y, g_final)


PROMPT_ROW_TILE = 512
PROMPT_SCAN_STEPS = 64


def kernel(x_prompt, x_sample, cache_win_k, cache_win_v, state_ssm_re, state_ssm_im, rel_bias, norm_mix, w_in, ssm_a_re, ssm_a_im, ssm_log_dt, ssm_b_re, ssm_b_im, ssm_c_re, ssm_c_im, ssm_d, ssm_w_glu, ssm_b_glu, norm_attn_out, norm_ssm_out, w_out, norm_ffn, ffn_w_gate, ffn_w_up, ffn_w_down, moe_w_router, moe_b_router, moe_w_gate, moe_w_up, moe_w_down, norm_final):
    batch, seq, d_model = x_prompt.shape
    dec_batch, dec_seq, _ = x_sample.shape
    depth = w_in.shape[0]
    win, n_heads = cache_win_k.shape[2:4]
    attn_width = n_heads * HEAD_DIM
    n_groups, n_state = ssm_a_re.shape[1:]
    ssm_width = n_groups * SSM_GROUP
    assert w_in.shape[2] == 3 * attn_width + ssm_width and seq == MAX_WINDOW
    dec_rows = dec_batch * dec_seq

    band_bias = _band_bias(rel_bias)
    sample_tabs = _sample_tables(rel_bias, win, dec_seq)
    ar, ai, bb_re, bb_im = _s5_params(ssm_a_re, ssm_a_im, ssm_log_dt, ssm_b_re, ssm_b_im)
    cache_k = jnp.transpose(cache_win_k, (0, 1, 3, 4, 2))
    cache_v = jnp.transpose(cache_win_v, (0, 1, 3, 4, 2))

    rows3 = lambda a: a.reshape(a.shape[0], 1, a.shape[1])
    norm_mix3, norm_ffn3, b_glu3 = rows3(norm_mix), rows3(norm_ffn), rows3(ssm_b_glu)
    g_attn3, g_ssm3 = rows3(norm_attn_out), rows3(norm_ssm_out)
    w_in_b, w_out_b, w_glu_b = w_in.astype(BF16), w_out.astype(BF16), ssm_w_glu.astype(BF16)
    ffn_b = [w.astype(BF16) for w in (ffn_w_gate, ffn_w_up, ffn_w_down)]
    moe_b = [w.astype(BF16) for w in (moe_w_gate, moe_w_up, moe_w_down)]
    tf = ffn_w_gate.shape[2] // 2

    def mix_and_ffn(x, l, attn, y, tm, y_tiles_per_seq):
        g_final = norm_final if l == depth - 1 else None
        mix = functools.partial(_post_mix, attn, y, x, g_attn3, g_ssm3, w_glu_b, b_glu3, w_out_b, l, tm,
                                y_tiles_per_seq)
        if l % 2 == 0:
            return _ffn(mix(), norm_ffn3, *ffn_b, l, g_final, tm, tf)
        n_experts = moe_w_gate.shape[1]
        wr, br = _padded_router(moe_w_router[l // 2], moe_b_router[l // 2])
        if 2 * x.shape[0] >= n_experts * EXPERT_ROW_TILE:
            x = mix()
            hn, route = _router(x, norm_ffn3, wr, br, l, n_experts, tm)
            return _moe_sorted(x, hn, route, *moe_b, l, g_final, tf)
        return _moe(mix(), norm_ffn3, wr, br, *moe_b, l, g_final, tm, tf)

    xp = x_prompt.reshape(batch * seq, d_model)
    xs = x_sample.reshape(dec_rows, d_model)
    zero_state = jnp.zeros((1, batch, 2 * n_groups * n_state), F32)
    k_all = jnp.zeros((depth, batch, attn_width, seq), F32)
    v_all = jnp.zeros((depth, batch, attn_width, seq), F32)
    wx, wc = _pack_s5_weights(bb_re, bb_im, ssm_c_re, ssm_c_im)
    s5_consts = (wx, wc, ar.reshape(depth, 1, -1), ai.reshape(depth, 1, -1), ssm_d.reshape(depth, 1, -1))
    sample_h0 = _pack_state(state_ssm_re, state_ssm_im)
    prompt_h, sample_h, sample_k, sample_v = [], [], [], []
    for l in range(depth):
        qkv, k_all, v_all, u = _norm_proj_prompt(xp, norm_mix3, w_in_b, l, k_all, v_all, PROMPT_ROW_TILE)
        attn = _prompt_attn(qkv, band_bias, batch, seq, attn_width)
        y, h_fin = _s5_scan(u, *s5_consts, zero_state, l, 0, batch, PROMPT_SCAN_STEPS, True)
        xp = mix_and_ffn(xp, l, attn, y, PROMPT_ROW_TILE, seq // PROMPT_ROW_TILE)
        prompt_h.append(h_fin)

        proj = _norm_proj(xs, norm_mix3, w_in_b, l, dec_rows)
        q, k_new, v_new = (proj[:, i * attn_width:(i + 1) * attn_width].reshape(dec_batch, dec_seq, n_heads, HEAD_DIM)
                           for i in range(3))
        attn = _sample_attn(q, k_new, v_new, cache_k, cache_v, l, sample_tabs).reshape(dec_rows, attn_width)
        u = proj[:, 3 * attn_width:].reshape(dec_batch, dec_seq, ssm_width).transpose(1, 0, 2)
        y, h_fin = _s5_scan(u.reshape(dec_rows, ssm_width), *s5_consts, sample_h0, l, l, dec_batch, dec_seq, False)
        y = y.reshape(dec_seq, dec_batch, ssm_width).transpose(1, 0, 2).reshape(dec_rows, ssm_width)
        xs = mix_and_ffn(xs, l, attn, y, dec_rows, None)
        sample_h.append(h_fin)
        sample_k.append(k_new)
        sample_v.append(v_new)

    def window_out(a):
        return jnp.transpose(a.reshape(depth, batch, n_heads, HEAD_DIM, seq), (0, 1, 4, 2, 3))

    return (xp.reshape(batch, seq, d_model), xs.reshape(dec_batch, dec_seq, d_model),
            window_out(k_all), window_out(v_all), *_unpack_state(jnp.stack(prompt_h, 0), n_groups, n_state),
            jnp.stack(sample_k, 0), jnp.stack(sample_v, 0),
            *_unpack_state(jnp.stack(sample_h, 0), n_groups, n_state))
```

```python
import functools

import numpy as np
import jax
import jax.numpy as jnp
from jax import lax
from jax.experimental import pallas as pl
from jax.experimental.pallas import tpu as pltpu

F32 = jnp.float32
BF16 = jnp.bfloat16

HEAD_DIM = 64
DILATION_PATTERNS = ((128, 1), (512, 4), (2048, 16))
MAX_WINDOW = 2048
N_BUCKETS = 32
SSM_GROUP = 16
SSM_STATE = 64
RMS_EPS = 1e-6
NEG_BIG = -1e30
LOG2_E = 1.4426950408889634

LANES = 128
Q_TILE = 128
ATTN_UNROLL = 4
GROUPS_PER_TILE = LANES // SSM_GROUP
STATE_TILE = GROUPS_PER_TILE * SSM_STATE
VMEM_LIMIT_BYTES = 56 * 1024 * 1024


def _compiler_params(semantics):
    return pltpu.CompilerParams(dimension_semantics=semantics, vmem_limit_bytes=VMEM_LIMIT_BYTES)


def _rms(x, g):
    return x * lax.rsqrt(jnp.mean(x * x, axis=-1, keepdims=True) + RMS_EPS) * g


def _t5_causal_bucket(dist):
    max_exact = N_BUCKETS // 2
    d = np.asarray(dist, dtype=np.int64)
    large = max_exact + (np.log(np.maximum(d, max_exact) / max_exact)
                         / np.log(MAX_WINDOW / max_exact)
                         * (N_BUCKETS - max_exact)).astype(np.int64)
    large = np.minimum(large, N_BUCKETS - 1)
    return np.where(d < max_exact, d, large).astype(np.int32)


def _multiplicity(d):
    m = np.zeros(d.shape, np.int32)
    for window, dil in DILATION_PATTERNS:
        m += ((d >= 0) & (d <= window) & (d % dil == 0)).astype(np.int32)
    return m


def _norm_proj_body(x_ref, g_ref, w_ref, o_ref):
    hn = _rms(x_ref[...], g_ref[...])
    o_ref[...] = jnp.dot(hn.astype(BF16), w_ref[...], preferred_element_type=F32)


def _norm_proj(x, g, w, layer, tm):
    n, d = x.shape
    c = w.shape[2]
    return pl.pallas_call(
        _norm_proj_body,
        grid=(n // tm,),
        in_specs=[pl.BlockSpec((tm, d), lambda i: (i, 0)),
                  pl.BlockSpec((None, 1, d), lambda i: (layer, 0, 0)),
                  pl.BlockSpec((None, d, c), lambda i: (layer, 0, 0))],
        out_specs=pl.BlockSpec((tm, c), lambda i: (i, 0)),
        out_shape=jax.ShapeDtypeStruct((n, c), F32),
        compiler_params=_compiler_params(("parallel",)),
        name="norm_proj",
    )(x, g, w)


def _norm_proj_prompt_body(x_ref, g_ref, w_ref, _k_all, _v_all, qkv_ref, kt_ref, vt_ref, u_ref, *, attn_width):
    hn = _rms(x_ref[...], g_ref[...])
    proj = jnp.dot(hn.astype(BF16), w_ref[...], preferred_element_type=F32)
    qkv_ref[...] = proj[:, :3 * attn_width]
    kt_ref[...] = proj[:, attn_width:2 * attn_width].T
    vt_ref[...] = proj[:, 2 * attn_width:3 * attn_width].T
    u_ref[...] = proj[:, 3 * attn_width:]


def _norm_proj_prompt(x, g, w, layer, k_all, v_all, tm):
    n, d = x.shape
    c = w.shape[2]
    _, batch, attn_width, seq = k_all.shape
    ch = c - 3 * attn_width
    per_seq = seq // tm
    window = pl.BlockSpec((None, None, attn_width, tm), lambda i: (layer, i // per_seq, 0, i % per_seq))
    untouched = pl.BlockSpec(memory_space=pl.ANY)
    return pl.pallas_call(
        functools.partial(_norm_proj_prompt_body, attn_width=attn_width),
        grid=(n // tm,),
        in_specs=[pl.BlockSpec((tm, d), lambda i: (i, 0)),
                  pl.BlockSpec((None, 1, d), lambda i: (layer, 0, 0)),
                  pl.BlockSpec((None, d, c), lambda i: (layer, 0, 0)),
                  untouched, untouched],
        out_specs=[pl.BlockSpec((tm, 3 * attn_width), lambda i: (i, 0)), window, window,
                   pl.BlockSpec((tm, ch), lambda i: (i % per_seq, i // per_seq))],
        out_shape=[jax.ShapeDtypeStruct((n, 3 * attn_width), F32),
                   jax.ShapeDtypeStruct(k_all.shape, F32),
                   jax.ShapeDtypeStruct(v_all.shape, F32),
                   jax.ShapeDtypeStruct((seq, batch * ch), F32)],
        input_output_aliases={3: 1, 4: 2},
        compiler_params=_compiler_params(("parallel",)),
        name="norm_proj_prompt",
    )(x, g, w, k_all, v_all)


def _band_bias(rel_bias):
    period = 3 * Q_TILE
    n_heads = rel_bias.shape[1]
    rows0 = []
    for window, dil in DILATION_PATTERNS:
        assert window // dil == Q_TILE
        vec = rel_bias[_t5_causal_bucket((Q_TILE - np.arange(Q_TILE + 1)) * dil)].astype(F32)
        rows0.append(jnp.concatenate([vec.T, jnp.full((n_heads, period - Q_TILE - 1), NEG_BIG, F32)], axis=1))
    row0 = jnp.stack(rows0, 0)
    flat = jnp.tile(row0, (1, 1, Q_TILE))[:, :, :Q_TILE * (period - 1)]
    band = flat.reshape(len(DILATION_PATTERNS), n_heads, Q_TILE, period - 1)[:, :, :, :2 * Q_TILE]
    return band * LOG2_E


def _prompt_attn_body(q_ref, k_ref, v_ref, bias_ref, o_ref, op_ref, m_ref, l_ref):
    seq = q_ref.shape[0]
    lane = lax.broadcasted_iota(jnp.int32, (Q_TILE, LANES), 1)
    head0 = lane < HEAD_DIM
    scale = HEAD_DIM ** -0.5 * LOG2_E

    def rows(start, size, dil):
        return pl.ds(start, size) if dil == 1 else pl.ds(start, size, stride=dil)

    def tiles(p, dil, starts, first):
        nk = Q_TILE if first else 2 * Q_TILE
        bias = (bias_ref[p, :, :, Q_TILE:] if first else bias_ref[p]).reshape(2 * Q_TILE, nk)
        work = []
        for q_start, k_start in starts:
            qsl, ksl = rows(q_start, Q_TILE, dil), rows(k_start, nk, dil)
            qt = q_ref[qsl, :] * scale
            q2 = jnp.concatenate([jnp.where(head0, qt, 0.0), jnp.where(head0, 0.0, qt)], axis=0).astype(BF16)
            v_ones = jnp.concatenate([v_ref[ksl, :].astype(BF16), jnp.ones((nk, LANES), BF16)], axis=1)
            work.append([qsl, v_ones, q2, k_ref[ksl, :].astype(BF16)])
        for w in work:
            w[2] = lax.dot_general(w[2], w[3], (((1,), (1,)), ((), ())), preferred_element_type=F32) + bias
        for w in work:
            w[3] = jnp.max(w[2], axis=-1, keepdims=True)
        for w in work:
            w[2] = jnp.exp2((w[2] - w[3]).astype(BF16))
        for w in work:
            w[2] = jnp.dot(w[2], w[1], preferred_element_type=F32)
        for qsl, _, ol, m in work:
            op_ref[p, qsl, :] = jnp.where(head0, ol[:Q_TILE, :LANES], ol[Q_TILE:, :LANES])
            m_ref[p, qsl, :] = jnp.where(head0, m[:Q_TILE], m[Q_TILE:])
            l_ref[p, qsl, :] = jnp.where(head0, ol[:Q_TILE, LANES:], ol[Q_TILE:, LANES:])

    for p, (_, dil) in enumerate(DILATION_PATTERNS):
        n_tiles = seq // dil // Q_TILE
        stride = dil * Q_TILE
        unroll = min(ATTN_UNROLL, dil)
        if dil == 1:
            unroll = max(u for u in range(1, 2 * ATTN_UNROLL + 1) if (n_tiles - 1) % u == 0)
            for r in range(dil):
                tiles(p, dil, [(r, r)], True)

                def blk(t, c, p=p, dil=dil, r=r, stride=stride, unroll=unroll):
                    first_t = 1 + t * unroll
                    tiles(p, dil, [(r + stride * (first_t + j), r + stride * (first_t + j - 1))
                                   for j in range(unroll)], False)
                    return c
                lax.fori_loop(0, (n_tiles - 1) // unroll, blk, 0, unroll=True)
        else:
            assert dil % unroll == 0

            def classes(g, c, p=p, dil=dil, n_tiles=n_tiles, stride=stride, unroll=unroll):
                rs = [g * unroll + j for j in range(unroll)]
                tiles(p, dil, [(r, r) for r in rs], True)
                if n_tiles > 1:
                    def blk(t, c2):
                        tiles(p, dil, [(r + stride * t, r + stride * (t - 1)) for r in rs], False)
                        return c2
                    lax.fori_loop(1, n_tiles, blk, 0, unroll=True)
                return c
            lax.fori_loop(0, dil // unroll, classes, 0, unroll=True)

    def combine(t, carry):
        r = pl.ds(pl.multiple_of(t * Q_TILE, Q_TILE), Q_TILE)
        ms = [m_ref[p, r, :] for p in range(len(DILATION_PATTERNS))]
        mx = functools.reduce(jnp.maximum, ms)
        ws = [jnp.exp2(m - mx) for m in ms]
        num = sum(w * op_ref[p, r, :] for p, w in enumerate(ws))
        den = sum(w * l_ref[p, r, :] for p, w in enumerate(ws))
        o_ref[r, :] = num / den
        return carry

    lax.fori_loop(0, seq // Q_TILE, combine, 0)


def _prompt_attn(proj, bias, batch, seq, attn_width):
    n_pairs = attn_width // LANES
    n_pat = len(DILATION_PATTERNS)
    assert seq % (Q_TILE * max(d for _, d in DILATION_PATTERNS)) == 0
    return pl.pallas_call(
        _prompt_attn_body,
        grid=(batch, n_pairs),
        in_specs=[pl.BlockSpec((seq, LANES), lambda b, h: (b, h)),
                  pl.BlockSpec((seq, LANES), lambda b, h: (b, n_pairs + h)),
                  pl.BlockSpec((seq, LANES), lambda b, h: (b, 2 * n_pairs + h)),
                  pl.BlockSpec((n_pat, 2, Q_TILE, 2 * Q_TILE), lambda b, h: (0, h, 0, 0))],
        out_specs=pl.BlockSpec((seq, LANES), lambda b, h: (b, h)),
        out_shape=jax.ShapeDtypeStruct((batch * seq, attn_width), F32),
        scratch_shapes=[pltpu.VMEM((n_pat, seq, LANES), F32)] * 3,
        compiler_params=_compiler_params(("parallel", "parallel")),
        name="prompt_attn",
    )(proj, proj, proj, bias)


def _sample_tables(rel_bias, win, n_new):
    n_heads = rel_bias.shape[1]
    vec_t = rel_bias[_t5_causal_bucket(np.arange(win + n_new))].astype(F32).T
    flipped = vec_t[:, ::-1]
    t = np.arange(n_new)
    d_cache = win + t[:, None] - np.arange(win)[None, :]
    d_new = t[:, None] - t[None, :]
    b_cache = jnp.stack([flipped[:, n_new - 1 - i:n_new - 1 - i + win] for i in range(n_new)], 0)
    b_new = jnp.transpose(vec_t[:, np.maximum(d_new, 0)], (1, 0, 2))
    out = []
    for b, d in ((b_cache, d_cache), (b_new, d_new)):
        mult = _multiplicity(d)
        b = jnp.where((mult > 0)[:, None, :], b, NEG_BIG).reshape(n_new * n_heads, d.shape[1])
        m = np.broadcast_to(np.maximum(mult, 1)[:, None, :], (n_new, n_heads, d.shape[1]))
        out += [b, jnp.asarray(m.reshape(n_new * n_heads, d.shape[1]), F32)]
    return out


def _sample_attn_body(q_ref, kc_ref, kn_ref, vc_ref, vn_ref, bc_ref, mc_ref, bn_ref, mn_ref, hm_ref, o_ref):
    n_new = o_ref.shape[0]
    win = kc_ref.shape[-1]
    qb = (q_ref[...] * HEAD_DIM ** -0.5).astype(BF16)
    s_c = jnp.dot(qb, kc_ref[...].reshape(-1, win).astype(BF16), preferred_element_type=F32) + bc_ref[...]
    s_n = jnp.dot(qb, kn_ref[...].astype(BF16), preferred_element_type=F32) + bn_ref[...]
    m = jnp.maximum(jnp.max(s_c, axis=-1, keepdims=True), jnp.max(s_n, axis=-1, keepdims=True))
    e_c = jnp.exp(s_c - m) * mc_ref[...]
    e_n = jnp.exp(s_n - m) * mn_ref[...]
    den = jnp.sum(e_c, axis=-1, keepdims=True) + jnp.sum(e_n, axis=-1, keepdims=True)
    nt = (((1,), (1,)), ((), ()))
    o = (lax.dot_general(e_c.astype(BF16), vc_ref[...].reshape(-1, win).astype(BF16), nt,
                         preferred_element_type=F32)
         + lax.dot_general(e_n.astype(BF16), vn_ref[...].astype(BF16), nt, preferred_element_type=F32))
    o = o / den * hm_ref[...]
    o_ref[...] = jnp.sum(o.reshape(n_new, -1, o.shape[-1]), axis=1)


def _sample_attn(q, k_new, v_new, cache_k, cache_v, layer, tables):
    depth, batch, n_heads, hd, win = cache_k.shape
    n_new = q.shape[1]
    width = n_heads * hd
    rows = n_new * n_heads
    eye = jnp.eye(n_heads, dtype=F32)
    q_bd = jnp.einsum('bthd,hg->btghd', q, eye).reshape(batch, rows, width)
    kn = jnp.transpose(k_new, (0, 2, 3, 1)).reshape(batch, width, n_new)
    vn = jnp.transpose(v_new, (0, 2, 3, 1)).reshape(batch, width, n_new)
    head_mask = jnp.asarray(np.kron(np.tile(np.eye(n_heads), (n_new, 1)), np.ones((1, hd))), F32)
    cache_spec = pl.BlockSpec((None, None, n_heads, hd, win), lambda b: (layer, b, 0, 0, 0))
    new_spec = pl.BlockSpec((None, width, n_new), lambda b: (b, 0, 0))
    const = lambda a: pl.BlockSpec(a.shape, lambda b: (0, 0))
    return pl.pallas_call(
        _sample_attn_body,
        grid=(batch,),
        in_specs=[pl.BlockSpec((None, rows, width), lambda b: (b, 0, 0)),
                  cache_spec, new_spec, cache_spec, new_spec] + [const(t) for t in tables] + [const(head_mask)],
        out_specs=pl.BlockSpec((None, n_new, width), lambda b: (b, 0, 0)),
        out_shape=jax.ShapeDtypeStruct((batch, n_new, width), F32),
        compiler_params=_compiler_params(("parallel",)),
        name="sample_attn",
    )(q_bd, cache_k, kn, cache_v, vn, *tables, head_mask)


def _s5_param_body(are_ref, aim_ref, ldt_ref, bre_ref, bim_ref, ar_ref, ai_ref, bbre_ref, bbim_ref):
    a_re, a_im = are_ref[...], aim_ref[...]
    dt = jnp.exp(ldt_ref[...])
    mag = jnp.exp(dt * a_re)
    ar, ai = mag * jnp.cos(dt * a_im), mag * jnp.sin(dt * a_im)
    nr, ni = ar - 1.0, ai
    den = a_re * a_re + a_im * a_im
    fr = (nr * a_re + ni * a_im) / den
    fi = (ni * a_re - nr * a_im) / den
    br, bi = bre_ref[...], bim_ref[...]
    ar_ref[...] = ar
    ai_ref[...] = ai
    bbre_ref[...] = fr[:, None, :] * br - fi[:, None, :] * bi
    bbim_ref[...] = fr[:, None, :] * bi + fi[:, None, :] * br


def _s5_params(a_re, a_im, log_dt, b_re, b_im):
    depth, g, p = a_re.shape
    c = b_re.shape[-1]
    n = depth * g
    full = lambda shape: pl.BlockSpec(shape, lambda: (0,) * len(shape))
    ar, ai, bbre, bbim = pl.pallas_call(
        _s5_param_body,
        in_specs=[full((n, p)), full((n, p)), full((n, 1)), full((n, c, p)), full((n, c, p))],
        out_specs=[full((n, p)), full((n, p)), full((n, c, p)), full((n, c, p))],
        out_shape=[jax.ShapeDtypeStruct((n, p), F32), jax.ShapeDtypeStruct((n, p), F32),
                   jax.ShapeDtypeStruct((n, c, p), F32), jax.ShapeDtypeStruct((n, c, p), F32)],
        name="s5_params",
    )(a_re.reshape(n, p), a_im.reshape(n, p), log_dt.reshape(n, 1),
      jnp.swapaxes(b_re, -1, -2).reshape(n, c, p), jnp.swapaxes(b_im, -1, -2).reshape(n, c, p))
    return (ar.reshape(depth, g, p), ai.reshape(depth, g, p),
            bbre.reshape(depth, g, c, p), bbim.reshape(depth, g, c, p))


def _pack_s5_weights(bb_re, bb_im, c_re, c_im):
    depth, g, c, p = bb_re.shape
    tiles = g // GROUPS_PER_TILE
    eye = jnp.eye(GROUPS_PER_TILE, dtype=F32)

    def inp(w):
        w = w.astype(F32).reshape(depth, tiles, GROUPS_PER_TILE, c, p)
        return jnp.einsum('ljgcp,gh->ljgchp', w, eye).reshape(depth, tiles, LANES, STATE_TILE)

    def outp(w):
        w = w.astype(F32).reshape(depth, tiles, GROUPS_PER_TILE, c, p)
        return jnp.einsum('ljgcp,gh->ljgphc', w, eye).reshape(depth, tiles, STATE_TILE, LANES)

    wx = jnp.concatenate([inp(bb_re), inp(bb_im)], axis=-1).astype(BF16)
    wc = jnp.concatenate([outp(c_re), -outp(c_im)], axis=2).astype(BF16)
    return wx, wc


def _pack_state(h_re, h_im):
    lead, (g, p) = h_re.shape[:-2], h_re.shape[-2:]
    tiles = g // GROUPS_PER_TILE
    st = jnp.stack([h_re.reshape(*lead, tiles, STATE_TILE), h_im.reshape(*lead, tiles, STATE_TILE)], axis=-2)
    return st.reshape(*lead, tiles * 2 * STATE_TILE).astype(F32)


def _unpack_state(h, g, p):
    lead = h.shape[:-1]
    st = h.reshape(*lead, g // GROUPS_PER_TILE, 2, STATE_TILE)
    return st[..., 0, :].reshape(*lead, g, p), st[..., 1, :].reshape(*lead, g, p)


def _s5_scan_body(u_ref, wx_ref, wc_ref, ar_ref, ai_ref, d_ref, h0_ref, y_ref, hout_ref, us_ref, ys_ref, xs_ref,
                  h_ref, *, bsz, steps, batch_on_lanes):
    tiles = wx_ref.shape[0]
    ch = tiles * LANES

    @pl.when(pl.program_id(0) == 0)
    def _():
        h_ref[...] = h0_ref[...]

    for j in range(tiles):
        if batch_on_lanes:
            for b in range(bsz):
                us_ref[j, pl.ds(b, steps, stride=bsz), :] = u_ref[:, ch * b + LANES * j:ch * b + LANES * (j + 1)]
        else:
            us_ref[j] = u_ref[:, LANES * j:LANES * (j + 1)]

    for j in range(tiles):
        xs_ref[:, 2 * STATE_TILE * j:2 * STATE_TILE * (j + 1)] = jnp.dot(
            us_ref[j].astype(BF16), wx_ref[j], preferred_element_type=F32)

    for j in range(tiles):
        re = slice(2 * STATE_TILE * j, 2 * STATE_TILE * j + STATE_TILE)
        im = slice(2 * STATE_TILE * j + STATE_TILE, 2 * STATE_TILE * (j + 1))
        a_r = jnp.broadcast_to(ar_ref[:, STATE_TILE * j:STATE_TILE * (j + 1)], (bsz, STATE_TILE))
        a_i = jnp.broadcast_to(ai_ref[:, STATE_TILE * j:STATE_TILE * (j + 1)], (bsz, STATE_TILE))

        def step(t, carry, re=re, im=im, a_r=a_r, a_i=a_i):
            hr, hi = carry
            r = pl.ds(pl.multiple_of(t * bsz, bsz), bsz)
            nhr = a_r * hr - a_i * hi + xs_ref[r, re]
            nhi = a_r * hi + a_i * hr + xs_ref[r, im]
            xs_ref[r, re] = nhr
            xs_ref[r, im] = nhi
            return nhr, nhi

        hr, hi = lax.fori_loop(0, steps, step, (h_ref[:, re], h_ref[:, im]), unroll=True)
        h_ref[:, re] = hr
        h_ref[:, im] = hi

    for j in range(tiles):
        cols = slice(LANES * j, LANES * (j + 1))
        hb = xs_ref[:, 2 * STATE_TILE * j:2 * STATE_TILE * (j + 1)].astype(BF16)
        y = jnp.dot(hb, wc_ref[j], preferred_element_type=F32) + d_ref[:, cols] * us_ref[j]
        if batch_on_lanes:
            ys_ref[j] = y
            for b in range(bsz):
                y_ref[:, ch * b + LANES * j:ch * b + LANES * (j + 1)] = ys_ref[j, pl.ds(b, steps, stride=bsz), :]
        else:
            y_ref[:, cols] = y
    hout_ref[...] = h_ref[...]


def _s5_scan(u, wx, wc, ar, ai, d_skip, h0, layer, h0_layer, bsz, steps, batch_on_lanes):
    tiles = wx.shape[1]
    ch = tiles * LANES
    width = tiles * 2 * STATE_TILE
    rows = steps * bsz
    if batch_on_lanes:
        n_chunks = u.shape[0] // steps
        io_spec = pl.BlockSpec((steps, bsz * ch), lambda i: (i, 0))
    else:
        n_chunks = 1
        assert u.shape == (rows, ch)
        io_spec = pl.BlockSpec((rows, ch), lambda i: (0, 0))
    of_layer = lambda a: pl.BlockSpec((None,) + a.shape[1:], lambda i: (layer,) + (0,) * (a.ndim - 1))
    return pl.pallas_call(
        functools.partial(_s5_scan_body, bsz=bsz, steps=steps, batch_on_lanes=batch_on_lanes),
        grid=(n_chunks,),
        in_specs=[io_spec, of_layer(wx), of_layer(wc), of_layer(ar), of_layer(ai), of_layer(d_skip),
                  pl.BlockSpec((None, bsz, width), lambda i: (h0_layer, 0, 0))],
        out_specs=[io_spec, pl.BlockSpec((bsz, width), lambda i: (0, 0))],
        out_shape=[jax.ShapeDtypeStruct(u.shape, F32), jax.ShapeDtypeStruct((bsz, width), F32)],
        scratch_shapes=[pltpu.VMEM((tiles, rows, LANES), F32), pltpu.VMEM((tiles, rows, LANES), F32),
                        pltpu.VMEM((rows, width), F32), pltpu.VMEM((bsz, width), F32)],
        compiler_params=_compiler_params(("arbitrary",)),
        name="s5_scan",
    )(u, wx, wc, ar, ai, d_skip, h0)


def _post_mix_body(attn_ref, y_ref, x_ref, ga_ref, gs_ref, wglu_ref, bglu_ref, wo_ref, o_ref):
    aw = attn_ref.shape[1]
    an = _rms(attn_ref[...], ga_ref[...])
    z = jax.nn.gelu(y_ref[...])
    gate = jax.nn.sigmoid(jnp.dot(z.astype(BF16), wglu_ref[...], preferred_element_type=F32) + bglu_ref[...])
    sn = _rms(z * gate, gs_ref[...])
    o_ref[...] = (x_ref[...]
                  + jnp.dot(an.astype(BF16), wo_ref[:aw, :], preferred_element_type=F32)
                  + jnp.dot(sn.astype(BF16), wo_ref[aw:, :], preferred_element_type=F32))


def _post_mix(attn, y, x, g_attn, g_ssm, w_glu, b_glu, w_out, layer, tm, y_tiles_per_seq):
    n, d = x.shape
    aw, sw = attn.shape[1], w_glu.shape[1]
    row = lambda w: pl.BlockSpec((tm, w), lambda i: (i, 0))
    vec = lambda w: pl.BlockSpec((None, 1, w), lambda i: (layer, 0, 0))
    mat = lambda r, c: pl.BlockSpec((None, r, c), lambda i: (layer, 0, 0))
    if y_tiles_per_seq is None:
        y_spec = row(sw)
    else:
        y_spec = pl.BlockSpec((tm, sw), lambda i: (i % y_tiles_per_seq, i // y_tiles_per_seq))
    return pl.pallas_call(
        _post_mix_body,
        grid=(n // tm,),
        in_specs=[row(aw), y_spec, row(d), vec(aw), vec(sw), mat(sw, sw), vec(sw), mat(aw + sw, d)],
        out_specs=row(d),
        out_shape=jax.ShapeDtypeStruct((n, d), F32),
        compiler_params=_compiler_params(("parallel",)),
        name="post_mix",
    )(attn, y, x, g_attn, g_ssm, w_glu, b_glu, w_out)


def _finish(x_ref, acc_ref, gf_ref, o_ref):
    y = x_ref[...] + acc_ref[...]
    o_ref[...] = y if gf_ref is None else _rms(y, gf_ref[...])


def _ffn_body(x_ref, g_ref, wg_ref, wu_ref, wd_ref, *rest, final, tf):
    gf_ref = rest[0] if final else None
    o_ref, acc_ref = rest[-2:]
    hn = _rms(x_ref[...], g_ref[...]).astype(BF16)
    for f in range(wg_ref.shape[1] // tf):
        cols = slice(f * tf, (f + 1) * tf)
        gate = jnp.dot(hn, wg_ref[:, cols], preferred_element_type=F32)
        up = jnp.dot(hn, wu_ref[:, cols], preferred_element_type=F32)
        act = (jax.nn.silu(gate) * up).astype(BF16)
        part = jnp.dot(act, wd_ref[cols, :], preferred_element_type=F32)
        if f == 0:
            acc_ref[...] = part
        else:
            acc_ref[...] += part
    _finish(x_ref, acc_ref, gf_ref, o_ref)


def _ffn(x, g, wg, wu, wd, layer, g_final, tm, tf):
    n, d = x.shape
    ff = wg.shape[2]
    final = g_final is not None
    blk = layer // 2
    in_specs = [pl.BlockSpec((tm, d), lambda i: (i, 0)),
                pl.BlockSpec((None, 1, d), lambda i: (layer, 0, 0)),
                pl.BlockSpec((None, d, ff), lambda i: (blk, 0, 0)),
                pl.BlockSpec((None, d, ff), lambda i: (blk, 0, 0)),
                pl.BlockSpec((None, ff, d), lambda i: (blk, 0, 0))]
    args = [x, g, wg, wu, wd]
    if final:
        in_specs.append(pl.BlockSpec((1, d), lambda i: (0, 0)))
        args.append(g_final.reshape(1, d))
    return pl.pallas_call(
        functools.partial(_ffn_body, final=final, tf=tf),
        grid=(n // tm,),
        in_specs=in_specs,
        out_specs=pl.BlockSpec((tm, d), lambda i: (i, 0)),
        out_shape=jax.ShapeDtypeStruct((n, d), F32),
        scratch_shapes=[pltpu.VMEM((tm, d), F32)],
        compiler_params=_compiler_params(("parallel",)),
        name="ffn_dense",
    )(*args)


def _route(hn, wr_ref, br_ref, n_experts, lane):
    w = wr_ref[...]
    hn_hi, w_hi = hn.astype(BF16), w.astype(BF16)
    hn_lo, w_lo = (hn - hn_hi.astype(F32)).astype(BF16), (w - w_hi.astype(F32)).astype(BF16)
    dot = functools.partial(jnp.dot, preferred_element_type=F32)
    logits = dot(hn_hi, w_hi) + dot(hn_hi, w_lo) + dot(hn_lo, w_hi) + br_ref[...]
    logits = jnp.where(lane < n_experts, logits, -jnp.inf)
    m1 = jnp.max(logits, axis=-1, keepdims=True)
    i1 = jnp.min(jnp.where(logits == m1, lane, float(LANES)), axis=-1, keepdims=True)
    rest_logits = jnp.where(lane == i1, -jnp.inf, logits)
    m2 = jnp.max(rest_logits, axis=-1, keepdims=True)
    i2 = jnp.min(jnp.where(rest_logits == m2, lane, float(LANES)), axis=-1, keepdims=True)
    e2 = jnp.exp(m2 - m1)
    return i1, i2, 1.0 / (1.0 + e2), e2 / (1.0 + e2)


def _moe_body(x_ref, g_ref, wr_ref, br_ref, wg_ref, wu_ref, wd_ref, *rest, n_experts, final):
    gf_ref = rest[0] if final else None
    o_ref, hn_ref, comb_ref, acc_ref = rest[-4:]
    e, f = pl.program_id(1), pl.program_id(2)
    lane = lax.broadcasted_iota(jnp.int32, comb_ref.shape, 1).astype(F32)

    @pl.when((e == 0) & (f == 0))
    def _():
        hn = _rms(x_ref[...], g_ref[...])
        hn_ref[...] = hn.astype(BF16)
        acc_ref[...] = jnp.zeros_like(acc_ref)
        i1, i2, g1, g2 = _route(hn, wr_ref, br_ref, n_experts, lane)
        comb_ref[...] = jnp.where(lane == i1, g1, 0.0) + jnp.where(lane == i2, g2, 0.0)

    hn = hn_ref[...]
    gate = jnp.dot(hn, wg_ref[...], preferred_element_type=F32)
    up = jnp.dot(hn, wu_ref[...], preferred_element_type=F32)
    act = (jax.nn.silu(gate) * up).astype(BF16)
    weight = jnp.sum(jnp.where(lane == e.astype(F32), comb_ref[...], 0.0), axis=-1, keepdims=True)
    acc_ref[...] += weight * jnp.dot(act, wd_ref[...], preferred_element_type=F32)

    @pl.when((e == n_experts - 1) & (f == pl.num_programs(2) - 1))
    def _():
        _finish(x_ref, acc_ref, gf_ref, o_ref)


def _moe(x, g, wr, br, wg, wu, wd, layer, g_final, tm, tf):
    n, d = x.shape
    _, n_experts, _, ff = wg.shape
    final = g_final is not None
    blk = layer // 2
    in_specs = [pl.BlockSpec((tm, d), lambda i, e, f: (i, 0)),
                pl.BlockSpec((None, 1, d), lambda i, e, f: (layer, 0, 0)),
                pl.BlockSpec((d, LANES), lambda i, e, f: (0, 0)),
                pl.BlockSpec((1, LANES), lambda i, e, f: (0, 0)),
                pl.BlockSpec((None, None, d, tf), lambda i, e, f: (blk, e, 0, f)),
                pl.BlockSpec((None, None, d, tf), lambda i, e, f: (blk, e, 0, f)),
                pl.BlockSpec((None, None, tf, d), lambda i, e, f: (blk, e, f, 0))]
    args = [x, g, wr, br, wg, wu, wd]
    if final:
        in_specs.append(pl.BlockSpec((1, d), lambda i, e, f: (0, 0)))
        args.append(g_final.reshape(1, d))
    return pl.pallas_call(
        functools.partial(_moe_body, n_experts=n_experts, final=final),
        grid=(n // tm, n_experts, ff // tf),
        in_specs=in_specs,
        out_specs=pl.BlockSpec((tm, d), lambda i, e, f: (i, 0)),
        out_shape=jax.ShapeDtypeStruct((n, d), F32),
        scratch_shapes=[pltpu.VMEM((tm, d), BF16), pltpu.VMEM((tm, LANES), F32), pltpu.VMEM((tm, d), F32)],
        compiler_params=_compiler_params(("parallel", "arbitrary", "arbitrary")),
        name="moe",
    )(*args)


EXPERT_ROW_TILE = 512
MOE_DMA_ROWS = 256


def _router_body(x_ref, g_ref, wr_ref, br_ref, hn_ref, route_ref, *, n_experts):
    lane = lax.broadcasted_iota(jnp.int32, route_ref.shape, 1).astype(F32)
    hn = _rms(x_ref[...], g_ref[...])
    hn_ref[...] = hn
    i1, i2, g1, g2 = _route(hn, wr_ref, br_ref, n_experts, lane)
    route_ref[...] = jnp.where(lane == 0.0, i1, jnp.where(lane == 1.0, i2, jnp.where(
        lane == 2.0, g1, jnp.where(lane == 3.0, g2, 0.0))))


def _router(x, g, wr, br, layer, n_experts, tm):
    n, d = x.shape
    return pl.pallas_call(
        functools.partial(_router_body, n_experts=n_experts),
        grid=(n // tm,),
        in_specs=[pl.BlockSpec((tm, d), lambda i: (i, 0)),
                  pl.BlockSpec((None, 1, d), lambda i: (layer, 0, 0)),
                  pl.BlockSpec((d, LANES), lambda i: (0, 0)),
                  pl.BlockSpec((1, LANES), lambda i: (0, 0))],
        out_specs=[pl.BlockSpec((tm, d), lambda i: (i, 0)), pl.BlockSpec((tm, LANES), lambda i: (i, 0))],
        out_shape=[jax.ShapeDtypeStruct((n, d), F32), jax.ShapeDtypeStruct((n, LANES), F32)],
        compiler_params=_compiler_params(("parallel",)),
        name="moe_router",
    )(x, g, wr, br)


def _padded_router(w_router, b_router):
    d, n_experts = w_router.shape
    wr = jnp.zeros((d, LANES), F32).at[:, :n_experts].set(w_router.astype(F32))
    br = jnp.zeros((1, LANES), F32).at[0, :n_experts].set(b_router.astype(F32))
    return wr, br


def _dispatch_plan(experts, n_experts, n_tiles):
    flat = experts.reshape(-1)
    onehot = (flat[:, None] == jnp.arange(n_experts, dtype=jnp.int32)[None, :]).astype(jnp.int32)
    running = jnp.cumsum(onehot, axis=0)
    rank = jnp.sum(onehot * running, axis=1) - 1
    tiles_per_expert = (running[-1] + EXPERT_ROW_TILE - 1) // EXPERT_ROW_TILE
    tile_end = jnp.cumsum(tiles_per_expert)
    group_start = (tile_end - tiles_per_expert) * EXPERT_ROW_TILE
    slot = jnp.sum(onehot * group_start[None, :], axis=1) + rank
    tile = jnp.arange(n_tiles, dtype=jnp.int32)
    tile_expert = jnp.minimum(jnp.sum((tile[:, None] >= tile_end[None, :]).astype(jnp.int32), axis=1),
                              n_experts - 1)
    tile_valid = (tile < tile_end[-1]).astype(jnp.int32)
    return slot.astype(jnp.int32), tile_expert.astype(jnp.int32), tile_valid


def _row_copy(src_ref, src_row, dst_ref, dst_row, sem):
    return pltpu.make_async_copy(src_ref.at[pl.ds(src_row, 1)], dst_ref.at[pl.ds(dst_row, 1)], sem)


def _slot_tokens_body(slot_ref, zeros_ref, tok_ref, sem):
    clear = pltpu.make_async_copy(zeros_ref, tok_ref, sem)
    clear.start()
    clear.wait()

    def fill(i, c):
        for k in range(8):
            tok_ref[slot_ref[8 * i + k]] = 4 * i + k // 2
        return c

    lax.fori_loop(0, slot_ref.shape[0] // 8, fill, 0)


def _slot_tokens(slot, n_slots):
    smem = pl.BlockSpec(memory_space=pltpu.SMEM)
    return pl.pallas_call(
        _slot_tokens_body,
        in_specs=[smem, pl.BlockSpec(memory_space=pl.ANY)],
        out_specs=smem,
        out_shape=jax.ShapeDtypeStruct((n_slots,), jnp.int32),
        scratch_shapes=[pltpu.SemaphoreType.DMA(())],
        compiler_params=pltpu.CompilerParams(disable_bounds_checks=True),
        name="moe_slot_tokens",
    )(slot, jnp.zeros((n_slots,), jnp.int32))


def _experts_body(tok_ref, te_ref, tv_ref, hn_ref, wg_ref, wu_ref, wd_ref, y_ref, xbuf_ref, sem, *, tf):
    t = pl.program_id(0)
    last = pl.num_programs(0) - 1
    rows = y_ref.shape[0]
    n_f = wg_ref.shape[1] // tf
    cur = t % 2

    def copy(tile, buf, j):
        return _row_copy(hn_ref, tok_ref[tile * rows + j], xbuf_ref.at[buf], j, sem.at[buf])

    def start_all(tile, buf):
        for j in range(rows):
            copy(tile, buf, j).start()

    def wait_all(buf):
        pltpu.make_async_copy(hn_ref.at[pl.ds(0, rows)], xbuf_ref.at[buf], sem.at[buf]).wait()

    @pl.when(t == 0)
    def _():
        start_all(0, 0)

    @pl.when((t == 0) | (tv_ref[jnp.maximum(t - 1, 0)] != 0))
    def _():
        wait_all(cur)

    live = tv_ref[t] != 0

    @pl.when(live)
    def _():
        start_all(jnp.minimum(t + 1, last), 1 - cur)

    @pl.when(live)
    def _():
        xb = xbuf_ref[cur].astype(BF16)
        for f in range(n_f):
            cols = slice(f * tf, (f + 1) * tf)
            gate = jnp.dot(xb, wg_ref[:, cols], preferred_element_type=F32)
            up = jnp.dot(xb, wu_ref[:, cols], preferred_element_type=F32)
            act = (jax.nn.silu(gate) * up).astype(BF16)
            part = jnp.dot(act, wd_ref[cols, :], preferred_element_type=F32)
            if f == 0:
                y_ref[...] = part
            else:
                y_ref[...] += part

    @pl.when(jnp.logical_not(live))
    def _():
        y_ref[...] = jnp.zeros_like(y_ref)

    @pl.when((t == last) & live)
    def _():
        wait_all(1 - cur)


def _experts(tok, tile_expert, tile_valid, hn, wg, wu, wd, layer, tf):
    d = hn.shape[1]
    ff = wg.shape[3]
    blk = layer // 2
    n_tiles = tile_expert.shape[0]
    weights = lambda r, c: pl.BlockSpec((None, None, r, c), lambda t, tok, te, tv: (blk, te[t], 0, 0))
    return pl.pallas_call(
        functools.partial(_experts_body, tf=tf),
        grid_spec=pltpu.PrefetchScalarGridSpec(
            num_scalar_prefetch=3,
            grid=(n_tiles,),
            in_specs=[pl.BlockSpec(memory_space=pl.ANY), weights(d, ff), weights(d, ff), weights(ff, d)],
            out_specs=pl.BlockSpec((EXPERT_ROW_TILE, d), lambda t, tok, te, tv: (t, 0)),
            scratch_shapes=[pltpu.VMEM((2, EXPERT_ROW_TILE, d), F32), pltpu.SemaphoreType.DMA((2,))]),
        out_shape=jax.ShapeDtypeStruct((n_tiles * EXPERT_ROW_TILE, d), F32),
        compiler_params=pltpu.CompilerParams(dimension_semantics=("arbitrary",), disable_bounds_checks=True,
                                             vmem_limit_bytes=VMEM_LIMIT_BYTES),
        name="moe_experts",
    )(tok, tile_expert, tile_valid, hn, wg, wu, wd)


def _combine_body(slot_ref, x_ref, route_ref, y_ref, *rest, final):
    gf_ref = rest[0] if final else None
    o_ref, ya_ref, yb_ref, sem = rest[-4:]
    i = pl.program_id(0)
    cur = i % 2

    def start_step(step, buf):
        base = step * MOE_DMA_ROWS

        for j in range(MOE_DMA_ROWS):
            t = base + j
            _row_copy(y_ref, slot_ref[2 * t], ya_ref.at[buf], j, sem.at[0, buf]).start()
            _row_copy(y_ref, slot_ref[2 * t + 1], yb_ref.at[buf], j, sem.at[1, buf]).start()

    @pl.when(i == 0)
    def _():
        start_step(0, 0)

    @pl.when(i + 1 < pl.num_programs(0))
    def _():
        start_step(i + 1, 1 - cur)

    for k, buf_ref in enumerate((ya_ref, yb_ref)):
        pltpu.make_async_copy(y_ref.at[pl.ds(0, MOE_DMA_ROWS)], buf_ref.at[cur], sem.at[k, cur]).wait()
    y = x_ref[...] + route_ref[:, 2:3] * ya_ref[cur] + route_ref[:, 3:4] * yb_ref[cur]
    o_ref[...] = y if gf_ref is None else _rms(y, gf_ref[...])


def _combine(slot, x, route, y, g_final):
    n, d = x.shape
    final = g_final is not None
    in_specs = [pl.BlockSpec((MOE_DMA_ROWS, d), lambda i, s: (i, 0)),
                pl.BlockSpec((MOE_DMA_ROWS, LANES), lambda i, s: (i, 0)),
                pl.BlockSpec(memory_space=pl.ANY)]
    args = [slot, x, route, y]
    if final:
        in_specs.append(pl.BlockSpec((1, d), lambda i, s: (0, 0)))
        args.append(g_final.reshape(1, d))
    return pl.pallas_call(
        functools.partial(_combine_body, final=final),
        grid_spec=pltpu.PrefetchScalarGridSpec(
            num_scalar_prefetch=1,
            grid=(n // MOE_DMA_ROWS,),
            in_specs=in_specs,
            out_specs=pl.BlockSpec((MOE_DMA_ROWS, d), lambda i, s: (i, 0)),
            scratch_shapes=[pltpu.VMEM((2, MOE_DMA_ROWS, d), F32), pltpu.VMEM((2, MOE_DMA_ROWS, d), F32),
                            pltpu.SemaphoreType.DMA((2, 2))]),
        out_shape=jax.ShapeDtypeStruct((n, d), F32),
        compiler_params=pltpu.CompilerParams(dimension_semantics=("arbitrary",), disable_bounds_checks=True,
                                             vmem_limit_bytes=VMEM_LIMIT_BYTES),
        name="moe_combine",
    )(*args)


def _moe_sorted(x, hn, route, wg, wu, wd, layer, g_final, tf):
    n = x.shape[0]
    n_experts = wg.shape[1]
    n_tiles = 2 * n // EXPERT_ROW_TILE + n_experts
    slot, tile_expert, tile_valid = _dispatch_plan(route[:, :2].astype(jnp.int32), n_experts, n_tiles)
    tok = _slot_tokens(slot, n_tiles * EXPERT_ROW_TILE)
    y = _experts(tok, tile_expert, tile_valid, hn, wg, wu, wd, layer, tf)
    return _combine(slot, x, route, y, g_final)


PROMPT_ROW_TILE = 512
PROMPT_SCAN_STEPS = 64


def kernel(x_prompt, x_sample, cache_win_k, cache_win_v, state_ssm_re, state_ssm_im, rel_bias, norm_mix, w_in, ssm_a_re, ssm_a_im, ssm_log_dt, ssm_b_re, ssm_b_im, ssm_c_re, ssm_c_im, ssm_d, ssm_w_glu, ssm_b_glu, norm_attn_out, norm_ssm_out, w_out, norm_ffn, ffn_w_gate, ffn_w_up, ffn_w_down, moe_w_router, moe_b_router, moe_w_gate, moe_w_up, moe_w_down, norm_final):
    batch, seq, d_model = x_prompt.shape
    dec_batch, dec_seq, _ = x_sample.shape
    depth = w_in.shape[0]
    win, n_heads = cache_win_k.shape[2:4]
    attn_width = n_heads * HEAD_DIM
    n_groups, n_state = ssm_a_re.shape[1:]
    ssm_width = n_groups * SSM_GROUP
    assert w_in.shape[2] == 3 * attn_width + ssm_width and seq == MAX_WINDOW
    dec_rows = dec_batch * dec_seq

    band_bias = _band_bias(rel_bias)
    sample_tabs = _sample_tables(rel_bias, win, dec_seq)
    ar, ai, bb_re, bb_im = _s5_params(ssm_a_re, ssm_a_im, ssm_log_dt, ssm_b_re, ssm_b_im)
    cache_k = jnp.transpose(cache_win_k, (0, 1, 3, 4, 2))
    cache_v = jnp.transpose(cache_win_v, (0, 1, 3, 4, 2))

    rows3 = lambda a: a.reshape(a.shape[0], 1, a.shape[1])
    norm_mix3, norm_ffn3, b_glu3 = rows3(norm_mix), rows3(norm_ffn), rows3(ssm_b_glu)
    g_attn3, g_ssm3 = rows3(norm_attn_out), rows3(norm_ssm_out)
    w_in_b, w_out_b, w_glu_b = w_in.astype(BF16), w_out.astype(BF16), ssm_w_glu.astype(BF16)
    ffn_b = [w.astype(BF16) for w in (ffn_w_gate, ffn_w_up, ffn_w_down)]
    moe_b = [w.astype(BF16) for w in (moe_w_gate, moe_w_up, moe_w_down)]
    tf = ffn_w_gate.shape[2] // 2

    def mix_and_ffn(x, l, attn, y, tm, y_tiles_per_seq):
        g_final = norm_final if l == depth - 1 else None
        mix = functools.partial(_post_mix, attn, y, x, g_attn3, g_ssm3, w_glu_b, b_glu3, w_out_b, l, tm,
                                y_tiles_per_seq)
        if l % 2 == 0:
            return _ffn(mix(), norm_ffn3, *ffn_b, l, g_final, tm, tf)
        n_experts = moe_w_gate.shape[1]
        wr, br = _padded_router(moe_w_router[l // 2], moe_b_router[l // 2])
        if 2 * x.shape[0] >= n_experts * EXPERT_ROW_TILE:
            x = mix()
            hn, route = _router(x, norm_ffn3, wr, br, l, n_experts, tm)
            return _moe_sorted(x, hn, route, *moe_b, l, g_final, tf)
        return _moe(mix(), norm_ffn3, wr, br, *moe_b, l, g_final, tm, tf)

    xp = x_prompt.reshape(batch * seq, d_model)
    xs = x_sample.reshape(dec_rows, d_model)
    zero_state = jnp.zeros((1, batch, 2 * n_groups * n_state), F32)
    k_all = jnp.zeros((depth, batch, attn_width, seq), F32)
    v_all = jnp.zeros((depth, batch, attn_width, seq), F32)
    wx, wc = _pack_s5_weights(bb_re, bb_im, ssm_c_re, ssm_c_im)
    s5_consts = (wx, wc, ar.reshape(depth, 1, -1), ai.reshape(depth, 1, -1), ssm_d.reshape(depth, 1, -1))
    sample_h0 = _pack_state(state_ssm_re, state_ssm_im)
    prompt_h, sample_h, sample_k, sample_v = [], [], [], []
    for l in range(depth):
        qkv, k_all, v_all, u = _norm_proj_prompt(xp, norm_mix3, w_in_b, l, k_all, v_all, PROMPT_ROW_TILE)
        attn = _prompt_attn(qkv, band_bias, batch, seq, attn_width)
        y, h_fin = _s5_scan(u, *s5_consts, zero_state, l, 0, batch, PROMPT_SCAN_STEPS, True)
        xp = mix_and_ffn(xp, l, attn, y, PROMPT_ROW_TILE, seq // PROMPT_ROW_TILE)
        prompt_h.append(h_fin)

        proj = _norm_proj(xs, norm_mix3, w_in_b, l, dec_rows)
        q, k_new, v_new = (proj[:, i * attn_width:(i + 1) * attn_width].reshape(dec_batch, dec_seq, n_heads, HEAD_DIM)
                           for i in range(3))
        attn = _sample_attn(q, k_new, v_new, cache_k, cache_v, l, sample_tabs).reshape(dec_rows, attn_width)
        u = proj[:, 3 * attn_width:].reshape(dec_batch, dec_seq, ssm_width).transpose(1, 0, 2)
        y, h_fin = _s5_scan(u.reshape(dec_rows, ssm_width), *s5_consts, sample_h0, l, l, dec_batch, dec_seq, False)
        y = y.reshape(dec_seq, dec_batch, ssm_width).transpose(1, 0, 2).reshape(dec_rows, ssm_width)
        xs = mix_and_ffn(xs, l, attn, y, dec_rows, None)
        sample_h.append(h_fin)
        sample_k.append(k_new)
        sample_v.append(v_new)

    def window_out(a):
        return jnp.transpose(a.reshape(depth, batch, n_heads, HEAD_DIM, seq), (0, 1, 4, 2, 3))

    return (xp.reshape(batch, seq, d_model), xs.reshape(dec_batch, dec_seq, d_model),
            window_out(k_all), window_out(v_all), *_unpack_state(jnp.stack(prompt_h, 0), n_groups, n_state),
            jnp.stack(sample_k, 0), jnp.stack(sample_v, 0),
            *_unpack_state(jnp.stack(sample_h, 0), n_groups, n_state))
```
